```python
import math
import jax
import jax.numpy as jnp
from jax import lax
import numpy as np


D_MODEL = 1024
BATCH = 16
SEQ = 2048
DEPTH = 2

GRID_W = 64
CTX_LEN = 256
EPS = 1e-6
LOG_FLOOR = 1e-30
N_BRANCH = 4
BRANCH_W = D_MODEL // 4
HEAD_DIM = 64
CHUNK = 64
DN_HEADS = BRANCH_W // HEAD_DIM
DN_CONV = 5
S5_WIDTH = BRANCH_W
S5_GROUP = 16
S5_GROUPS = S5_WIDTH // S5_GROUP
S5_STATE = 64
HG_HEADS = BRANCH_W // HEAD_DIM
ATT_HEADS = BRANCH_W // HEAD_DIM
ATT_KV_HEADS = ATT_HEADS // 2
ATT_GROUP = ATT_HEADS // ATT_KV_HEADS
ATT_BLOCK = 128
ROPE_THETA = 10000.0
D_FF = ((8 * D_MODEL // 3 + 127) // 128) * 128
MACARON_W = 0.5
N_MOD = 9
IN_SIZES = (
    3 * BRANCH_W,
    BRANCH_W,
    2 * DN_HEADS,
    2 * DN_HEADS,
    S5_WIDTH,
    BRANCH_W,
    2 * BRANCH_W,
    BRANCH_W,
    BRANCH_W,
    ATT_HEADS * HEAD_DIM,
    2 * ATT_KV_HEADS * HEAD_DIM,
    N_BRANCH * D_MODEL,
)
IN_COLS = sum(IN_SIZES)

kernel_name = 'hybrid_parallel_flow_block'


def rms_norm(x, g):
    xf = x.astype(jnp.float32)
    y = xf * lax.rsqrt(jnp.mean(xf * xf, axis=-1, keepdims=True) + EPS)
    return (y * g.astype(jnp.float32)).astype(x.dtype)


def modulate(h, shift, scale):
    return h * (1.0 + scale) + shift


def ada_mods(cond, w, b):
    m = jax.nn.silu(cond) @ w + b
    m = m.reshape(m.shape[:-1] + (N_MOD, D_MODEL))
    return [m[..., i, :][..., None, :] for i in range(N_MOD)]


def split_cols(p):
    return jnp.split(p, np.cumsum(IN_SIZES)[:-1].tolist(), axis=-1)


def seq_join(part_c, part_l, reverse):
    if reverse:
        part_c, part_l = jnp.flip(part_c, axis=1), jnp.flip(part_l, axis=1)
    return jnp.concatenate([part_c, part_l], axis=1)


def seq_split(y, n_ctx, reverse):
    y_c, y_l = y[:, :n_ctx], y[:, n_ctx:]
    if reverse:
        y_c, y_l = jnp.flip(y_c, axis=1), jnp.flip(y_l, axis=1)
    return y_c, y_l


def dwconv_centred(x, w):
    k = w.shape[0]
    return lax.conv_general_dilated(x, w[:, None, :].astype(x.dtype), window_strides=(1,), padding=[(k // 2, k // 2)], dimension_numbers=('NWC', 'WIO', 'NWC'), feature_group_count=x.shape[-1])


def l2norm(t):
    return t * lax.rsqrt(jnp.sum(t * t, axis=-1, keepdims=True) + EPS)


def to_chunks(a, n):
    return a.reshape((a.shape[0], n, CHUNK) + a.shape[2:])


def axial_rope_tables(n_tokens):
    rows = n_tokens // GRID_W
    r, col = jnp.meshgrid(jnp.arange(rows), jnp.arange(GRID_W), indexing='ij')
    axis_dim = HEAD_DIM // 2
    inv = ROPE_THETA ** (-jnp.arange(0, axis_dim, 2, dtype=jnp.float32) / axis_dim)
    ang_r = r.reshape(-1, 1).astype(jnp.float32) * inv
    ang_c = col.reshape(-1, 1).astype(jnp.float32) * inv
    ang = jnp.concatenate([ang_r, ang_r, ang_c, ang_c], axis=-1)
    return jnp.cos(ang), jnp.sin(ang)


def apply_axial_rope(x, cos, sin):
    x1, x2, x3, x4 = jnp.split(x, 4, axis=-1)
    rot = jnp.concatenate([-x2, x1, -x4, x3], axis=-1)
    shape = (cos.shape[0],) + (1,) * (x.ndim - 3) + (HEAD_DIM,)
    return x * cos.reshape(shape) + rot * sin.reshape(shape)


def gated_delta_scan(q, k, v, g, beta):
    bsz, t_len, n_heads, dk = q.shape
    dv = v.shape[-1]
    n = t_len // CHUNK
    qc, kc, vc = (jnp.swapaxes(to_chunks(a, n), 2, 3) for a in (q, k, v))
    gc, bc = (jnp.swapaxes(to_chunks(a, n), 2, 3) for a in (g, beta))
    gcum = jnp.cumsum(gc, axis=-1)
    pos = jnp.arange(CHUNK)
    incl = pos[:, None] >= pos[None, :]
    strict = pos[:, None] > pos[None, :]
    decay = jnp.where(incl, jnp.exp(jnp.where(incl, gcum[..., :, None] - gcum[..., None, :], 0.0)), 0.0)
    lower = jnp.where(strict, jnp.einsum('bnhid,bnhjd->bnhij', kc, kc) * decay * bc[..., :, None], 0.0)
    eye = jnp.eye(CHUNK, dtype=q.dtype)
    rhs = jnp.concatenate([vc * bc[..., None], kc * (bc * jnp.exp(gcum))[..., None]], axis=-1)
    sol = lax.linalg.triangular_solve(eye + lower, rhs, left_side=True, lower=True)
    u, w = sol[..., :dv], sol[..., dv:]
    attn = jnp.einsum('bnhid,bnhjd->bnhij', qc, kc) * decay
    q_dec = qc * jnp.exp(gcum)[..., None]
    k_dec = kc * jnp.exp(gcum[..., -1:] - gcum)[..., None]
    g_last = jnp.exp(gcum[..., -1])

    def step(state, xs):
        u_i, w_i, attn_i, qd_i, kd_i, gl_i = xs
        v_new = u_i - jnp.einsum('bhck,bhkv->bhcv', w_i, state)
        o_i = jnp.einsum('bhck,bhkv->bhcv', qd_i, state) + jnp.einsum('bhij,bhjv->bhiv', attn_i, v_new)
        state = state * gl_i[..., None, None] + jnp.einsum('bhck,bhcv->bhkv', kd_i, v_new)
        return state, o_i

    xs = tuple(jnp.moveaxis(a, 1, 0) for a in (u, w, attn, q_dec, k_dec, g_last))
    _, o = lax.scan(step, jnp.zeros((bsz, n_heads, dk, dv), q.dtype), xs)
    return jnp.swapaxes(jnp.moveaxis(o, 0, 1), 2, 3).reshape(bsz, t_len, n_heads, dv)


def hgrn2_scan(q, log_f, k, v):
    bsz, t_len, n_heads, dk = q.shape
    dv = v.shape[-1]
    n = t_len // CHUNK
    bcum = jnp.cumsum(to_chunks(log_f, n), axis=2)
    pos = jnp.arange(CHUNK)
    incl = (pos[:, None] >= pos[None, :])[None, :, :, None, None]

    def step(state, xs):
        q_i, k_i, v_i, b_i = xs
        diff = jnp.where(incl, b_i[:, :, None] - b_i[:, None, :], 0.0)
        w_ts = jnp.where(incl, jnp.exp(diff), 0.0)
        scores = jnp.einsum('bthk,bshk,btshk->bhts', q_i, k_i, w_ts)
        b_last = b_i[:, -1]
        o_i = jnp.einsum('bhts,bshv->bthv', scores, v_i) + jnp.einsum('bthk,bhkv->bthv', q_i * jnp.exp(b_i), state)
        state = state * jnp.exp(b_last)[..., None] + jnp.einsum('bshk,bshv->bhkv', k_i * jnp.exp(b_last[:, None] - b_i), v_i)
        return state, o_i

    xs = tuple(jnp.moveaxis(a, 1, 0) for a in (to_chunks(q, n), to_chunks(k, n), to_chunks(v, n), bcum))
    _, o = lax.scan(step, jnp.zeros((bsz, n_heads, dk, dv), q.dtype), xs)
    return jnp.moveaxis(o, 0, 1).reshape(bsz, t_len, n_heads, dv)


def complex_affine_combine(e1, e2):
    a1r, a1i, b1r, b1i = e1
    a2r, a2i, b2r, b2i = e2
    return (a2r * a1r - a2i * a1i, a2r * a1i + a2i * a1r, a2r * b1r - a2i * b1i + b2r, a2r * b1i + a2i * b1r + b2i)


def s5_scan(u, lam_re, lam_im, log_step, b_re, b_im):
    lam_re, lam_im = lam_re.astype(jnp.float32), lam_im.astype(jnp.float32)
    step = jnp.exp(log_step.astype(jnp.float32))[:, None]
    mag = jnp.exp(lam_re * step)
    a_re, a_im = mag * jnp.cos(lam_im * step), mag * jnp.sin(lam_im * step)
    den = lam_re * lam_re + lam_im * lam_im
    n_re = a_re - 1.0
    coef_re = (n_re * lam_re + a_im * lam_im) / den
    coef_im = (a_im * lam_re - n_re * lam_im) / den
    b_re, b_im = b_re.astype(jnp.float32), b_im.astype(jnp.float32)
    bb_re = coef_re[..., None] * b_re - coef_im[..., None] * b_im
    bb_im = coef_re[..., None] * b_im + coef_im[..., None] * b_re
    x_re = jnp.einsum('btgh,gph->btgp', u, bb_re)
    x_im = jnp.einsum('btgh,gph->btgp', u, bb_im)
    shape = (1, u.shape[1]) + a_re.shape
    _, _, h_re, h_im = lax.associative_scan(complex_affine_combine, (jnp.broadcast_to(a_re, shape), jnp.broadcast_to(a_im, shape), x_re, x_im), axis=1)
    return h_re, h_im


def deltanet_mixer(qkv_c, qkv_l, z_c, z_l, a_c, a_l, b_c, b_l, conv_w, a_log, dt_bias, norm_g):
    dt = qkv_l.dtype
    n_ctx = qkv_c.shape[1]

    def prep(qkv):
        h = jax.nn.silu(dwconv_centred(qkv, conv_w)).astype(jnp.float32)
        h = h.reshape(h.shape[:2] + (3, DN_HEADS, HEAD_DIM))
        return l2norm(h[:, :, 0]) * HEAD_DIM ** -0.5, l2norm(h[:, :, 1]), h[:, :, 2]

    def gates(a, b, d):
        a = a.astype(jnp.float32)[..., d * DN_HEADS:(d + 1) * DN_HEADS]
        b = b.astype(jnp.float32)[..., d * DN_HEADS:(d + 1) * DN_HEADS]
        g = -jnp.exp(a_log[d].astype(jnp.float32)) * jax.nn.softplus(a + dt_bias[d].astype(jnp.float32))
        return g, jax.nn.sigmoid(b)

    qc, kc, vc = prep(qkv_c)
    ql, kl, vl = prep(qkv_l)
    oc, ol = [], []
    for d in range(2):
        rev = d == 1
        g_c, beta_c = gates(a_c, b_c, d)
        g_l, beta_l = gates(a_l, b_l, d)
        o = gated_delta_scan(seq_join(qc, ql, rev), seq_join(kc, kl, rev), seq_join(vc, vl, rev), seq_join(g_c, g_l, rev), seq_join(beta_c, beta_l, rev))
        o_c, o_l = seq_split(o, n_ctx, rev)
        oc.append(o_c)
        ol.append(o_l)

    def out(o, z):
        y = rms_norm(o, norm_g) * jax.nn.silu(z.astype(jnp.float32).reshape(o.shape))
        return y.reshape(o.shape[:2] + (BRANCH_W,)).astype(dt)

    return out(oc[0] + oc[1], z_c), out(ol[0] + ol[1], z_l)


def s5_mixer(u_c, u_l, lam_re, lam_im, log_step, b_re, b_im, c_re, c_im, d_skip, w_glu):
    dt = u_l.dtype
    n_ctx = u_c.shape[1]

    def grp(u):
        return u.astype(jnp.float32).reshape(u.shape[:2] + (S5_GROUPS, S5_GROUP))

    gu_c, gu_l = grp(u_c), grp(u_l)
    c_re, c_im = c_re.astype(jnp.float32), c_im.astype(jnp.float32)
    yc, yl = [], []
    for d in range(2):
        rev = d == 1
        h_re, h_im = s5_scan(seq_join(gu_c, gu_l, rev), lam_re[d], lam_im[d], log_step[d], b_re, b_im)
        y = jnp.einsum('gkp,btgp->btgk', c_re, h_re) - jnp.einsum('gkp,btgp->btgk', c_im, h_im)
        y_c, y_l = seq_split(y, n_ctx, rev)
        yc.append(y_c)
        yl.append(y_l)

    def out(y, u):
        y = (y + d_skip.astype(jnp.float32).reshape(S5_GROUPS, S5_GROUP) * u).reshape(u.shape[:2] + (S5_WIDTH,))
        ab = jax.nn.gelu(y) @ w_glu.astype(jnp.float32)
        return (ab[..., :S5_WIDTH] * jax.nn.sigmoid(ab[..., S5_WIDTH:])).astype(dt)

    return out(yc[0] + yc[1], gu_c), out(yl[0] + yl[1], gu_l)


def hgrn2_mixer(q_c, q_l, f_c, f_l, v_c, v_l, g_c, g_l, lb, norm_g):
    dt = q_l.dtype
    n_ctx = q_c.shape[1]
    lb = lb.astype(jnp.float32).reshape(HG_HEADS, HEAD_DIM)

    def heads(t):
        return t.astype(jnp.float32).reshape(t.shape[:2] + (HG_HEADS, HEAD_DIM))

    def forget(f, d):
        f = heads(f[..., d * BRANCH_W:(d + 1) * BRANCH_W])
        f_gate = lb + (1.0 - lb) * jax.nn.sigmoid(f)
        return jnp.log(jnp.maximum(f_gate, LOG_FLOOR)), (1.0 - lb) * jax.nn.sigmoid(-f)

    qc, ql = jax.nn.silu(heads(q_c)), jax.nn.silu(heads(q_l))
    vc, vl = heads(v_c), heads(v_l)
    oc, ol = [], []
    for d in range(2):
        rev = d == 1
        lf_c, k_c = forget(f_c, d)
        lf_l, k_l = forget(f_l, d)
        o = hgrn2_scan(seq_join(qc, ql, rev), seq_join(lf_c, lf_l, rev), seq_join(k_c, k_l, rev), seq_join(vc, vl, rev))
        o_c, o_l = seq_split(o, n_ctx, rev)
        oc.append(o_c)
        ol.append(o_l)

    def out(o, g):
        y = rms_norm(o, norm_g) * jax.nn.sigmoid(heads(g))
        return y.reshape(o.shape[:2] + (BRANCH_W,)).astype(dt)

    return out(oc[0] + oc[1], g_c), out(ol[0] + ol[1], g_l)


def attention_mixer(q_c, q_l, kv_c, kv_l, qn_g, kn_g, cos, sin):
    dt = q_l.dtype
    scale = HEAD_DIM ** -0.5

    def split_qkv(q, kv):
        bsz, t = q.shape[:2]
        q = rms_norm(q.astype(jnp.float32).reshape(bsz, t, ATT_KV_HEADS, ATT_GROUP, HEAD_DIM), qn_g)
        kv = kv.astype(jnp.float32).reshape(bsz, t, 2, ATT_KV_HEADS, HEAD_DIM)
        return q, rms_norm(kv[:, :, 0], kn_g), kv[:, :, 1]

    qc, kc, vc = split_qkv(q_c, kv_c)
    ql, kl, vl = split_qkv(q_l, kv_l)
    ql, kl = apply_axial_rope(ql, cos, sin), apply_axial_rope(kl, cos, sin)
    p_c = jax.nn.softmax(jnp.einsum('bqkgd,bskd->bkgqs', qc, kc) * scale, axis=-1)
    oc = jnp.einsum('bkgqs,bskd->bqkgd', p_c, vc)
    keys = jnp.concatenate([kc, kl], axis=1)
    vals = jnp.concatenate([vc, vl], axis=1)
    bsz, n_lat = ql.shape[:2]
    qb = jnp.moveaxis(ql.reshape((bsz, n_lat // ATT_BLOCK, ATT_BLOCK) + ql.shape[2:]), 1, 0)

    def attend(q_blk):
        p = jax.nn.softmax(jnp.einsum('bqkgd,bskd->bkgqs', q_blk, keys) * scale, axis=-1)
        return jnp.einsum('bkgqs,bskd->bqkgd', p, vals)

    ol = jnp.moveaxis(lax.map(attend, qb), 0, 1)
    return oc.reshape(oc.shape[:2] + (BRANCH_W,)).astype(dt), ol.reshape(bsz, n_lat, BRANCH_W).astype(dt)


def merge_branches(ys, gates, w_branch, w_out):
    acc = None
    for i, y in enumerate(ys):
        term = jax.nn.sigmoid(gates[..., i * D_MODEL:(i + 1) * D_MODEL]) * (y @ w_branch[i])
        acc = term if acc is None else acc + term
    return acc @ w_out


def swiglu_half_step(x, g, shift, scale, gate, w1, w3, w2):
    h = modulate(rms_norm(x, g), shift, scale)
    return x + MACARON_W * gate * ((jax.nn.silu(h @ w1) * (h @ w3)) @ w2)


def setup_inputs(seed: int = 0) -> dict:
    key = jax.random.key(seed)
    ks = jax.random.split(key, 32)

    def nrm(i, shape, s):
        return s * jax.random.normal(ks[i], shape, jnp.float32)

    def unif(i, shape, lo, hi):
        return jax.random.uniform(ks[i], shape, jnp.float32, lo, hi)

    L = DEPTH
    dt0 = jnp.exp(unif(13, (L, 2, DN_HEADS), math.log(1e-3), math.log(1e-1)))
    return {
        'x': nrm(0, (BATCH, SEQ, D_MODEL), 1.0),
        'c': nrm(1, (BATCH, D_MODEL), 1.0),
        'ctx': nrm(2, (BATCH, CTX_LEN, D_MODEL), 1.0),
        'c_ctx': nrm(3, (D_MODEL,), 1.0),
        'ada_w': nrm(4, (L, D_MODEL, N_MOD * D_MODEL), 0.5 * D_MODEL ** -0.5),
        'ada_b': nrm(5, (L, N_MOD * D_MODEL), 0.02),
        'norm_g': 1.0 + nrm(6, (L, 3, D_MODEL), 0.05),
        'ffn_w1': nrm(7, (L, 2, D_MODEL, D_FF), D_MODEL ** -0.5),
        'ffn_w3': nrm(8, (L, 2, D_MODEL, D_FF), D_MODEL ** -0.5),
        'ffn_w2': nrm(9, (L, 2, D_FF, D_MODEL), D_FF ** -0.5),
        'w_in': nrm(10, (L, D_MODEL, IN_COLS), D_MODEL ** -0.5),
        'dn_conv': nrm(11, (L, DN_CONV, 3 * BRANCH_W), DN_CONV ** -0.5),
        'dn_a_log': jnp.log(unif(12, (L, 2, DN_HEADS), 1.0, 16.0)),
        'dn_dt_bias': dt0 + jnp.log(-jnp.expm1(-dt0)),
        'dn_norm_g': 1.0 + nrm(14, (L, HEAD_DIM), 0.05),
        's5_lam_re': -0.5 + nrm(15, (L, 2, S5_GROUPS, S5_STATE), 0.01),
        's5_lam_im': jnp.pi * jnp.arange(S5_STATE, dtype=jnp.float32) + nrm(16, (L, 2, S5_GROUPS, S5_STATE), 0.01),
        's5_log_step': unif(17, (L, 2, S5_GROUPS), math.log(1e-3), math.log(1e-1)),
        's5_b_re': nrm(18, (L, S5_GROUPS, S5_STATE, S5_GROUP), (2 * S5_GROUP) ** -0.5),
        's5_b_im': nrm(19, (L, S5_GROUPS, S5_STATE, S5_GROUP), (2 * S5_GROUP) ** -0.5),
        's5_c_re': nrm(20, (L, S5_GROUPS, S5_GROUP, S5_STATE), S5_STATE ** -0.5),
        's5_c_im': nrm(21, (L, S5_GROUPS, S5_GROUP, S5_STATE), S5_STATE ** -0.5),
        's5_d': nrm(22, (L, S5_WIDTH), 1.0),
        's5_glu': nrm(23, (L, S5_WIDTH, 2 * S5_WIDTH), S5_WIDTH ** -0.5),
        'hg_lb_logits': nrm(24, (L, HG_HEADS * HEAD_DIM), 0.1),
        'hg_norm_g': 1.0 + nrm(25, (L, HEAD_DIM), 0.05),
        'at_qn_g': 1.0 + nrm(26, (L, HEAD_DIM), 0.05),
        'at_kn_g': 1.0 + nrm(27, (L, HEAD_DIM), 0.05),
        'w_branch': nrm(28, (L, N_BRANCH, BRANCH_W, D_MODEL), BRANCH_W ** -0.5),
        'w_out': nrm(29, (L, D_MODEL, D_MODEL), D_MODEL ** -0.5),
        'final_g': 1.0 + nrm(30, (D_MODEL,), 0.05),
    }


def reference(x, c, ctx, c_ctx, ada_w, ada_b, norm_g, ffn_w1, ffn_w3, ffn_w2, w_in, dn_conv, dn_a_log, dn_dt_bias, dn_norm_g, s5_lam_re, s5_lam_im, s5_log_step, s5_b_re, s5_b_im, s5_c_re, s5_c_im, s5_d, s5_glu, hg_lb_logits, hg_norm_g, at_qn_g, at_kn_g, w_branch, w_out, final_g):
    cos, sin = axial_rope_tables(x.shape[1])
    lb_w = jax.nn.softmax(hg_lb_logits.astype(jnp.float32), axis=0)
    lower_bounds = jnp.cumsum(lb_w, axis=0) - lb_w[0:1]
    x_l, x_c = x, ctx
    for l in range(DEPTH):
        m_l = ada_mods(c, ada_w[l], ada_b[l])
        m_c = ada_mods(c_ctx, ada_w[l], ada_b[l])
        x_l = swiglu_half_step(x_l, norm_g[l, 0], m_l[0], m_l[1], m_l[2], ffn_w1[l, 0], ffn_w3[l, 0], ffn_w2[l, 0])
        x_c = swiglu_half_step(x_c, norm_g[l, 0], m_c[0], m_c[1], m_c[2], ffn_w1[l, 0], ffn_w3[l, 0], ffn_w2[l, 0])
        p_l = split_cols(modulate(rms_norm(x_l, norm_g[l, 1]), m_l[3], m_l[4]) @ w_in[l])
        p_c = split_cols(modulate(rms_norm(x_c, norm_g[l, 1]), m_c[3], m_c[4]) @ w_in[l])
        y_dn = deltanet_mixer(p_c[0], p_l[0], p_c[1], p_l[1], p_c[2], p_l[2], p_c[3], p_l[3], dn_conv[l], dn_a_log[l], dn_dt_bias[l], dn_norm_g[l])
        y_s5 = s5_mixer(p_c[4], p_l[4], s5_lam_re[l], s5_lam_im[l], s5_log_step[l], s5_b_re[l], s5_b_im[l], s5_c_re[l], s5_c_im[l], s5_d[l], s5_glu[l])
        y_hg = hgrn2_mixer(p_c[5], p_l[5], p_c[6], p_l[6], p_c[7], p_l[7], p_c[8], p_l[8], lower_bounds[l], hg_norm_g[l])
        y_at = attention_mixer(p_c[9], p_l[9], p_c[10], p_l[10], at_qn_g[l], at_kn_g[l], cos, sin)
        x_l = x_l + m_l[5] * merge_branches([y_dn[1], y_s5[1], y_hg[1], y_at[1]], p_l[11], w_branch[l], w_out[l])
        x_l = swiglu_half_step(x_l, norm_g[l, 2], m_l[6], m_l[7], m_l[8], ffn_w1[l, 1], ffn_w3[l, 1], ffn_w2[l, 1])
        if l < DEPTH - 1:
            x_c = x_c + m_c[5] * merge_branches([y_dn[0], y_s5[0], y_hg[0], y_at[0]], p_c[11], w_branch[l], w_out[l])
            x_c = swiglu_half_step(x_c, norm_g[l, 2], m_c[6], m_c[7], m_c[8], ffn_w1[l, 1], ffn_w3[l, 1], ffn_w2[l, 1])
    return rms_norm(x_l, final_g)
```

```python
import functools
import math

import numpy as np
import jax
import jax.numpy as jnp
from jax import lax
from jax.experimental import pallas as pl
from jax.experimental.pallas import tpu as pltpu

F32 = jnp.float32
BF16 = jnp.bfloat16

D_MODEL = 1024
N_MOD = 9
BRANCH_W = 256
HEAD_DIM = 64
N_HEADS = 4
CHUNK = 64
EPS = 1e-6
LOG_FLOOR = 1e-30
GRID_W = 64
ROPE_THETA = 10000.0
FF_CHUNK = 256
S5_GROUPS = 16
S5_GROUP = 16
S5_STATE = 64
S5_LANES = S5_GROUPS * S5_STATE
NEG_BIG = -1e30
V7X_VMEM_LIMIT = 56 * 1024 * 1024


def _bf(x):
    return x.astype(BF16)


def _dot(a, b):
    return jnp.dot(a, b, preferred_element_type=F32)


def _dot_nt(a, b):
    return lax.dot_general(a, b, (((1,), (1,)), ((), ())), preferred_element_type=F32)


def _dot_tn(a, b):
    return lax.dot_general(a, b, (((0,), (0,)), ((), ())), preferred_element_type=F32)


def _split2(x):
    hi = _bf(x)
    lo = _bf(x - hi.astype(F32))
    return hi, lo


def _split3(x):
    x1 = _bf(x)
    r1 = x - x1.astype(F32)
    x2 = _bf(r1)
    x3 = _bf(r1 - x2.astype(F32))
    return x1, x2, x3


def _dot_x2(x, m):
    hi, lo = _split2(x)
    return _dot(hi, m) + _dot(lo, m)


def _dot_x3(x, m):
    x1, x2, x3 = _split3(x)
    return _dot(x1, m) + _dot(x2, m) + _dot(x3, m)


def _dot3_left(m, x):
    x1, x2, x3 = _split3(x)
    return _dot(m, x1) + _dot(m, x2) + _dot(m, x3)


def _dot_hp(a, b):
    a_hi, a_lo = _split2(a)
    b_hi, b_lo = _split2(b)
    return _dot(a_hi, b_hi) + _dot(a_hi, b_lo) + _dot(a_lo, b_hi)


def _sigmoid(x):
    return jax.nn.sigmoid(x)


def _silu(x):
    return x * jax.nn.sigmoid(x)


def _rms(x):
    return x * lax.rsqrt(jnp.mean(x * x, axis=-1, keepdims=True) + EPS)


def _head_rms(x, bd_ones, gain):
    ss = _dot_x2(x * x, bd_ones) * (1.0 / HEAD_DIM)
    return x * lax.rsqrt(ss + EPS) * gain


def _stack4(x):
    return jnp.concatenate([x, x, x, x], axis=0)


def _unstack4(x):
    n = x.shape[0] // 4
    return x[0:n] + x[n:2 * n] + x[2 * n:3 * n] + x[3 * n:4 * n]


def _np_block_diag_ones(n, blk):
    i = np.arange(n)
    return (i[:, None] // blk == i[None, :] // blk).astype(np.float32)


def _np_tri(n, blk, lower, strict):
    i = np.arange(n)
    same = i[:, None] // blk == i[None, :] // blk
    a, b = i[:, None] % blk, i[None, :] % blk
    if lower:
        t = (a > b) if strict else (a >= b)
    else:
        t = (a < b) if strict else (a <= b)
    return (same & t).astype(np.float32)


def _np_doubling_masks(n, blk):
    i = np.arange(n)
    r, c = i[:, None], i[None, :]
    out = [(r // 2 == c // 2)]
    s = 2
    while s < blk:
        out.append((r // (2 * s) == c // (2 * s)) & (r // s != c // s))
        s *= 2
    return np.stack(out).astype(np.float32)


def _bwd_order(s, n_ctx, n_all):
    return jnp.where(s < n_ctx, n_ctx - 1 - s, n_all + n_ctx - 1 - s)


def _ada_kernel(c_ref, w_ref, b_ref, o_ref):
    s = _silu(c_ref[...])
    w = w_ref[...]
    s_hi, s_lo = _split2(s)
    w_hi, w_lo = _split2(w)
    o_ref[...] = _dot(s_hi, w_hi) + _dot(s_hi, w_lo) + _dot(s_lo, w_hi) + b_ref[...]


def _ada_mods(cond, ada_w, ada_b):
    n_layers = ada_w.shape[0]
    r = cond.shape[0]
    n_out = ada_w.shape[2]
    tn = 1152
    out = pl.pallas_call(
        _ada_kernel,
        grid=(n_layers, n_out // tn),
        in_specs=[
            pl.BlockSpec((r, D_MODEL), lambda l, n: (0, 0)),
            pl.BlockSpec((None, D_MODEL, tn), lambda l, n: (l, 0, n)),
            pl.BlockSpec((None, 1, tn), lambda l, n: (l, 0, n)),
        ],
        out_specs=pl.BlockSpec((None, r, tn), lambda l, n: (l, 0, n)),
        out_shape=jax.ShapeDtypeStruct((n_layers, r, n_out), F32),
        compiler_params=pltpu.CompilerParams(dimension_semantics=("parallel", "parallel"),
                                             vmem_limit_bytes=40 * 1024 * 1024),
        name="ada_mods",
    )(cond, ada_w, ada_b.reshape(n_layers, 1, n_out))
    return out.reshape(n_layers, r, N_MOD, D_MODEL)


def _ffn_kernel(*refs, mod_base, final):
    if final:
        x_ref, m_ref, g_ref, w1_ref, w3_ref, w2_ref, fg_ref, o_ref, h_ref, acc_ref = refs
    else:
        x_ref, m_ref, g_ref, w1_ref, w3_ref, w2_ref, o_ref, h_ref, acc_ref = refs
    x = x_ref[...]
    shift = m_ref[mod_base:mod_base + 1, :]
    scale = m_ref[mod_base + 1:mod_base + 2, :]
    gate = m_ref[mod_base + 2:mod_base + 3, :]
    h_ref[...] = _bf(_rms(x) * g_ref[...] * (1.0 + scale) + shift)
    acc_ref[...] = jnp.zeros_like(acc_ref)

    def body(c, carry):
        h = h_ref[...]
        a = _dot(h, w1_ref[c])
        b = _dot(h, w3_ref[c])
        acc_ref[...] += _dot(_bf(_silu(a) * b), w2_ref[c])
        return carry

    lax.fori_loop(0, w1_ref.shape[0], body, 0)
    out = x + 0.5 * gate * acc_ref[...]
    if final:
        out = _rms(out) * fg_ref[...]
    o_ref[...] = out


def _ffn_half_step(xj, mods, g, w1c, w3c, w2c, *, mod_base, tm, nct, final_g=None):
    bsz, t_all, _ = xj.shape
    final = final_g is not None

    def tok_map(b, j):
        return (b, j, 0)

    def mod_map(b, j):
        return (jnp.where(j < nct, bsz, b), 0, 0)

    const3 = lambda b, j: (0, 0, 0)
    const2 = lambda b, j: (0, 0)
    in_specs = [
        pl.BlockSpec((None, tm, D_MODEL), tok_map),
        pl.BlockSpec((None, N_MOD, D_MODEL), mod_map),
        pl.BlockSpec((1, D_MODEL), const2),
        pl.BlockSpec(w1c.shape, const3),
        pl.BlockSpec(w3c.shape, const3),
        pl.BlockSpec(w2c.shape, const3),
    ]
    args = [xj, mods, g.reshape(1, D_MODEL), w1c, w3c, w2c]
    if final:
        in_specs.append(pl.BlockSpec((1, D_MODEL), const2))
        args.append(final_g.reshape(1, D_MODEL))
    return pl.pallas_call(
        functools.partial(_ffn_kernel, mod_base=mod_base, final=final),
        grid=(bsz, t_all // tm),
        in_specs=in_specs,
        out_specs=pl.BlockSpec((None, tm, D_MODEL), tok_map),
        out_shape=jax.ShapeDtypeStruct(xj.shape, F32),
        scratch_shapes=[pltpu.VMEM((tm, D_MODEL), BF16), pltpu.VMEM((tm, D_MODEL), F32)],
        compiler_params=pltpu.CompilerParams(dimension_semantics=("parallel", "parallel"),
                                             vmem_limit_bytes=V7X_VMEM_LIMIT),
        name="ffn_half_step",
    )(*args)


def _inproj_kernel(x_ref, m_ref, g_ref, wdn_ref, ws5_ref, whg_ref, waq_ref, wkv_ref,
                   dn_ref, s5_ref, hg_ref, aq_ref, kv_ref):
    x = x_ref[...]
    h = _bf(_rms(x) * g_ref[...] * (1.0 + m_ref[4:5, :]) + m_ref[3:4, :])
    dn_ref[...] = _dot(h, wdn_ref[...])
    s5_ref[...] = _dot(h, ws5_ref[...])
    hg_ref[...] = _dot(h, whg_ref[...])
    aq_ref[...] = _dot(h, waq_ref[...])
    kv_ref[...] = _dot(h, wkv_ref[...])


def _input_projection(xj, mods, g, ws, *, tm, nct):
    bsz, t_all, _ = xj.shape
    tok_map = lambda b, j: (b, j, 0)
    mod_map = lambda b, j: (jnp.where(j < nct, bsz, b), 0, 0)
    const2 = lambda b, j: (0, 0)
    widths = [w.shape[1] for w in ws]
    return pl.pallas_call(
        _inproj_kernel,
        grid=(bsz, t_all // tm),
        in_specs=[pl.BlockSpec((None, tm, D_MODEL), tok_map),
                  pl.BlockSpec((None, N_MOD, D_MODEL), mod_map),
                  pl.BlockSpec((1, D_MODEL), const2)]
                 + [pl.BlockSpec(w.shape, const2) for w in ws],
        out_specs=[pl.BlockSpec((None, tm, n), tok_map) for n in widths],
        out_shape=[jax.ShapeDtypeStruct((bsz, t_all, n), F32) for n in widths],
        compiler_params=pltpu.CompilerParams(dimension_semantics=("parallel", "parallel"),
                                             vmem_limit_bytes=V7X_VMEM_LIMIT),
        name="input_projection",
    )(xj, mods, g.reshape(1, D_MODEL), *ws)


def _dn_prep_kernel(p_ref, prev_ref, next_ref, cw_ref, al_ref, dtb_ref, e_ref, tril_ref, triu_ref, bd_ref,
                    q_ref, k_ref, v_ref, gb_ref, *, tm, nct, nt):
    j = pl.program_id(1)
    first = jnp.logical_or(j == 0, j == nct)
    last = jnp.logical_or(j == nct - 1, j == nt - 1)
    x = p_ref[:, 0:3 * BRANCH_W]
    pv = prev_ref[...] * jnp.where(first, 0.0, 1.0)
    nx = next_ref[...] * jnp.where(last, 0.0, 1.0)
    ext = jnp.concatenate([pv, x, nx], axis=0)
    n = tm + 16
    acc = jnp.zeros((tm, 3 * BRANCH_W), F32)
    for t in range(5):
        sh = (2 - t) % n
        r = ext if sh == 0 else pltpu.roll(ext, sh, 0)
        acc = acc + r[8:8 + tm] * cw_ref[t:t + 1, :]
    h = _silu(acc)
    bd = bd_ref[...]

    def l2n(t):
        return t * lax.rsqrt(_dot_x2(t * t, bd) + EPS)

    q_ref[...] = l2n(h[:, 0:BRANCH_W]) * (HEAD_DIM ** -0.5)
    k_ref[...] = l2n(h[:, BRANCH_W:2 * BRANCH_W])
    v_ref[...] = h[:, 2 * BRANCH_W:3 * BRANCH_W]

    ab = p_ref[:, 4 * BRANCH_W:4 * BRANCH_W + 128]
    z = ab + dtb_ref[...]
    softplus = jnp.maximum(z, 0.0) + jnp.log1p(jnp.exp(-jnp.abs(z)))
    g = -jnp.exp(al_ref[...]) * softplus
    lane = lax.broadcasted_iota(jnp.int32, ab.shape, 1)
    g = jnp.where(lane < 8, g, 0.0)
    cs_f = _dot3_left(tril_ref[...], g)
    cs_b = _dot3_left(triu_ref[...], g)
    comp = jnp.where(lane < 4, cs_f, jnp.where(lane < 8, cs_b, jnp.where(lane < 16, _sigmoid(ab), 0.0)))
    gb_ref[...] = _dot_x3(comp, e_ref[...])


def _dn_chunk(q, k, v, gc, be, s_prev, bd, tri_incl, tri_strict, it, eye, lvl_ref, fwd):
    ks = _bf(_stack4(k) * bd)
    gram = _dot_nt(ks, ks)
    row_gc = _stack4(gc)
    col_gc = jnp.sum(gc * it, axis=0, keepdims=True)
    decay = jnp.exp(jnp.where(tri_incl > 0.5, row_gc - col_gc, NEG_BIG))
    a = gram * decay * _stack4(be) * tri_strict
    t = eye - a * lvl_ref[0]
    for lv in range(1, lvl_ref.shape[0]):
        x = _dot_hp(a * lvl_ref[lv], t)
        t = t - _dot_hp(t, x)
    egc = jnp.exp(gc)
    tb = _bf(t)
    u = _unstack4(_dot(tb, _bf(_stack4(v * be) * bd)))
    w = _unstack4(_dot(tb, _bf(_stack4(k * be * egc) * bd)))
    attn = _dot_nt(_bf(_stack4(q) * bd), ks) * decay
    total = gc[CHUNK - 1:CHUNK, :] if fwd else gc[0:1, :]
    sb = _bf(s_prev)
    v_new = u - _dot(_bf(w), sb)
    o = _dot(_bf(q * egc), sb) + _unstack4(_dot(_bf(attn), _bf(_stack4(v_new) * bd)))
    k_dec = k * jnp.exp(total - gc)
    s_new = s_prev * jnp.exp(total) + _dot_tn(_bf(k_dec), _bf(v_new)) * bd
    return o, s_new


def _dn_scan_kernel(qf_ref, kf_ref, vf_ref, gcf_ref, bef_ref, qb_ref, kb_ref, vb_ref, gcb_ref, beb_ref,
                    bd_ref, li_ref, ls_ref, ui_ref, us_ref, it_ref, eye_ref, lvl_ref,
                    of_ref, ob_ref, s_ref):
    @pl.when(pl.program_id(1) == 0)
    def _():
        s_ref[...] = jnp.zeros_like(s_ref)

    bd, it, eye = bd_ref[...], it_ref[...], eye_ref[...]
    o_f, s_f = _dn_chunk(qf_ref[...], kf_ref[...], vf_ref[...], gcf_ref[...], bef_ref[...], s_ref[0],
                         bd, li_ref[...], ls_ref[...], it, eye, lvl_ref, True)
    o_b, s_b = _dn_chunk(qb_ref[...], kb_ref[...], vb_ref[...], gcb_ref[...], beb_ref[...], s_ref[1],
                         bd, ui_ref[...], us_ref[...], it, eye, lvl_ref, False)
    of_ref[...] = o_f
    ob_ref[...] = o_b
    s_ref[0] = s_f
    s_ref[1] = s_b


def _deltanet(p_dn, conv_w, a_log, dt_bias, *, tm, tc):
    bsz, t_all, _ = p_dn.shape
    nt, nct = t_all // tm, tc // tm
    al = jnp.zeros((1, 128), F32).at[0, 0:8].set(a_log.reshape(8).astype(F32))
    dtb = jnp.zeros((1, 128), F32).at[0, 0:8].set(dt_bias.reshape(8).astype(F32))
    e = np.zeros((128, 4 * BRANCH_W), np.float32)
    for jj in range(4):
        for h in range(N_HEADS):
            e[4 * jj + h, jj * BRANCH_W + h * HEAD_DIM: jj * BRANCH_W + (h + 1) * HEAD_DIM] = 1.0
    consts = [jnp.asarray(e, BF16),
              jnp.asarray(_np_tri(tm, CHUNK, True, False), BF16),
              jnp.asarray(_np_tri(tm, CHUNK, False, False), BF16),
              jnp.asarray(_np_block_diag_ones(BRANCH_W, HEAD_DIM), BF16)]
    tok = lambda b, j: (b, j, 0)
    c2 = lambda b, j: (0, 0)
    hb = tm // 8
    q, k, v, gb = pl.pallas_call(
        functools.partial(_dn_prep_kernel, tm=tm, nct=nct, nt=nt),
        grid=(bsz, nt),
        in_specs=[pl.BlockSpec((None, tm, p_dn.shape[2]), tok),
                  pl.BlockSpec((None, 8, 3 * BRANCH_W), lambda b, j: (b, jnp.maximum(j * hb - 1, 0), 0)),
                  pl.BlockSpec((None, 8, 3 * BRANCH_W), lambda b, j: (b, jnp.minimum((j + 1) * hb, nt * hb - 1), 0)),
                  pl.BlockSpec(conv_w.shape, c2), pl.BlockSpec((1, 128), c2), pl.BlockSpec((1, 128), c2)]
                 + [pl.BlockSpec(c.shape, c2) for c in consts],
        out_specs=[pl.BlockSpec((None, tm, BRANCH_W), tok)] * 3 + [pl.BlockSpec((None, tm, 4 * BRANCH_W), tok)],
        out_shape=[jax.ShapeDtypeStruct((bsz, t_all, BRANCH_W), F32)] * 3
                  + [jax.ShapeDtypeStruct((bsz, t_all, 4 * BRANCH_W), F32)],
        compiler_params=pltpu.CompilerParams(dimension_semantics=("parallel", "parallel"),
                                             vmem_limit_bytes=40 * 1024 * 1024),
        name="deltanet_prep",
    )(p_dn, p_dn, p_dn, conv_w.astype(F32), al, dtb, *consts)

    nc, ncc = t_all // CHUNK, tc // CHUNK
    n4 = N_HEADS * CHUNK
    masks = [jnp.asarray(_np_block_diag_ones(n4, CHUNK), F32),
             jnp.asarray(_np_tri(n4, CHUNK, True, False), F32),
             jnp.asarray(_np_tri(n4, CHUNK, True, True), F32),
             jnp.asarray(_np_tri(n4, CHUNK, False, False), F32),
             jnp.asarray(_np_tri(n4, CHUNK, False, True), F32),
             jnp.asarray(np.tile(np.eye(CHUNK, dtype=np.float32), (1, N_HEADS)), F32),
             jnp.asarray(np.eye(n4, dtype=np.float32), F32),
             jnp.asarray(_np_doubling_masks(n4, CHUNK), F32)]
    cmap = lambda m: (lambda b, s: (0,) * m.ndim)
    blk = (None, CHUNK, BRANCH_W)
    fmap = lambda col: (lambda b, s: (b, s, col))
    bmap = lambda col: (lambda b, s: (b, _bwd_order(s, ncc, nc), col))
    o_f, o_b = pl.pallas_call(
        _dn_scan_kernel,
        grid=(bsz, nc),
        in_specs=[pl.BlockSpec(blk, fmap(0))] * 3 + [pl.BlockSpec(blk, fmap(0)), pl.BlockSpec(blk, fmap(2))]
                 + [pl.BlockSpec(blk, bmap(0))] * 3 + [pl.BlockSpec(blk, bmap(1)), pl.BlockSpec(blk, bmap(3))]
                 + [pl.BlockSpec(m.shape, cmap(m)) for m in masks],
        out_specs=[pl.BlockSpec(blk, fmap(0)), pl.BlockSpec(blk, bmap(0))],
        out_shape=[jax.ShapeDtypeStruct((bsz, t_all, BRANCH_W), F32)] * 2,
        scratch_shapes=[pltpu.VMEM((2, n4, n4), F32)],
        compiler_params=pltpu.CompilerParams(dimension_semantics=("parallel", "arbitrary"),
                                             vmem_limit_bytes=40 * 1024 * 1024),
        name="deltanet_scan",
    )(q, k, v, gb, gb, q, k, v, gb, gb, *masks)
    return o_f, o_b


def _hg_chunk(q, f, v, lb, tri, st_prev, bd_ones, bd_mask, acc_ref, tmp_ref, fwd):
    qs = _silu(q)
    f_gate = lb + (1.0 - lb) * _sigmoid(f)
    logf = jnp.log(jnp.maximum(f_gate, LOG_FLOOR))
    kk = (1.0 - lb) * _sigmoid(-f)
    b = _dot3_left(tri, logf)
    b_last = b[CHUNK - 1:CHUNK, :] if fwd else b[0:1, :]
    acc_ref[...] = _dot_nt(_bf(qs * jnp.exp(b)), _bf(st_prev))
    tmp_ref[0], tmp_ref[1], tmp_ref[2], tmp_ref[3] = qs, kk, b, v
    for g in range(CHUNK // 8):
        lo, hi = (8 * g, CHUNK) if fwd else (0, 8 * g + 8)
        nr = hi - lo
        qg, bg = tmp_ref[0, lo:hi, :], tmp_ref[2, lo:hi, :]
        tg = lax.broadcasted_iota(jnp.int32, (nr, BRANCH_W), 0) + lo
        rows = []
        for s in range(8 * g, 8 * g + 8):
            m = (tg >= s) if fwd else (tg <= s)
            rows.append(qg * tmp_ref[1, s:s + 1, :] * jnp.exp(jnp.where(m, bg - tmp_ref[2, s:s + 1, :], NEG_BIG)))
        se = _dot(_bf(jnp.concatenate(rows, axis=0)), bd_ones)
        contrib = se[0:nr] * tmp_ref[3, 8 * g:8 * g + 1, :]
        for i in range(1, 8):
            contrib = contrib + se[i * nr:(i + 1) * nr] * tmp_ref[3, 8 * g + i:8 * g + i + 1, :]
        acc_ref[lo:hi, :] += contrib
    k_dec = kk * jnp.exp(b_last - b)
    st_new = st_prev * jnp.exp(b_last) + _dot_tn(_bf(v), _bf(k_dec)) * bd_mask
    return st_new


def _hg_scan_kernel(qf_ref, ff_ref, vf_ref, qb_ref, fb_ref, vb_ref, lg_ref, bdo_ref, bdm_ref, tl_ref, tu_ref,
                    of_ref, ob_ref, st_ref, tmp_ref, *, layer):
    @pl.when(pl.program_id(1) == 0)
    def _():
        st_ref[...] = jnp.zeros_like(st_ref)

    n_layers = lg_ref.shape[0]
    mx = lg_ref[0:1, :]
    for i in range(1, n_layers):
        mx = jnp.maximum(mx, lg_ref[i:i + 1, :])
    den = jnp.zeros_like(mx)
    num = jnp.zeros_like(mx)
    for i in range(n_layers):
        e = jnp.exp(lg_ref[i:i + 1, :] - mx)
        den = den + e
        if 1 <= i <= layer:
            num = num + e
    lb = num / den
    bdo, bdm = bdo_ref[...], bdm_ref[...]
    st_ref[0] = _hg_chunk(qf_ref[...], ff_ref[...], vf_ref[...], lb, tl_ref[...], st_ref[0], bdo, bdm, of_ref,
                          tmp_ref.at[0], True)
    st_ref[1] = _hg_chunk(qb_ref[...], fb_ref[...], vb_ref[...], lb, tu_ref[...], st_ref[1], bdo, bdm, ob_ref,
                          tmp_ref.at[1], False)


def _hgrn2(p_hg, lb_logits, layer, *, tc):
    bsz, t_all, _ = p_hg.shape
    nc, ncc = t_all // CHUNK, tc // CHUNK
    consts = [jnp.asarray(_np_block_diag_ones(BRANCH_W, HEAD_DIM), BF16),
              jnp.asarray(_np_block_diag_ones(BRANCH_W, HEAD_DIM), F32),
              jnp.asarray(_np_tri(CHUNK, CHUNK, True, False), BF16),
              jnp.asarray(_np_tri(CHUNK, CHUNK, False, False), BF16)]
    blk = (None, CHUNK, BRANCH_W)
    c2 = lambda b, s: (0, 0)
    fmap = lambda col: (lambda b, s: (b, s, col))
    bmap = lambda col: (lambda b, s: (b, _bwd_order(s, ncc, nc), col))
    return pl.pallas_call(
        functools.partial(_hg_scan_kernel, layer=layer),
        grid=(bsz, nc),
        in_specs=[pl.BlockSpec(blk, fmap(0)), pl.BlockSpec(blk, fmap(1)), pl.BlockSpec(blk, fmap(3)),
                  pl.BlockSpec(blk, bmap(0)), pl.BlockSpec(blk, bmap(2)), pl.BlockSpec(blk, bmap(3)),
                  pl.BlockSpec(lb_logits.shape, c2)] + [pl.BlockSpec(c.shape, c2) for c in consts],
        out_specs=[pl.BlockSpec(blk, fmap(0)), pl.BlockSpec(blk, bmap(0))],
        out_shape=[jax.ShapeDtypeStruct((bsz, t_all, BRANCH_W), F32)] * 2,
        scratch_shapes=[pltpu.VMEM((2, BRANCH_W, BRANCH_W), F32), pltpu.VMEM((2, 4, CHUNK, BRANCH_W), F32)],
        compiler_params=pltpu.CompilerParams(dimension_semantics=("parallel", "arbitrary"),
                                             vmem_limit_bytes=40 * 1024 * 1024),
        name="hgrn2_scan",
    )(p_hg, p_hg, p_hg, p_hg, p_hg, p_hg, lb_logits.astype(F32), *consts)


def _s5_kernel(uf_ref, ub_ref, bb_ref, cc_ref, a_ref, yf_ref, yb_ref, x_ref, h_ref, *, tt, bsz):
    @pl.when(pl.program_id(0) == 0)
    def _():
        h_ref[...] = jnp.zeros_like(h_ref)

    x_ref[0] = _dot(_bf(uf_ref[...]), bb_ref[0])
    x_ref[1] = _dot(_bf(ub_ref[...]), bb_ref[1])

    def step(i, carry):
        hfr, hfi, hbr, hbi = carry
        rf = pl.multiple_of(i * bsz, bsz)
        rb = pl.multiple_of((tt - 1 - i) * bsz, bsz)
        ar, ai = a_ref[0, 0], a_ref[0, 1]
        xr = x_ref[0, pl.ds(rf, bsz), 0:S5_LANES]
        xi = x_ref[0, pl.ds(rf, bsz), S5_LANES:2 * S5_LANES]
        nfr = ar * hfr - ai * hfi + xr
        nfi = ar * hfi + ai * hfr + xi
        x_ref[0, pl.ds(rf, bsz), 0:S5_LANES] = nfr
        x_ref[0, pl.ds(rf, bsz), S5_LANES:2 * S5_LANES] = nfi
        ar, ai = a_ref[1, 0], a_ref[1, 1]
        xr = x_ref[1, pl.ds(rb, bsz), 0:S5_LANES]
        xi = x_ref[1, pl.ds(rb, bsz), S5_LANES:2 * S5_LANES]
        nbr = ar * hbr - ai * hbi + xr
        nbi = ar * hbi + ai * hbr + xi
        x_ref[1, pl.ds(rb, bsz), 0:S5_LANES] = nbr
        x_ref[1, pl.ds(rb, bsz), S5_LANES:2 * S5_LANES] = nbi
        return nfr, nfi, nbr, nbi

    hs = lax.fori_loop(0, tt, step, (h_ref[0, 0], h_ref[0, 1], h_ref[1, 0], h_ref[1, 1]))
    h_ref[0, 0], h_ref[0, 1], h_ref[1, 0], h_ref[1, 1] = hs
    yf_ref[...] = _dot(_bf(x_ref[0]), cc_ref[0])
    yb_ref[...] = _dot(_bf(x_ref[1]), cc_ref[1])


def _s5_matrices(lam_re, lam_im, log_step, b_re, b_im, c_re, c_im):
    lam_re, lam_im = lam_re.astype(F32), lam_im.astype(F32)
    step = jnp.exp(log_step.astype(F32))[..., None]
    mag = jnp.exp(lam_re * step)
    a_re, a_im = mag * jnp.cos(lam_im * step), mag * jnp.sin(lam_im * step)
    den = lam_re * lam_re + lam_im * lam_im
    n_re = a_re - 1.0
    coef_re = (n_re * lam_re + a_im * lam_im) / den
    coef_im = (a_im * lam_re - n_re * lam_im) / den
    b_re, b_im = b_re.astype(F32), b_im.astype(F32)
    bb_re = coef_re[..., None] * b_re - coef_im[..., None] * b_im
    bb_im = coef_re[..., None] * b_im + coef_im[..., None] * b_re
    eye_g = jnp.eye(S5_GROUPS, dtype=F32)
    bb = jnp.stack([bb_re, bb_im], axis=1)
    bb = jnp.einsum('drgph,gk->dghrkp', bb, eye_g).reshape(2, BRANCH_W, 2 * S5_LANES)
    cc = jnp.stack([c_re.astype(F32), -c_im.astype(F32)], axis=0)
    cc = jnp.einsum('rgkp,gj->rjpgk', cc, eye_g).reshape(2 * S5_LANES, BRANCH_W)
    return bb.astype(BF16), cc.astype(BF16), a_re.reshape(2, S5_LANES), a_im.reshape(2, S5_LANES)


def _s5(u_tb, mats, *, bsz, tc, tt):
    bb, cc, a_re, a_im = mats
    n_rows = u_tb.shape[0]
    t_all = n_rows // bsz
    nt, nct = t_all // tt, tc // tt
    a = jnp.broadcast_to(jnp.stack([a_re, a_im], axis=1)[:, :, None, :], (2, 2, bsz, S5_LANES))
    cc2 = jnp.broadcast_to(cc[None], (2,) + cc.shape)
    blk = (tt * bsz, BRANCH_W)
    fmap = lambda s: (s, 0)
    bmap = lambda s: (_bwd_order(s, nct, nt), 0)
    return pl.pallas_call(
        functools.partial(_s5_kernel, tt=tt, bsz=bsz),
        grid=(nt,),
        in_specs=[pl.BlockSpec(blk, fmap), pl.BlockSpec(blk, bmap),
                  pl.BlockSpec(bb.shape, lambda s: (0, 0, 0)), pl.BlockSpec(cc2.shape, lambda s: (0, 0, 0)),
                  pl.BlockSpec(a.shape, lambda s: (0, 0, 0, 0))],
        out_specs=[pl.BlockSpec(blk, fmap), pl.BlockSpec(blk, bmap)],
        out_shape=[jax.ShapeDtypeStruct((n_rows, BRANCH_W), F32)] * 2,
        scratch_shapes=[pltpu.VMEM((2, tt * bsz, 2 * S5_LANES), F32), pltpu.VMEM((2, 2, bsz, S5_LANES), F32)],
        compiler_params=pltpu.CompilerParams(dimension_semantics=("arbitrary",),
                                             vmem_limit_bytes=V7X_VMEM_LIMIT),
        name="s5_scan",
    )(u_tb, u_tb, bb, cc2, a)


def _rope(x, cos, sin_signed):
    w = x.shape[-1]
    lane = lax.broadcasted_iota(jnp.int32, x.shape, 1)
    up = pltpu.roll(x, w - 16, 1)
    dn = pltpu.roll(x, 16, 1)
    return x * cos + jnp.where((lane & 31) < 16, up, dn) * sin_signed


def _attn_kernel(q_ref, kv_ref, cq_ref, sq_ref, ck_ref, sk_ref, qg_ref, kg_ref, bdq_ref, bdk_ref,
                 o_ref, kp_ref, vp_ref, *, tm, tc, nct):
    j = pl.program_id(1)
    t_all = kv_ref.shape[0]
    half = BRANCH_W // 2

    @pl.when(j == 0)
    def _():
        def pairs(x):
            lane = lax.broadcasted_iota(jnp.int32, x.shape, 1)
            sw = pltpu.roll(x, HEAD_DIM, 1)
            return _bf(jnp.where(lane < HEAD_DIM, x, sw)), _bf(jnp.where(lane < HEAD_DIM, sw, x))

        for lo, hi, rope in ((0, tc, False), (tc, t_all, True)):
            k = _head_rms(kv_ref[lo:hi, 0:half], bdk_ref[...], kg_ref[...])
            if rope:
                k = _rope(k, ck_ref[...], sk_ref[...])
            kp_ref[0, lo:hi, :], kp_ref[1, lo:hi, :] = pairs(k)
            vp_ref[0, lo:hi, :], vp_ref[1, lo:hi, :] = pairs(kv_ref[lo:hi, half:2 * half])

    def attend(q, nk):
        lane = lax.broadcasted_iota(jnp.int32, (tm, half), 1)
        outs = []
        for g in range(2):
            qp = q[:, g * half:(g + 1) * half]
            acc = jnp.zeros((tm, half), F32)
            for hh in range(2):
                sel = (lane < HEAD_DIM) if hh == 0 else (lane >= HEAD_DIM)
                s = _dot_nt(_bf(jnp.where(sel, qp, 0.0)), kp_ref[g, 0:nk, :])
                p = jnp.exp(s - jnp.max(s, axis=-1, keepdims=True))
                o = _dot(_bf(p), vp_ref[g, 0:nk, :]) / jnp.sum(p, axis=-1, keepdims=True)
                acc = acc + jnp.where(sel, o, 0.0)
            outs.append(acc)
        o_ref[...] = jnp.concatenate(outs, axis=1)

    qn = _head_rms(q_ref[...], bdq_ref[...], qg_ref[...]) * (HEAD_DIM ** -0.5)

    @pl.when(j < nct)
    def _():
        attend(qn, tc)

    @pl.when(j >= nct)
    def _():
        attend(_rope(qn, cq_ref[...], sq_ref[...]), t_all)


def _rope_tables(n_tokens):
    rows = n_tokens // GRID_W
    r, col = np.meshgrid(np.arange(rows), np.arange(GRID_W), indexing='ij')
    axis_dim = HEAD_DIM // 2
    inv = jnp.asarray(ROPE_THETA, F32) ** (-jnp.arange(0, axis_dim, 2, dtype=F32) / axis_dim)
    ang_r = jnp.asarray(r.reshape(-1, 1), F32) * inv
    ang_c = jnp.asarray(col.reshape(-1, 1), F32) * inv
    ang = jnp.concatenate([ang_r, ang_r, ang_c, ang_c], axis=-1)
    sign = np.where((np.arange(HEAD_DIM) % 32) < 16, -1.0, 1.0).astype(np.float32)
    cos, sin = jnp.cos(ang), jnp.sin(ang) * sign
    return jnp.tile(cos, (1, N_HEADS)), jnp.tile(sin, (1, N_HEADS))


def _attention(p_q, p_kv, qn_g, kn_g, cos4, sin4, *, tm, tc):
    bsz, t_all, _ = p_q.shape
    nt, nct = t_all // tm, tc // tm
    half = BRANCH_W // 2
    tok = lambda b, j: (b, j, 0)
    c2 = lambda b, j: (0, 0)
    lat = lambda b, j: (jnp.maximum(j - nct, 0), 0)
    qg = jnp.tile(qn_g.astype(F32), N_HEADS).reshape(1, BRANCH_W)
    kg = jnp.tile(kn_g.astype(F32), 2).reshape(1, half)
    bdq = jnp.asarray(_np_block_diag_ones(BRANCH_W, HEAD_DIM), BF16)
    bdk = jnp.asarray(_np_block_diag_ones(half, HEAD_DIM), BF16)
    cosk, sink = cos4[:, 0:half], sin4[:, 0:half]
    return pl.pallas_call(
        functools.partial(_attn_kernel, tm=tm, tc=tc, nct=nct),
        grid=(bsz, nt),
        in_specs=[pl.BlockSpec((None, tm, BRANCH_W), tok),
                  pl.BlockSpec((None, t_all, BRANCH_W), lambda b, j: (b, 0, 0)),
                  pl.BlockSpec((tm, BRANCH_W), lat), pl.BlockSpec((tm, BRANCH_W), lat),
                  pl.BlockSpec(cosk.shape, c2), pl.BlockSpec(sink.shape, c2),
                  pl.BlockSpec((1, BRANCH_W), c2), pl.BlockSpec((1, half), c2),
                  pl.BlockSpec(bdq.shape, c2), pl.BlockSpec(bdk.shape, c2)],
        out_specs=pl.BlockSpec((None, tm, BRANCH_W), tok),
        out_shape=jax.ShapeDtypeStruct((bsz, t_all, BRANCH_W), F32),
        scratch_shapes=[pltpu.VMEM((2, t_all, half), BF16), pltpu.VMEM((2, t_all, half), BF16)],
        compiler_params=pltpu.CompilerParams(dimension_semantics=("parallel", "arbitrary"),
                                             vmem_limit_bytes=V7X_VMEM_LIMIT),
        name="attention",
    )(p_q, p_kv, cos4, sin4, cosk, sink, qg, kg, bdq, bdk)


def _merge_kernel(x_ref, m_ref, g_ref, dnf_ref, dnb_ref, z_ref, sf_ref, sb_ref, u_ref, hf_ref, hb_ref, hgate_ref,
                  at_ref, dng_ref, hgg_ref, dsk_ref, glu_ref, bd_ref, wg_ref, wb_ref, wo_ref, o_ref):
    x = x_ref[...]
    bd = bd_ref[...]
    h = _bf(_rms(x) * g_ref[...] * (1.0 + m_ref[4:5, :]) + m_ref[3:4, :])
    y_dn = _head_rms(dnf_ref[...] + dnb_ref[...], bd, dng_ref[...]) * _silu(z_ref[...])
    ys = sf_ref[...] + sb_ref[...] + dsk_ref[...] * u_ref[...]
    ab = _dot(_bf(jax.nn.gelu(ys, approximate=True)), glu_ref[...])
    y_s5 = ab[:, 0:BRANCH_W] * _sigmoid(ab[:, BRANCH_W:2 * BRANCH_W])
    y_hg = _head_rms(hf_ref[...] + hb_ref[...], bd, hgg_ref[...]) * _sigmoid(hgate_ref[...])
    ys_all = (y_dn, y_s5, y_hg, at_ref[...])
    acc = jnp.zeros(x.shape, F32)
    for i in range(4):
        acc = acc + _sigmoid(_dot(h, wg_ref[i])) * _dot(_bf(ys_all[i]), wb_ref[i])
    o_ref[...] = x + m_ref[5:6, :] * _dot(_bf(acc), wo_ref[...])


def _merge(xj, mods, g, br, params, *, tm, nct, j0):
    bsz, t_all, _ = xj.shape
    n_tiles = t_all // tm - j0
    tok = lambda b, j: (b, j + j0, 0)
    col = lambda c: (lambda b, j: (b, j + j0, c))
    mod_map = lambda b, j: (jnp.where(j + j0 < nct, bsz, b), 0, 0)
    c2 = lambda b, j: (0, 0)
    c3 = lambda b, j: (0, 0, 0)
    tb = (None, tm, BRANCH_W)
    dn_g, hg_g, d_skip, w_glu, wg, wb, wo = params
    bd = jnp.asarray(_np_block_diag_ones(BRANCH_W, HEAD_DIM), BF16)
    return pl.pallas_call(
        _merge_kernel,
        grid=(bsz, n_tiles),
        in_specs=[pl.BlockSpec((None, tm, D_MODEL), tok),
                  pl.BlockSpec((None, N_MOD, D_MODEL), mod_map),
                  pl.BlockSpec((1, D_MODEL), c2),
                  pl.BlockSpec(tb, tok), pl.BlockSpec(tb, tok), pl.BlockSpec(tb, col(3)),
                  pl.BlockSpec(tb, tok), pl.BlockSpec(tb, tok), pl.BlockSpec(tb, tok),
                  pl.BlockSpec(tb, tok), pl.BlockSpec(tb, tok), pl.BlockSpec(tb, col(4)),
                  pl.BlockSpec(tb, tok),
                  pl.BlockSpec((1, BRANCH_W), c2), pl.BlockSpec((1, BRANCH_W), c2), pl.BlockSpec((1, BRANCH_W), c2),
                  pl.BlockSpec(w_glu.shape, c2), pl.BlockSpec(bd.shape, c2),
                  pl.BlockSpec(wg.shape, c3), pl.BlockSpec(wb.shape, c3), pl.BlockSpec(wo.shape, c2)],
        out_specs=pl.BlockSpec((None, tm, D_MODEL), lambda b, j: (b, j, 0)),
        out_shape=jax.ShapeDtypeStruct((bsz, n_tiles * tm, D_MODEL), F32),
        compiler_params=pltpu.CompilerParams(dimension_semantics=("parallel", "parallel"),
                                             vmem_limit_bytes=V7X_VMEM_LIMIT),
        name="branch_merge",
    )(xj, mods, g.reshape(1, D_MODEL), br['dn_f'], br['dn_b'], br['p_dn'], br['s5_f'], br['s5_b'], br['p_s5'],
      br['hg_f'], br['hg_b'], br['p_hg'], br['at'], dn_g, hg_g, d_skip, w_glu, bd, wg, wb, wo)


def kernel(x, c, ctx, c_ctx, ada_w, ada_b, norm_g, ffn_w1, ffn_w3, ffn_w2, w_in, dn_conv, dn_a_log, dn_dt_bias,
           dn_norm_g, s5_lam_re, s5_lam_im, s5_log_step, s5_b_re, s5_b_im, s5_c_re, s5_c_im, s5_d, s5_glu,
           hg_lb_logits, hg_norm_g, at_qn_g, at_kn_g, w_branch, w_out, final_g):
    bsz, tl, _ = x.shape
    tc = ctx.shape[1]
    t_all = tc + tl
    depth = ada_w.shape[0]
    tm = 256 if (tc % 256 == 0 and tl % 256 == 0) else 128
    assert tc % tm == 0 and tl % tm == 0 and tm % CHUNK == 0 and bsz % 8 == 0
    nct = tc // tm
    d_ff = ffn_w1.shape[-1]
    n_ff = d_ff // FF_CHUNK

    n_rows = -(-(bsz + 1) // 8) * 8
    cond = jnp.zeros((n_rows, D_MODEL), F32).at[0:bsz].set(c.astype(F32)).at[bsz].set(c_ctx.astype(F32))
    mods_all = _ada_mods(cond, ada_w.astype(F32), ada_b.astype(F32))
    cos4, sin4 = _rope_tables(tl)
    xj = jnp.concatenate([ctx, x], axis=1).astype(F32)

    def ffn_weights(l, i):
        w1 = ffn_w1[l, i].astype(BF16).reshape(D_MODEL, n_ff, FF_CHUNK).transpose(1, 0, 2)
        w3 = ffn_w3[l, i].astype(BF16).reshape(D_MODEL, n_ff, FF_CHUNK).transpose(1, 0, 2)
        w2 = ffn_w2[l, i].astype(BF16).reshape(n_ff, FF_CHUNK, D_MODEL)
        return w1, w3, w2

    for l in range(depth):
        last = l == depth - 1
        mods = mods_all[l]
        xj = _ffn_half_step(xj, mods, norm_g[l, 0], *ffn_weights(l, 0), mod_base=0, tm=tm, nct=nct)

        wl = w_in[l]
        w_dn = jnp.concatenate([wl[:, 0:1040], jnp.zeros((D_MODEL, 112), wl.dtype)], axis=1)
        ws = [w.astype(BF16) for w in (w_dn, wl[:, 1040:1296], wl[:, 1296:2576], wl[:, 2576:2832], wl[:, 2832:3088])]
        p_dn, p_s5, p_hg, p_aq, p_kv = _input_projection(xj, mods, norm_g[l, 1], ws, tm=tm, nct=nct)

        dn_f, dn_b = _deltanet(p_dn, dn_conv[l], dn_a_log[l], dn_dt_bias[l], tm=tm, tc=tc)
        hg_f, hg_b = _hgrn2(p_hg, hg_lb_logits, l, tc=tc)
        mats = _s5_matrices(s5_lam_re[l], s5_lam_im[l], s5_log_step[l], s5_b_re[l], s5_b_im[l], s5_c_re[l], s5_c_im[l])
        u_tb = p_s5.transpose(1, 0, 2).reshape(t_all * bsz, BRANCH_W)
        s5_f, s5_b = _s5(u_tb, mats, bsz=bsz, tc=tc, tt=CHUNK)
        s5_f = s5_f.reshape(t_all, bsz, BRANCH_W).transpose(1, 0, 2)
        s5_b = s5_b.reshape(t_all, bsz, BRANCH_W).transpose(1, 0, 2)
        y_at = _attention(p_aq, p_kv, at_qn_g[l], at_kn_g[l], cos4, sin4, tm=tm, tc=tc)

        br = dict(dn_f=dn_f, dn_b=dn_b, p_dn=p_dn, s5_f=s5_f, s5_b=s5_b, p_s5=p_s5, hg_f=hg_f, hg_b=hg_b,
                  p_hg=p_hg, at=y_at)
        params = (jnp.tile(dn_norm_g[l].astype(F32), N_HEADS).reshape(1, BRANCH_W),
                  jnp.tile(hg_norm_g[l].astype(F32), N_HEADS).reshape(1, BRANCH_W),
                  s5_d[l].astype(F32).reshape(1, BRANCH_W),
                  s5_glu[l].astype(BF16),
                  wl[:, 3088:].astype(BF16).reshape(D_MODEL, 4, D_MODEL).transpose(1, 0, 2),
                  w_branch[l].astype(BF16),
                  w_out[l].astype(BF16))
        xj = _merge(xj, mods, norm_g[l, 1], br, params, tm=tm, nct=nct, j0=nct if last else 0)
        xj = _ffn_half_step(xj, mods, norm_g[l, 2], *ffn_weights(l, 1), mod_base=6, tm=tm,
                            nct=0 if last else nct, final_g=final_g.astype(F32) if last else None)
    return xj.astype(x.dtype)
```

```python
import functools
import math

import numpy as np
import jax
import jax.numpy as jnp
from jax import lax
from jax.experimental import pallas as pl
from jax.experimental.pallas import tpu as pltpu

F32 = jnp.float32
BF16 = jnp.bfloat16

D_MODEL = 1024
N_MOD = 9
BRANCH_W = 256
HEAD_DIM = 64
N_HEADS = 4
CHUNK = 64
EPS = 1e-6
LOG_FLOOR = 1e-30
GRID_W = 64
ROPE_THETA = 10000.0
FF_CHUNK = 256
S5_GROUPS = 16
S5_GROUP = 16
S5_STATE = 64
S5_LANES = S5_GROUPS * S5_STATE
NEG_BIG = -1e30
V7X_VMEM_LIMIT = 56 * 1024 * 1024


def _bf(x):
    return x.astype(BF16)


def _dot(a, b):
    return jnp.dot(a, b, preferred_element_type=F32)


def _dot_nt(a, b):
    return lax.dot_general(a, b, (((1,), (1,)), ((), ())), preferred_element_type=F32)


def _dot_tn(a, b):
    return lax.dot_general(a, b, (((0,), (0,)), ((), ())), preferred_element_type=F32)


def _split2(x):
    hi = _bf(x)
    lo = _bf(x - hi.astype(F32))
    return hi, lo


def _split3(x):
    x1 = _bf(x)
    r1 = x - x1.astype(F32)
    x2 = _bf(r1)
    x3 = _bf(r1 - x2.astype(F32))
    return x1, x2, x3


def _dot_x2(x, m):
    hi, lo = _split2(x)
    return _dot(hi, m) + _dot(lo, m)


def _dot_x3(x, m):
    x1, x2, x3 = _split3(x)
    return _dot(x1, m) + _dot(x2, m) + _dot(x3, m)


def _dot3_left(m, x):
    x1, x2, x3 = _split3(x)
    return _dot(m, x1) + _dot(m, x2) + _dot(m, x3)


def _dot_hp(a, b):
    a_hi, a_lo = _split2(a)
    b_hi, b_lo = _split2(b)
    return _dot(a_hi, b_hi) + _dot(a_hi, b_lo) + _dot(a_lo, b_hi)


def _sigmoid(x):
    return jax.nn.sigmoid(x)


def _silu(x):
    return x * jax.nn.sigmoid(x)


def _rms(x):
    return x * lax.rsqrt(jnp.mean(x * x, axis=-1, keepdims=True) + EPS)


def _head_rms(x, bd_ones, gain):
    ss = _dot_x2(x * x, bd_ones) * (1.0 / HEAD_DIM)
    return x * lax.rsqrt(ss + EPS) * gain


def _stack4(x):
    return jnp.concatenate([x, x, x, x], axis=0)


def _unstack4(x):
    n = x.shape[0] // 4
    return x[0:n] + x[n:2 * n] + x[2 * n:3 * n] + x[3 * n:4 * n]


def _np_block_diag_ones(n, blk):
    i = np.arange(n)
    return (i[:, None] // blk == i[None, :] // blk).astype(np.float32)


def _np_tri(n, blk, lower, strict):
    i = np.arange(n)
    same = i[:, None] // blk == i[None, :] // blk
    a, b = i[:, None] % blk, i[None, :] % blk
    if lower:
        t = (a > b) if strict else (a >= b)
    else:
        t = (a < b) if strict else (a <= b)
    return (same & t).astype(np.float32)


def _np_doubling_masks(n, blk):
    i = np.arange(n)
    r, c = i[:, None], i[None, :]
    out = [(r // 2 == c // 2)]
    s = 2
    while s < blk:
        out.append((r // (2 * s) == c // (2 * s)) & (r // s != c // s))
        s *= 2
    return np.stack(out).astype(np.float32)


def _bwd_order(s, n_ctx, n_all):
    return jnp.where(s < n_ctx, n_ctx - 1 - s, n_all + n_ctx - 1 - s)


def _ada_kernel(c_ref, w_ref, b_ref, o_ref):
    s = _silu(c_ref[...])
    w = w_ref[...]
    s_hi, s_lo = _split2(s)
    w_hi, w_lo = _split2(w)
    o_ref[...] = _dot(s_hi, w_hi) + _dot(s_hi, w_lo) + _dot(s_lo, w_hi) + b_ref[...]


def _ada_mods(cond, ada_w, ada_b):
    n_layers = ada_w.shape[0]
    r = cond.shape[0]
    n_out = ada_w.shape[2]
    tn = 1152
    out = pl.pallas_call(
        _ada_kernel,
        grid=(n_layers, n_out // tn),
        in_specs=[
            pl.BlockSpec((r, D_MODEL), lambda l, n: (0, 0)),
            pl.BlockSpec((None, D_MODEL, tn), lambda l, n: (l, 0, n)),
            pl.BlockSpec((None, 1, tn), lambda l, n: (l, 0, n)),
        ],
        out_specs=pl.BlockSpec((None, r, tn), lambda l, n: (l, 0, n)),
        out_shape=jax.ShapeDtypeStruct((n_layers, r, n_out), F32),
        compiler_params=pltpu.CompilerParams(dimension_semantics=("parallel", "parallel"),
                                             vmem_limit_bytes=40 * 1024 * 1024),
        name="ada_mods",
    )(cond, ada_w, ada_b.reshape(n_layers, 1, n_out))
    return out.reshape(n_layers, r, N_MOD, D_MODEL)


def _ffn_kernel(*refs, mod_base, final):
    if final:
        x_ref, m_ref, g_ref, w1_ref, w3_ref, w2_ref, fg_ref, o_ref, h_ref, acc_ref = refs
    else:
        x_ref, m_ref, g_ref, w1_ref, w3_ref, w2_ref, o_ref, h_ref, acc_ref = refs
    x = x_ref[...]
    shift = m_ref[mod_base:mod_base + 1, :]
    scale = m_ref[mod_base + 1:mod_base + 2, :]
    gate = m_ref[mod_base + 2:mod_base + 3, :]
    h_ref[...] = _bf(_rms(x) * g_ref[...] * (1.0 + scale) + shift)
    acc_ref[...] = jnp.zeros_like(acc_ref)

    def body(c, carry):
        h = h_ref[...]
        a = _dot(h, w1_ref[c])
        b = _dot(h, w3_ref[c])
        acc_ref[...] += _dot(_bf(_silu(a) * b), w2_ref[c])
        return carry

    lax.fori_loop(0, w1_ref.shape[0], body, 0)
    out = x + 0.5 * gate * acc_ref[...]
    if final:
        out = _rms(out) * fg_ref[...]
    o_ref[...] = out


def _ffn_half_step(xj, mods, g, w1c, w3c, w2c, *, mod_base, tm, nct, final_g=None):
    bsz, t_all, _ = xj.shape
    final = final_g is not None

    def tok_map(b, j):
        return (b, j, 0)

    def mod_map(b, j):
        return (jnp.where(j < nct, bsz, b), 0, 0)

    const3 = lambda b, j: (0, 0, 0)
    const2 = lambda b, j: (0, 0)
    in_specs = [
        pl.BlockSpec((None, tm, D_MODEL), tok_map),
        pl.BlockSpec((None, N_MOD, D_MODEL), mod_map),
        pl.BlockSpec((1, D_MODEL), const2),
        pl.BlockSpec(w1c.shape, const3),
        pl.BlockSpec(w3c.shape, const3),
        pl.BlockSpec(w2c.shape, const3),
    ]
    args = [xj, mods, g.reshape(1, D_MODEL), w1c, w3c, w2c]
    if final:
        in_specs.append(pl.BlockSpec((1, D_MODEL), const2))
        args.append(final_g.reshape(1, D_MODEL))
    return pl.pallas_call(
        functools.partial(_ffn_kernel, mod_base=mod_base, final=final),
        grid=(bsz, t_all // tm),
        in_specs=in_specs,
        out_specs=pl.BlockSpec((None, tm, D_MODEL), tok_map),
        out_shape=jax.ShapeDtypeStruct(xj.shape, F32),
        scratch_shapes=[pltpu.VMEM((tm, D_MODEL), BF16), pltpu.VMEM((tm, D_MODEL), F32)],
        compiler_params=pltpu.CompilerParams(dimension_semantics=("parallel", "parallel"),
                                             vmem_limit_bytes=V7X_VMEM_LIMIT),
        name="ffn_half_step",
    )(*args)


def _inproj_kernel(x_ref, m_ref, g_ref, wdn_ref, ws5_ref, whg_ref, waq_ref, wkv_ref,
                   dn_ref, s5_ref, hg_ref, aq_ref, kv_ref):
    x = x_ref[...]
    h = _bf(_rms(x) * g_ref[...] * (1.0 + m_ref[4:5, :]) + m_ref[3:4, :])
    dn_ref[...] = _dot(h, wdn_ref[...])
    s5_ref[...] = _dot(h, ws5_ref[...])
    hg_ref[...] = _dot(h, whg_ref[...])
    aq_ref[...] = _dot(h, waq_ref[...])
    kv_ref[...] = _dot(h, wkv_ref[...])


def _input_projection(xj, mods, g, ws, *, tm, nct):
    bsz, t_all, _ = xj.shape
    tok_map = lambda b, j: (b, j, 0)
    mod_map = lambda b, j: (jnp.where(j < nct, bsz, b), 0, 0)
    const2 = lambda b, j: (0, 0)
    widths = [w.shape[1] for w in ws]
    return pl.pallas_call(
        _inproj_kernel,
        grid=(bsz, t_all // tm),
        in_specs=[pl.BlockSpec((None, tm, D_MODEL), tok_map),
                  pl.BlockSpec((None, N_MOD, D_MODEL), mod_map),
                  pl.BlockSpec((1, D_MODEL), const2)]
                 + [pl.BlockSpec(w.shape, const2) for w in ws],
        out_specs=[pl.BlockSpec((None, tm, n), tok_map) for n in widths],
        out_shape=[jax.ShapeDtypeStruct((bsz, t_all, n), F32) for n in widths],
        compiler_params=pltpu.CompilerParams(dimension_semantics=("parallel", "parallel"),
                                             vmem_limit_bytes=V7X_VMEM_LIMIT),
        name="input_projection",
    )(xj, mods, g.reshape(1, D_MODEL), *ws)


def _dn_prep_kernel(p_ref, prev_ref, next_ref, cw_ref, al_ref, dtb_ref, e_ref, tril_ref, triu_ref, bd_ref,
                    q_ref, k_ref, v_ref, gb_ref, *, tm, nct, nt):
    j = pl.program_id(1)
    first = jnp.logical_or(j == 0, j == nct)
    last = jnp.logical_or(j == nct - 1, j == nt - 1)
    x = p_ref[:, 0:3 * BRANCH_W]
    pv = prev_ref[...] * jnp.where(first, 0.0, 1.0)
    nx = next_ref[...] * jnp.where(last, 0.0, 1.0)
    ext = jnp.concatenate([pv, x, nx], axis=0)
    n = tm + 16
    acc = jnp.zeros((tm, 3 * BRANCH_W), F32)
    for t in range(5):
        sh = (2 - t) % n
        r = ext if sh == 0 else pltpu.roll(ext, sh, 0)
        acc = acc + r[8:8 + tm] * cw_ref[t:t + 1, :]
    h = _silu(acc)
    bd = bd_ref[...]

    def l2n(t):
        return t * lax.rsqrt(_dot_x2(t * t, bd) + EPS)

    q_ref[...] = l2n(h[:, 0:BRANCH_W]) * (HEAD_DIM ** -0.5)
    k_ref[...] = l2n(h[:, BRANCH_W:2 * BRANCH_W])
    v_ref[...] = h[:, 2 * BRANCH_W:3 * BRANCH_W]

    ab = p_ref[:, 4 * BRANCH_W:4 * BRANCH_W + 128]
    z = ab + dtb_ref[...]
    softplus = jnp.maximum(z, 0.0) + jnp.log1p(jnp.exp(-jnp.abs(z)))
    g = -jnp.exp(al_ref[...]) * softplus
    lane = lax.broadcasted_iota(jnp.int32, ab.shape, 1)
    g = jnp.where(lane < 8, g, 0.0)
    cs_f = _dot3_left(tril_ref[...], g)
    cs_b = _dot3_left(triu_ref[...], g)
    comp = jnp.where(lane < 4, cs_f, jnp.where(lane < 8, cs_b, jnp.where(lane < 16, _sigmoid(ab), 0.0)))
    gb_ref[...] = _dot_x3(comp, e_ref[...])


def _dn_chunk_local(q, k, v, gc, be, bd, tri_incl, tri_strict, it, eye, lvl_ref, fwd):
    ks = _bf(_stack4(k) * bd)
    gram = _dot_nt(ks, ks)
    row_gc = _stack4(gc)
    col_gc = jnp.sum(gc * it, axis=0, keepdims=True)
    decay = jnp.exp(jnp.where(tri_incl > 0.5, row_gc - col_gc, NEG_BIG))
    a = gram * decay * _stack4(be) * tri_strict
    t = eye - a * lvl_ref[0]
    for lv in range(1, lvl_ref.shape[0]):
        tb = _bf(t)
        x = _dot(_bf(a * lvl_ref[lv]), tb)
        t = t - _dot(tb, _bf(x))
    egc = jnp.exp(gc)
    tb = _bf(t)
    u = _unstack4(_dot(tb, _bf(_stack4(v * be) * bd)))
    w = _unstack4(_dot(tb, _bf(_stack4(k * be * egc) * bd)))
    attn = _bf(_dot_nt(_bf(_stack4(q) * bd), ks) * decay)
    total = gc[CHUNK - 1:CHUNK, :] if fwd else gc[0:1, :]
    return u, _bf(w), attn, _bf(q * egc), _bf(k * jnp.exp(total - gc)), jnp.exp(total)


def _dn_chunk_state(local, s_prev, bd):
    u, w, attn, q_dec, k_dec, g_last = local
    sb = _bf(s_prev)
    v_new = u - _dot(w, sb)
    o = _dot(q_dec, sb) + _unstack4(_dot(attn, _bf(_stack4(v_new) * bd)))
    s_new = s_prev * g_last + _dot_tn(k_dec, _bf(v_new)) * bd
    return o, s_new


def _dn_scan_kernel(qf_ref, kf_ref, vf_ref, gcf_ref, bef_ref, qb_ref, kb_ref, vb_ref, gcb_ref, beb_ref,
                    bd_ref, li_ref, ls_ref, ui_ref, us_ref, it_ref, eye_ref, lvl_ref,
                    of_ref, ob_ref, s_ref):
    @pl.when(pl.program_id(1) == 0)
    def _():
        s_ref[...] = jnp.zeros_like(s_ref)

    bd, it, eye = bd_ref[...], it_ref[...], eye_ref[...]
    n_sub = qf_ref.shape[0] // CHUNK
    dirs = ((qf_ref, kf_ref, vf_ref, gcf_ref, bef_ref, li_ref, ls_ref, of_ref, True),
            (qb_ref, kb_ref, vb_ref, gcb_ref, beb_ref, ui_ref, us_ref, ob_ref, False))
    local = []
    for q_ref, k_ref, v_ref, gc_ref, be_ref, ti_ref, ts_ref, _, fwd in dirs:
        for g in range(n_sub):
            r = slice(g * CHUNK, (g + 1) * CHUNK)
            local.append(_dn_chunk_local(q_ref[r, :], k_ref[r, :], v_ref[r, :], gc_ref[r, :], be_ref[r, :],
                                         bd, ti_ref[...], ts_ref[...], it, eye, lvl_ref, fwd))
    for d, (*_, o_ref, fwd) in enumerate(dirs):
        s = s_ref[d]
        for g in (range(n_sub) if fwd else range(n_sub - 1, -1, -1)):
            o, s = _dn_chunk_state(local[d * n_sub + g], s, bd)
            o_ref[g * CHUNK:(g + 1) * CHUNK, :] = o
        s_ref[d] = s


def _deltanet(p_dn, conv_w, a_log, dt_bias, *, tm, tc):
    bsz, t_all, _ = p_dn.shape
    nt, nct = t_all // tm, tc // tm
    al = jnp.zeros((1, 128), F32).at[0, 0:8].set(a_log.reshape(8).astype(F32))
    dtb = jnp.zeros((1, 128), F32).at[0, 0:8].set(dt_bias.reshape(8).astype(F32))
    e = np.zeros((128, 4 * BRANCH_W), np.float32)
    for jj in range(4):
        for h in range(N_HEADS):
            e[4 * jj + h, jj * BRANCH_W + h * HEAD_DIM: jj * BRANCH_W + (h + 1) * HEAD_DIM] = 1.0
    consts = [jnp.asarray(e, BF16),
              jnp.asarray(_np_tri(tm, CHUNK, True, False), BF16),
              jnp.asarray(_np_tri(tm, CHUNK, False, False), BF16),
              jnp.asarray(_np_block_diag_ones(BRANCH_W, HEAD_DIM), BF16)]
    tok = lambda b, j: (b, j, 0)
    c2 = lambda b, j: (0, 0)
    hb = tm // 8
    q, k, v, gb = pl.pallas_call(
        functools.partial(_dn_prep_kernel, tm=tm, nct=nct, nt=nt),
        grid=(bsz, nt),
        in_specs=[pl.BlockSpec((None, tm, p_dn.shape[2]), tok),
                  pl.BlockSpec((None, 8, 3 * BRANCH_W), lambda b, j: (b, jnp.maximum(j * hb - 1, 0), 0)),
                  pl.BlockSpec((None, 8, 3 * BRANCH_W), lambda b, j: (b, jnp.minimum((j + 1) * hb, nt * hb - 1), 0)),
                  pl.BlockSpec(conv_w.shape, c2), pl.BlockSpec((1, 128), c2), pl.BlockSpec((1, 128), c2)]
                 + [pl.BlockSpec(c.shape, c2) for c in consts],
        out_specs=[pl.BlockSpec((None, tm, BRANCH_W), tok)] * 3 + [pl.BlockSpec((None, tm, 4 * BRANCH_W), tok)],
        out_shape=[jax.ShapeDtypeStruct((bsz, t_all, BRANCH_W), F32)] * 3
                  + [jax.ShapeDtypeStruct((bsz, t_all, 4 * BRANCH_W), F32)],
        compiler_params=pltpu.CompilerParams(dimension_semantics=("parallel", "parallel"),
                                             vmem_limit_bytes=40 * 1024 * 1024),
        name="deltanet_prep",
    )(p_dn, p_dn, p_dn, conv_w.astype(F32), al, dtb, *consts)

    nc, ncc = nt, nct
    n4 = N_HEADS * CHUNK
    masks = [jnp.asarray(_np_block_diag_ones(n4, CHUNK), F32),
             jnp.asarray(_np_tri(n4, CHUNK, True, False), F32),
             jnp.asarray(_np_tri(n4, CHUNK, True, True), F32),
             jnp.asarray(_np_tri(n4, CHUNK, False, False), F32),
             jnp.asarray(_np_tri(n4, CHUNK, False, True), F32),
             jnp.asarray(np.tile(np.eye(CHUNK, dtype=np.float32), (1, N_HEADS)), F32),
             jnp.asarray(np.eye(n4, dtype=np.float32), F32),
             jnp.asarray(_np_doubling_masks(n4, CHUNK), F32)]
    cmap = lambda m: (lambda b, s: (0,) * m.ndim)
    blk = (None, tm, BRANCH_W)
    fmap = lambda col: (lambda b, s: (b, s, col))
    bmap = lambda col: (lambda b, s: (b, _bwd_order(s, ncc, nc), col))
    o_f, o_b = pl.pallas_call(
        _dn_scan_kernel,
        grid=(bsz, nc),
        in_specs=[pl.BlockSpec(blk, fmap(0))] * 3 + [pl.BlockSpec(blk, fmap(0)), pl.BlockSpec(blk, fmap(2))]
                 + [pl.BlockSpec(blk, bmap(0))] * 3 + [pl.BlockSpec(blk, bmap(1)), pl.BlockSpec(blk, bmap(3))]
                 + [pl.BlockSpec(m.shape, cmap(m)) for m in masks],
        out_specs=[pl.BlockSpec(blk, fmap(0)), pl.BlockSpec(blk, bmap(0))],
        out_shape=[jax.ShapeDtypeStruct((bsz, t_all, BRANCH_W), F32)] * 2,
        scratch_shapes=[pltpu.VMEM((2, n4, n4), F32)],
        compiler_params=pltpu.CompilerParams(dimension_semantics=("parallel", "arbitrary"),
                                             vmem_limit_bytes=40 * 1024 * 1024),
        name="deltanet_scan",
    )(q, k, v, gb, gb, q, k, v, gb, gb, *masks)
    return o_f, o_b


def _hg_chunk(q, f, v, lb, tri, st_prev, bd_ones, bd_mask, acc_ref, tmp_ref, fwd):
    qs = _silu(q)
    f_gate = lb + (1.0 - lb) * _sigmoid(f)
    logf = jnp.log(jnp.maximum(f_gate, LOG_FLOOR))
    kk = (1.0 - lb) * _sigmoid(-f)
    b = _dot3_left(tri, logf)
    b_last = b[CHUNK - 1:CHUNK, :] if fwd else b[0:1, :]
    acc_ref[...] = _dot_nt(_bf(qs * jnp.exp(b)), _bf(st_prev))
    tmp_ref[0], tmp_ref[1], tmp_ref[2], tmp_ref[3] = qs, kk, b, v
    for g in range(CHUNK // 8):
        lo, hi = (8 * g, CHUNK) if fwd else (0, 8 * g + 8)
        nr = hi - lo
        qg, bg = tmp_ref[0, lo:hi, :], tmp_ref[2, lo:hi, :]
        tg = lax.broadcasted_iota(jnp.int32, (nr, BRANCH_W), 0) + lo
        rows = []
        for s in range(8 * g, 8 * g + 8):
            m = (tg >= s) if fwd else (tg <= s)
            rows.append(qg * tmp_ref[1, s:s + 1, :] * jnp.exp(jnp.where(m, bg - tmp_ref[2, s:s + 1, :], NEG_BIG)))
        se = _dot(_bf(jnp.concatenate(rows, axis=0)), bd_ones)
        contrib = se[0:nr] * tmp_ref[3, 8 * g:8 * g + 1, :]
        for i in range(1, 8):
            contrib = contrib + se[i * nr:(i + 1) * nr] * tmp_ref[3, 8 * g + i:8 * g + i + 1, :]
        acc_ref[lo:hi, :] += contrib
    k_dec = kk * jnp.exp(b_last - b)
    st_new = st_prev * jnp.exp(b_last) + _dot_tn(_bf(v), _bf(k_dec)) * bd_mask
    return st_new


def _hg_scan_kernel(qf_ref, ff_ref, vf_ref, qb_ref, fb_ref, vb_ref, lg_ref, bdo_ref, bdm_ref, tl_ref, tu_ref,
                    of_ref, ob_ref, st_ref, tmp_ref, *, layer):
    @pl.when(pl.program_id(1) == 0)
    def _():
        st_ref[...] = jnp.zeros_like(st_ref)

    n_layers = lg_ref.shape[0]
    mx = lg_ref[0:1, :]
    for i in range(1, n_layers):
        mx = jnp.maximum(mx, lg_ref[i:i + 1, :])
    den = jnp.zeros_like(mx)
    num = jnp.zeros_like(mx)
    for i in range(n_layers):
        e = jnp.exp(lg_ref[i:i + 1, :] - mx)
        den = den + e
        if 1 <= i <= layer:
            num = num + e
    lb = num / den
    bdo, bdm = bdo_ref[...], bdm_ref[...]
    st_ref[0] = _hg_chunk(qf_ref[...], ff_ref[...], vf_ref[...], lb, tl_ref[...], st_ref[0], bdo, bdm, of_ref,
                          tmp_ref.at[0], True)
    st_ref[1] = _hg_chunk(qb_ref[...], fb_ref[...], vb_ref[...], lb, tu_ref[...], st_ref[1], bdo, bdm, ob_ref,
                          tmp_ref.at[1], False)


def _hgrn2(p_hg, lb_logits, layer, *, tc):
    bsz, t_all, _ = p_hg.shape
    nc, ncc = t_all // CHUNK, tc // CHUNK
    consts = [jnp.asarray(_np_block_diag_ones(BRANCH_W, HEAD_DIM), BF16),
              jnp.asarray(_np_block_diag_ones(BRANCH_W, HEAD_DIM), F32),
              jnp.asarray(_np_tri(CHUNK, CHUNK, True, False), BF16),
              jnp.asarray(_np_tri(CHUNK, CHUNK, False, False), BF16)]
    blk = (None, CHUNK, BRANCH_W)
    c2 = lambda b, s: (0, 0)
    fmap = lambda col: (lambda b, s: (b, s, col))
    bmap = lambda col: (lambda b, s: (b, _bwd_order(s, ncc, nc), col))
    return pl.pallas_call(
        functools.partial(_hg_scan_kernel, layer=layer),
        grid=(bsz, nc),
        in_specs=[pl.BlockSpec(blk, fmap(0)), pl.BlockSpec(blk, fmap(1)), pl.BlockSpec(blk, fmap(3)),
                  pl.BlockSpec(blk, bmap(0)), pl.BlockSpec(blk, bmap(2)), pl.BlockSpec(blk, bmap(3)),
                  pl.BlockSpec(lb_logits.shape, c2)] + [pl.BlockSpec(c.shape, c2) for c in consts],
        out_specs=[pl.BlockSpec(blk, fmap(0)), pl.BlockSpec(blk, bmap(0))],
        out_shape=[jax.ShapeDtypeStruct((bsz, t_all, BRANCH_W), F32)] * 2,
        scratch_shapes=[pltpu.VMEM((2, BRANCH_W, BRANCH_W), F32), pltpu.VMEM((2, 4, CHUNK, BRANCH_W), F32)],
        compiler_params=pltpu.CompilerParams(dimension_semantics=("parallel", "arbitrary"),
                                             vmem_limit_bytes=40 * 1024 * 1024),
        name="hgrn2_scan",
    )(p_hg, p_hg, p_hg, p_hg, p_hg, p_hg, lb_logits.astype(F32), *consts)


def _s5_kernel(uf_ref, ub_ref, bb_ref, cc_ref, a_ref, yf_ref, yb_ref, x_ref, h_ref, *, tt, bsz):
    @pl.when(pl.program_id(0) == 0)
    def _():
        h_ref[...] = jnp.zeros_like(h_ref)

    x_ref[0] = _dot(_bf(uf_ref[...]), bb_ref[0])
    x_ref[1] = _dot(_bf(ub_ref[...]), bb_ref[1])

    def step(i, carry):
        hfr, hfi, hbr, hbi = carry
        rf = pl.multiple_of(i * bsz, bsz)
        rb = pl.multiple_of((tt - 1 - i) * bsz, bsz)
        ar, ai = a_ref[0, 0], a_ref[0, 1]
        xr = x_ref[0, pl.ds(rf, bsz), 0:S5_LANES]
        xi = x_ref[0, pl.ds(rf, bsz), S5_LANES:2 * S5_LANES]
        nfr = ar * hfr - ai * hfi + xr
        nfi = ar * hfi + ai * hfr + xi
        x_ref[0, pl.ds(rf, bsz), 0:S5_LANES] = nfr
        x_ref[0, pl.ds(rf, bsz), S5_LANES:2 * S5_LANES] = nfi
        ar, ai = a_ref[1, 0], a_ref[1, 1]
        xr = x_ref[1, pl.ds(rb, bsz), 0:S5_LANES]
        xi = x_ref[1, pl.ds(rb, bsz), S5_LANES:2 * S5_LANES]
        nbr = ar * hbr - ai * hbi + xr
        nbi = ar * hbi + ai * hbr + xi
        x_ref[1, pl.ds(rb, bsz), 0:S5_LANES] = nbr
        x_ref[1, pl.ds(rb, bsz), S5_LANES:2 * S5_LANES] = nbi
        return nfr, nfi, nbr, nbi

    hs = lax.fori_loop(0, tt, step, (h_ref[0, 0], h_ref[0, 1], h_ref[1, 0], h_ref[1, 1]))
    h_ref[0, 0], h_ref[0, 1], h_ref[1, 0], h_ref[1, 1] = hs
    yf_ref[...] = _dot(_bf(x_ref[0]), cc_ref[0])
    yb_ref[...] = _dot(_bf(x_ref[1]), cc_ref[1])


def _s5_matrices(lam_re, lam_im, log_step, b_re, b_im, c_re, c_im):
    lam_re, lam_im = lam_re.astype(F32), lam_im.astype(F32)
    step = jnp.exp(log_step.astype(F32))[..., None]
    mag = jnp.exp(lam_re * step)
    a_re, a_im = mag * jnp.cos(lam_im * step), mag * jnp.sin(lam_im * step)
    den = lam_re * lam_re + lam_im * lam_im
    n_re = a_re - 1.0
    coef_re = (n_re * lam_re + a_im * lam_im) / den
    coef_im = (a_im * lam_re - n_re * lam_im) / den
    b_re, b_im = b_re.astype(F32), b_im.astype(F32)
    bb_re = coef_re[..., None] * b_re - coef_im[..., None] * b_im
    bb_im = coef_re[..., None] * b_im + coef_im[..., None] * b_re
    eye_g = jnp.eye(S5_GROUPS, dtype=F32)
    bb = jnp.stack([bb_re, bb_im], axis=1)
    bb = jnp.einsum('drgph,gk->dghrkp', bb, eye_g).reshape(2, BRANCH_W, 2 * S5_LANES)
    cc = jnp.stack([c_re.astype(F32), -c_im.astype(F32)], axis=0)
    cc = jnp.einsum('rgkp,gj->rjpgk', cc, eye_g).reshape(2 * S5_LANES, BRANCH_W)
    return bb.astype(BF16), cc.astype(BF16), a_re.reshape(2, S5_LANES), a_im.reshape(2, S5_LANES)


def _s5(u_tb, mats, *, bsz, tc, tt):
    bb, cc, a_re, a_im = mats
    n_rows = u_tb.shape[0]
    t_all = n_rows // bsz
    nt, nct = t_all // tt, tc // tt
    a = jnp.broadcast_to(jnp.stack([a_re, a_im], axis=1)[:, :, None, :], (2, 2, bsz, S5_LANES))
    cc2 = jnp.broadcast_to(cc[None], (2,) + cc.shape)
    blk = (tt * bsz, BRANCH_W)
    fmap = lambda s: (s, 0)
    bmap = lambda s: (_bwd_order(s, nct, nt), 0)
    return pl.pallas_call(
        functools.partial(_s5_kernel, tt=tt, bsz=bsz),
        grid=(nt,),
        in_specs=[pl.BlockSpec(blk, fmap), pl.BlockSpec(blk, bmap),
                  pl.BlockSpec(bb.shape, lambda s: (0, 0, 0)), pl.BlockSpec(cc2.shape, lambda s: (0, 0, 0)),
                  pl.BlockSpec(a.shape, lambda s: (0, 0, 0, 0))],
        out_specs=[pl.BlockSpec(blk, fmap), pl.BlockSpec(blk, bmap)],
        out_shape=[jax.ShapeDtypeStruct((n_rows, BRANCH_W), F32)] * 2,
        scratch_shapes=[pltpu.VMEM((2, tt * bsz, 2 * S5_LANES), F32), pltpu.VMEM((2, 2, bsz, S5_LANES), F32)],
        compiler_params=pltpu.CompilerParams(dimension_semantics=("arbitrary",),
                                             vmem_limit_bytes=V7X_VMEM_LIMIT),
        name="s5_scan",
    )(u_tb, u_tb, bb, cc2, a)


def _rope(x, cos, sin_signed):
    w = x.shape[-1]
    lane = lax.broadcasted_iota(jnp.int32, x.shape, 1)
    up = pltpu.roll(x, w - 16, 1)
    dn = pltpu.roll(x, 16, 1)
    return x * cos + jnp.where((lane & 31) < 16, up, dn) * sin_signed


def _attn_kernel(q_ref, kv_ref, cq_ref, sq_ref, ck_ref, sk_ref, qg_ref, kg_ref, bdq_ref, bdk_ref,
                 o_ref, kp_ref, vp_ref, *, tm, tc, nct):
    j = pl.program_id(1)
    t_all = kv_ref.shape[0]
    half = BRANCH_W // 2

    @pl.when(j == 0)
    def _():
        def pairs(x):
            lane = lax.broadcasted_iota(jnp.int32, x.shape, 1)
            sw = pltpu.roll(x, HEAD_DIM, 1)
            return _bf(jnp.where(lane < HEAD_DIM, x, sw)), _bf(jnp.where(lane < HEAD_DIM, sw, x))

        for lo, hi, rope in ((0, tc, False), (tc, t_all, True)):
            k = _head_rms(kv_ref[lo:hi, 0:half], bdk_ref[...], kg_ref[...])
            if rope:
                k = _rope(k, ck_ref[...], sk_ref[...])
            kp_ref[0, lo:hi, :], kp_ref[1, lo:hi, :] = pairs(k)
            vp_ref[0, lo:hi, :], vp_ref[1, lo:hi, :] = pairs(kv_ref[lo:hi, half:2 * half])

    def attend(q, nk):
        lane = lax.broadcasted_iota(jnp.int32, (tm, half), 1)
        outs = []
        for g in range(2):
            qp = q[:, g * half:(g + 1) * half]
            acc = jnp.zeros((tm, half), F32)
            for hh in range(2):
                sel = (lane < HEAD_DIM) if hh == 0 else (lane >= HEAD_DIM)
                s = _dot_nt(_bf(jnp.where(sel, qp, 0.0)), kp_ref[g, 0:nk, :])
                p = jnp.exp(s - jnp.max(s, axis=-1, keepdims=True))
                o = _dot(_bf(p), vp_ref[g, 0:nk, :]) / jnp.sum(p, axis=-1, keepdims=True)
                acc = acc + jnp.where(sel, o, 0.0)
            outs.append(acc)
        o_ref[...] = jnp.concatenate(outs, axis=1)

    qn = _head_rms(q_ref[...], bdq_ref[...], qg_ref[...]) * (HEAD_DIM ** -0.5)

    @pl.when(j < nct)
    def _():
        attend(qn, tc)

    @pl.when(j >= nct)
    def _():
        attend(_rope(qn, cq_ref[...], sq_ref[...]), t_all)


def _rope_tables(n_tokens):
    rows = n_tokens // GRID_W
    r, col = np.meshgrid(np.arange(rows), np.arange(GRID_W), indexing='ij')
    axis_dim = HEAD_DIM // 2
    inv = jnp.asarray(ROPE_THETA, F32) ** (-jnp.arange(0, axis_dim, 2, dtype=F32) / axis_dim)
    ang_r = jnp.asarray(r.reshape(-1, 1), F32) * inv
    ang_c = jnp.asarray(col.reshape(-1, 1), F32) * inv
    ang = jnp.concatenate([ang_r, ang_r, ang_c, ang_c], axis=-1)
    sign = np.where((np.arange(HEAD_DIM) % 32) < 16, -1.0, 1.0).astype(np.float32)
    cos, sin = jnp.cos(ang), jnp.sin(ang) * sign
    return jnp.tile(cos, (1, N_HEADS)), jnp.tile(sin, (1, N_HEADS))


def _attention(p_q, p_kv, qn_g, kn_g, cos4, sin4, *, tm, tc):
    bsz, t_all, _ = p_q.shape
    nt, nct = t_all // tm, tc // tm
    half = BRANCH_W // 2
    tok = lambda b, j: (b, j, 0)
    c2 = lambda b, j: (0, 0)
    lat = lambda b, j: (jnp.maximum(j - nct, 0), 0)
    qg = jnp.tile(qn_g.astype(F32), N_HEADS).reshape(1, BRANCH_W)
    kg = jnp.tile(kn_g.astype(F32), 2).reshape(1, half)
    bdq = jnp.asarray(_np_block_diag_ones(BRANCH_W, HEAD_DIM), BF16)
    bdk = jnp.asarray(_np_block_diag_ones(half, HEAD_DIM), BF16)
    cosk, sink = cos4[:, 0:half], sin4[:, 0:half]
    return pl.pallas_call(
        functools.partial(_attn_kernel, tm=tm, tc=tc, nct=nct),
        grid=(bsz, nt),
        in_specs=[pl.BlockSpec((None, tm, BRANCH_W), tok),
                  pl.BlockSpec((None, t_all, BRANCH_W), lambda b, j: (b, 0, 0)),
                  pl.BlockSpec((tm, BRANCH_W), lat), pl.BlockSpec((tm, BRANCH_W), lat),
                  pl.BlockSpec(cosk.shape, c2), pl.BlockSpec(sink.shape, c2),
                  pl.BlockSpec((1, BRANCH_W), c2), pl.BlockSpec((1, half), c2),
                  pl.BlockSpec(bdq.shape, c2), pl.BlockSpec(bdk.shape, c2)],
        out_specs=pl.BlockSpec((None, tm, BRANCH_W), tok),
        out_shape=jax.ShapeDtypeStruct((bsz, t_all, BRANCH_W), F32),
        scratch_shapes=[pltpu.VMEM((2, t_all, half), BF16), pltpu.VMEM((2, t_all, half), BF16)],
        compiler_params=pltpu.CompilerParams(dimension_semantics=("parallel", "arbitrary"),
                                             vmem_limit_bytes=V7X_VMEM_LIMIT),
        name="attention",
    )(p_q, p_kv, cos4, sin4, cosk, sink, qg, kg, bdq, bdk)


def _merge_kernel(x_ref, m_ref, g_ref, dnf_ref, dnb_ref, z_ref, sf_ref, sb_ref, u_ref, hf_ref, hb_ref, hgate_ref,
                  at_ref, dng_ref, hgg_ref, dsk_ref, glu_ref, bd_ref, wg_ref, wb_ref, wo_ref, o_ref):
    x = x_ref[...]
    bd = bd_ref[...]
    h = _bf(_rms(x) * g_ref[...] * (1.0 + m_ref[4:5, :]) + m_ref[3:4, :])
    y_dn = _head_rms(dnf_ref[...] + dnb_ref[...], bd, dng_ref[...]) * _silu(z_ref[...])
    ys = sf_ref[...] + sb_ref[...] + dsk_ref[...] * u_ref[...]
    ab = _dot(_bf(jax.nn.gelu(ys, approximate=True)), glu_ref[...])
    y_s5 = ab[:, 0:BRANCH_W] * _sigmoid(ab[:, BRANCH_W:2 * BRANCH_W])
    y_hg = _head_rms(hf_ref[...] + hb_ref[...], bd, hgg_ref[...]) * _sigmoid(hgate_ref[...])
    ys_all = (y_dn, y_s5, y_hg, at_ref[...])
    acc = jnp.zeros(x.shape, F32)
    for i in range(4):
        acc = acc + _sigmoid(_dot(h, wg_ref[i])) * _dot(_bf(ys_all[i]), wb_ref[i])
    o_ref[...] = x + m_ref[5:6, :] * _dot(_bf(acc), wo_ref[...])


def _merge(xj, mods, g, br, params, *, tm, nct, j0):
    bsz, t_all, _ = xj.shape
    n_tiles = t_all // tm - j0
    tok = lambda b, j: (b, j + j0, 0)
    col = lambda c: (lambda b, j: (b, j + j0, c))
    mod_map = lambda b, j: (jnp.where(j + j0 < nct, bsz, b), 0, 0)
    c2 = lambda b, j: (0, 0)
    c3 = lambda b, j: (0, 0, 0)
    tb = (None, tm, BRANCH_W)
    dn_g, hg_g, d_skip, w_glu, wg, wb, wo = params
    bd = jnp.asarray(_np_block_diag_ones(BRANCH_W, HEAD_DIM), BF16)
    return pl.pallas_call(
        _merge_kernel,
        grid=(bsz, n_tiles),
        in_specs=[pl.BlockSpec((None, tm, D_MODEL), tok),
                  pl.BlockSpec((None, N_MOD, D_MODEL), mod_map),
                  pl.BlockSpec((1, D_MODEL), c2),
                  pl.BlockSpec(tb, tok), pl.BlockSpec(tb, tok), pl.BlockSpec(tb, col(3)),
                  pl.BlockSpec(tb, tok), pl.BlockSpec(tb, tok), pl.BlockSpec(tb, tok),
                  pl.BlockSpec(tb, tok), pl.BlockSpec(tb, tok), pl.BlockSpec(tb, col(4)),
                  pl.BlockSpec(tb, tok),
                  pl.BlockSpec((1, BRANCH_W), c2), pl.BlockSpec((1, BRANCH_W), c2), pl.BlockSpec((1, BRANCH_W), c2),
                  pl.BlockSpec(w_glu.shape, c2), pl.BlockSpec(bd.shape, c2),
                  pl.BlockSpec(wg.shape, c3), pl.BlockSpec(wb.shape, c3), pl.BlockSpec(wo.shape, c2)],
        out_specs=pl.BlockSpec((None, tm, D_MODEL), lambda b, j: (b, j, 0)),
        out_shape=jax.ShapeDtypeStruct((bsz, n_tiles * tm, D_MODEL), F32),
        compiler_params=pltpu.CompilerParams(dimension_semantics=("parallel", "parallel"),
                                             vmem_limit_bytes=V7X_VMEM_LIMIT),
        name="branch_merge",
    )(xj, mods, g.reshape(1, D_MODEL), br['dn_f'], br['dn_b'], br['p_dn'], br['s5_f'], br['s5_b'], br['p_s5'],
      br['hg_f'], br['hg_b'], br['p_hg'], br['at'], dn_g, hg_g, d_skip, w_glu, bd, wg, wb, wo)


def kernel(x, c, ctx, c_ctx, ada_w, ada_b, norm_g, ffn_w1, ffn_w3, ffn_w2, w_in, dn_conv, dn_a_log, dn_dt_bias,
           dn_norm_g, s5_lam_re, s5_lam_im, s5_log_step, s5_b_re, s5_b_im, s5_c_re, s5_c_im, s5_d, s5_glu,
           hg_lb_logits, hg_norm_g, at_qn_g, at_kn_g, w_branch, w_out, final_g):
    bsz, tl, _ = x.shape
    tc = ctx.shape[1]
    t_all = tc + tl
    depth = ada_w.shape[0]
    tm = 256 if (tc % 256 == 0 and tl % 256 == 0) else 128
    assert tc % tm == 0 and tl % tm == 0 and tm % CHUNK == 0 and bsz % 8 == 0
    nct = tc // tm
    d_ff = ffn_w1.shape[-1]
    n_ff = d_ff // FF_CHUNK

    n_rows = -(-(bsz + 1) // 8) * 8
    cond = jnp.zeros((n_rows, D_MODEL), F32).at[0:bsz].set(c.astype(F32)).at[bsz].set(c_ctx.astype(F32))
    mods_all = _ada_mods(cond, ada_w.astype(F32), ada_b.astype(F32))
    cos4, sin4 = _rope_tables(tl)
    xj = jnp.concatenate([ctx, x], axis=1).astype(F32)

    def ffn_weights(l, i):
        w1 = ffn_w1[l, i].astype(BF16).reshape(D_MODEL, n_ff, FF_CHUNK).transpose(1, 0, 2)
        w3 = ffn_w3[l, i].astype(BF16).reshape(D_MODEL, n_ff, FF_CHUNK).transpose(1, 0, 2)
        w2 = ffn_w2[l, i].astype(BF16).reshape(n_ff, FF_CHUNK, D_MODEL)
        return w1, w3, w2

    for l in range(depth):
        last = l == depth - 1
        mods = mods_all[l]
        xj = _ffn_half_step(xj, mods, norm_g[l, 0], *ffn_weights(l, 0), mod_base=0, tm=tm, nct=nct)

        wl = w_in[l]
        w_dn = jnp.concatenate([wl[:, 0:1040], jnp.zeros((D_MODEL, 112), wl.dtype)], axis=1)
        ws = [w.astype(BF16) for w in (w_dn, wl[:, 1040:1296], wl[:, 1296:2576], wl[:, 2576:2832], wl[:, 2832:3088])]
        p_dn, p_s5, p_hg, p_aq, p_kv = _input_projection(xj, mods, norm_g[l, 1], ws, tm=tm, nct=nct)

        dn_f, dn_b = _deltanet(p_dn, dn_conv[l], dn_a_log[l], dn_dt_bias[l], tm=tm, tc=tc)
        hg_f, hg_b = _hgrn2(p_hg, hg_lb_logits, l, tc=tc)
        mats = _s5_matrices(s5_lam_re[l], s5_lam_im[l], s5_log_step[l], s5_b_re[l], s5_b_im[l], s5_c_re[l], s5_c_im[l])
        u_tb = p_s5.transpose(1, 0, 2).reshape(t_all * bsz, BRANCH_W)
        s5_f, s5_b = _s5(u_tb, mats, bsz=bsz, tc=tc, tt=CHUNK)
        s5_f = s5_f.reshape(t_all, bsz, BRANCH_W).transpose(1, 0, 2)
        s5_b = s5_b.reshape(t_all, bsz, BRANCH_W).transpose(1, 0, 2)
        y_at = _attention(p_aq, p_kv, at_qn_g[l], at_kn_g[l], cos4, sin4, tm=tm, tc=tc)

        br = dict(dn_f=dn_f, dn_b=dn_b, p_dn=p_dn, s5_f=s5_f, s5_b=s5_b, p_s5=p_s5, hg_f=hg_f, hg_b=hg_b,
                  p_hg=p_hg, at=y_at)
        params = (jnp.tile(dn_norm_g[l].astype(F32), N_HEADS).reshape(1, BRANCH_W),
                  jnp.tile(hg_norm_g[l].astype(F32), N_HEADS).reshape(1, BRANCH_W),
                  s5_d[l].astype(F32).reshape(1, BRANCH_W),
                  s5_glu[l].astype(BF16),
                  wl[:, 3088:].astype(BF16).reshape(D_MODEL, 4, D_MODEL).transpose(1, 0, 2),
                  w_branch[l].astype(BF16),
                  w_out[l].astype(BF16))
        xj = _merge(xj, mods, norm_g[l, 1], br, params, tm=tm, nct=nct, j0=nct if last else 0)
        xj = _ffn_half_step(xj, mods, norm_g[l, 2], *ffn_weights(l, 1), mod_base=6, tm=tm,
                            nct=0 if last else nct, final_g=final_g.astype(F32) if last else None)
    return xj.astype(x.dtype)
```

```python
import functools
import math

import numpy as np
import jax
import jax.numpy as jnp
from jax import lax
from jax.experimental import pallas as pl
from jax.experimental.pallas import tpu as pltpu

F32 = jnp.float32
BF16 = jnp.bfloat16

D_MODEL = 1024
N_MOD = 9
BRANCH_W = 256
HEAD_DIM = 64
N_HEADS = 4
CHUNK = 64
EPS = 1e-6
LOG_FLOOR = 1e-30
GRID_W = 64
ROPE_THETA = 10000.0
FF_CHUNK = 256
S5_GROUPS = 16
S5_GROUP = 16
S5_STATE = 64
S5_LANES = S5_GROUPS * S5_STATE
NEG_BIG = -1e30
V7X_VMEM_LIMIT = 56 * 1024 * 1024


def _bf(x):
    return x.astype(BF16)


def _dot(a, b):
    return jnp.dot(a, b, preferred_element_type=F32)


def _dot_nt(a, b):
    return lax.dot_general(a, b, (((1,), (1,)), ((), ())), preferred_element_type=F32)


def _dot_tn(a, b):
    return lax.dot_general(a, b, (((0,), (0,)), ((), ())), preferred_element_type=F32)


def _split2(x):
    hi = _bf(x)
    lo = _bf(x - hi.astype(F32))
    return hi, lo


def _split3(x):
    x1 = _bf(x)
    r1 = x - x1.astype(F32)
    x2 = _bf(r1)
    x3 = _bf(r1 - x2.astype(F32))
    return x1, x2, x3


def _dot_x2(x, m):
    hi, lo = _split2(x)
    return _dot(hi, m) + _dot(lo, m)


def _dot_x3(x, m):
    x1, x2, x3 = _split3(x)
    return _dot(x1, m) + _dot(x2, m) + _dot(x3, m)


def _dot3_left(m, x):
    x1, x2, x3 = _split3(x)
    return _dot(m, x1) + _dot(m, x2) + _dot(m, x3)


def _dot_hp(a, b):
    a_hi, a_lo = _split2(a)
    b_hi, b_lo = _split2(b)
    return _dot(a_hi, b_hi) + _dot(a_hi, b_lo) + _dot(a_lo, b_hi)


def _sigmoid(x):
    return jax.nn.sigmoid(x)


def _silu(x):
    return x * jax.nn.sigmoid(x)


def _rms(x):
    return x * lax.rsqrt(jnp.mean(x * x, axis=-1, keepdims=True) + EPS)


def _head_rms(x, bd_ones, gain):
    ss = _dot_x2(x * x, bd_ones) * (1.0 / HEAD_DIM)
    return x * lax.rsqrt(ss + EPS) * gain


def _stack4(x):
    return jnp.concatenate([x, x, x, x], axis=0)


def _unstack4(x):
    n = x.shape[0] // 4
    return x[0:n] + x[n:2 * n] + x[2 * n:3 * n] + x[3 * n:4 * n]


def _np_block_diag_ones(n, blk):
    i = np.arange(n)
    return (i[:, None] // blk == i[None, :] // blk).astype(np.float32)


def _np_tri(n, blk, lower, strict):
    i = np.arange(n)
    same = i[:, None] // blk == i[None, :] // blk
    a, b = i[:, None] % blk, i[None, :] % blk
    if lower:
        t = (a > b) if strict else (a >= b)
    else:
        t = (a < b) if strict else (a <= b)
    return (same & t).astype(np.float32)


def _np_doubling_masks(n, blk):
    i = np.arange(n)
    r, c = i[:, None], i[None, :]
    out = [(r // 2 == c // 2)]
    s = 2
    while s < blk:
        out.append((r // (2 * s) == c // (2 * s)) & (r // s != c // s))
        s *= 2
    return np.stack(out).astype(np.float32)


def _bwd_order(s, n_ctx, n_all):
    return jnp.where(s < n_ctx, n_ctx - 1 - s, n_all + n_ctx - 1 - s)


def _ada_kernel(c_ref, w_ref, b_ref, o_ref):
    s = _silu(c_ref[...])
    w = w_ref[...]
    s_hi, s_lo = _split2(s)
    w_hi, w_lo = _split2(w)
    o_ref[...] = _dot(s_hi, w_hi) + _dot(s_hi, w_lo) + _dot(s_lo, w_hi) + b_ref[...]


def _ada_mods(cond, ada_w, ada_b):
    n_layers = ada_w.shape[0]
    r = cond.shape[0]
    n_out = ada_w.shape[2]
    tn = 1152
    out = pl.pallas_call(
        _ada_kernel,
        grid=(n_layers, n_out // tn),
        in_specs=[
            pl.BlockSpec((r, D_MODEL), lambda l, n: (0, 0)),
            pl.BlockSpec((None, D_MODEL, tn), lambda l, n: (l, 0, n)),
            pl.BlockSpec((None, 1, tn), lambda l, n: (l, 0, n)),
        ],
        out_specs=pl.BlockSpec((None, r, tn), lambda l, n: (l, 0, n)),
        out_shape=jax.ShapeDtypeStruct((n_layers, r, n_out), F32),
        compiler_params=pltpu.CompilerParams(dimension_semantics=("parallel", "parallel"),
                                             vmem_limit_bytes=40 * 1024 * 1024),
        name="ada_mods",
    )(cond, ada_w, ada_b.reshape(n_layers, 1, n_out))
    return out.reshape(n_layers, r, N_MOD, D_MODEL)


def _ffn_kernel(*refs, mod_base, final):
    if final:
        x_ref, m_ref, g_ref, w1_ref, w3_ref, w2_ref, fg_ref, o_ref = refs
    else:
        x_ref, m_ref, g_ref, w1_ref, w3_ref, w2_ref, o_ref = refs
    x = x_ref[...]
    shift = m_ref[mod_base:mod_base + 1, :]
    scale = m_ref[mod_base + 1:mod_base + 2, :]
    gate = m_ref[mod_base + 2:mod_base + 3, :]
    h = _bf(_rms(x) * g_ref[...] * (1.0 + scale) + shift)
    us = []
    for c in range(w1_ref.shape[0]):
        a = _dot(h, w1_ref[c])
        us.append(_bf(_silu(a) * _dot(h, w3_ref[c])))
    out = x + 0.5 * gate * _dot(jnp.concatenate(us, axis=1), w2_ref[...])
    if final:
        out = _rms(out) * fg_ref[...]
    o_ref[...] = out


def _ffn_half_step(xj, mods, g, w1c, w3c, w2c, *, mod_base, tm, nct, final_g=None):
    bsz, t_all, _ = xj.shape
    final = final_g is not None

    def tok_map(b, j):
        return (b, j, 0)

    def mod_map(b, j):
        return (jnp.where(j < nct, bsz, b), 0, 0)

    const3 = lambda b, j: (0, 0, 0)
    const2 = lambda b, j: (0, 0)
    in_specs = [
        pl.BlockSpec((None, tm, D_MODEL), tok_map),
        pl.BlockSpec((None, N_MOD, D_MODEL), mod_map),
        pl.BlockSpec((1, D_MODEL), const2),
        pl.BlockSpec(w1c.shape, const3),
        pl.BlockSpec(w3c.shape, const3),
        pl.BlockSpec(w2c.shape, const2),
    ]
    args = [xj, mods, g.reshape(1, D_MODEL), w1c, w3c, w2c]
    if final:
        in_specs.append(pl.BlockSpec((1, D_MODEL), const2))
        args.append(final_g.reshape(1, D_MODEL))
    return pl.pallas_call(
        functools.partial(_ffn_kernel, mod_base=mod_base, final=final),
        grid=(bsz, t_all // tm),
        in_specs=in_specs,
        out_specs=pl.BlockSpec((None, tm, D_MODEL), tok_map),
        out_shape=jax.ShapeDtypeStruct(xj.shape, F32),
        compiler_params=pltpu.CompilerParams(dimension_semantics=("parallel", "parallel"),
                                             vmem_limit_bytes=V7X_VMEM_LIMIT),
        name="ffn_half_step",
    )(*args)


def _inproj_kernel(x_ref, m_ref, g_ref, wdn_ref, ws5_ref, whg_ref, waq_ref, wkv_ref,
                   dn_ref, s5_ref, hg_ref, aq_ref, kv_ref):
    x = x_ref[...]
    h = _bf(_rms(x) * g_ref[...] * (1.0 + m_ref[4:5, :]) + m_ref[3:4, :])
    dn_ref[...] = _dot(h, wdn_ref[...])
    s5_ref[...] = _dot(h, ws5_ref[...])
    hg_ref[...] = _dot(h, whg_ref[...])
    aq_ref[...] = _dot(h, waq_ref[...])
    kv_ref[...] = _dot(h, wkv_ref[...])


def _input_projection(xj, mods, g, ws, *, tm, nct):
    bsz, t_all, _ = xj.shape
    tok_map = lambda b, j: (b, j, 0)
    mod_map = lambda b, j: (jnp.where(j < nct, bsz, b), 0, 0)
    const2 = lambda b, j: (0, 0)
    widths = [w.shape[1] for w in ws]
    return pl.pallas_call(
        _inproj_kernel,
        grid=(bsz, t_all // tm),
        in_specs=[pl.BlockSpec((None, tm, D_MODEL), tok_map),
                  pl.BlockSpec((None, N_MOD, D_MODEL), mod_map),
                  pl.BlockSpec((1, D_MODEL), const2)]
                 + [pl.BlockSpec(w.shape, const2) for w in ws],
        out_specs=[pl.BlockSpec((None, tm, n), tok_map) for n in widths],
        out_shape=[jax.ShapeDtypeStruct((bsz, t_all, n), F32) for n in widths],
        compiler_params=pltpu.CompilerParams(dimension_semantics=("parallel", "parallel"),
                                             vmem_limit_bytes=V7X_VMEM_LIMIT),
        name="input_projection",
    )(xj, mods, g.reshape(1, D_MODEL), *ws)


def _dn_prep_kernel(p_ref, prev_ref, next_ref, cw_ref, al_ref, dtb_ref, e_ref, tril_ref, triu_ref, bd_ref,
                    q_ref, k_ref, v_ref, gb_ref, *, tm, nct, nt):
    j = pl.program_id(1)
    first = jnp.logical_or(j == 0, j == nct)
    last = jnp.logical_or(j == nct - 1, j == nt - 1)
    x = p_ref[:, 0:3 * BRANCH_W]
    pv = prev_ref[...] * jnp.where(first, 0.0, 1.0)
    nx = next_ref[...] * jnp.where(last, 0.0, 1.0)
    ext = jnp.concatenate([pv, x, nx], axis=0)
    n = tm + 16
    acc = jnp.zeros((tm, 3 * BRANCH_W), F32)
    for t in range(5):
        sh = (2 - t) % n
        r = ext if sh == 0 else pltpu.roll(ext, sh, 0)
        acc = acc + r[8:8 + tm] * cw_ref[t:t + 1, :]
    h = _silu(acc)
    bd = bd_ref[...]

    def l2n(t):
        return t * lax.rsqrt(_dot_x2(t * t, bd) + EPS)

    q_ref[...] = l2n(h[:, 0:BRANCH_W]) * (HEAD_DIM ** -0.5)
    k_ref[...] = l2n(h[:, BRANCH_W:2 * BRANCH_W])
    v_ref[...] = h[:, 2 * BRANCH_W:3 * BRANCH_W]

    ab = p_ref[:, 4 * BRANCH_W:4 * BRANCH_W + 128]
    z = ab + dtb_ref[...]
    softplus = jnp.maximum(z, 0.0) + jnp.log1p(jnp.exp(-jnp.abs(z)))
    g = -jnp.exp(al_ref[...]) * softplus
    lane = lax.broadcasted_iota(jnp.int32, ab.shape, 1)
    g = jnp.where(lane < 8, g, 0.0)
    cs_f = _dot3_left(tril_ref[...], g)
    cs_b = _dot3_left(triu_ref[...], g)
    comp = jnp.where(lane < 4, cs_f, jnp.where(lane < 8, cs_b, jnp.where(lane < 16, _sigmoid(ab), 0.0)))
    gb_ref[...] = _dot_x3(comp, e_ref[...])


def _dn_chunks_local(items, bd, it, eye, lvl_ref):
    ks = [_bf(_stack4(c[1]) * bd) for c in items]
    grams = [_dot_nt(x, x) for x in ks]
    decays, a_s = [], []
    for (q, k, v, gc, be, tri_incl, tri_strict, fwd), gram in zip(items, grams):
        col_gc = jnp.sum(gc * it, axis=0, keepdims=True)
        decay = jnp.exp(jnp.where(tri_incl > 0.5, _stack4(gc) - col_gc, NEG_BIG))
        decays.append(decay)
        a_s.append(gram * decay * _stack4(be) * tri_strict)
    ts = [eye - a * lvl_ref[0] for a in a_s]
    for lv in range(1, lvl_ref.shape[0]):
        tbs = [_bf(t) for t in ts]
        xs = [_dot(_bf(a * lvl_ref[lv]), tb) for a, tb in zip(a_s, tbs)]
        ts = [t - _dot(tb, _bf(x)) for t, tb, x in zip(ts, tbs, xs)]
    tbs = [_bf(t) for t in ts]
    egcs = [jnp.exp(c[3]) for c in items]
    us = [_unstack4(_dot(tb, _bf(_stack4(c[2] * c[4]) * bd))) for tb, c in zip(tbs, items)]
    ws = [_unstack4(_dot(tb, _bf(_stack4(c[1] * c[4] * e) * bd))) for tb, c, e in zip(tbs, items, egcs)]
    attns = [_bf(_dot_nt(_bf(_stack4(c[0]) * bd), x) * dc) for c, x, dc in zip(items, ks, decays)]
    out = []
    for (q, k, v, gc, be, _, _, fwd), u, w, attn, egc in zip(items, us, ws, attns, egcs):
        total = gc[CHUNK - 1:CHUNK, :] if fwd else gc[0:1, :]
        out.append((u, _bf(w), attn, _bf(q * egc), _bf(k * jnp.exp(total - gc)), jnp.exp(total)))
    return out


def _dn_chunk_state(local, s_prev, bd):
    u, w, attn, q_dec, k_dec, g_last = local
    sb = _bf(s_prev)
    v_new = u - _dot(w, sb)
    o = _dot(q_dec, sb) + _unstack4(_dot(attn, _bf(_stack4(v_new) * bd)))
    s_new = s_prev * g_last + _dot_tn(k_dec, _bf(v_new)) * bd
    return o, s_new


def _dn_scan_kernel(qf_ref, kf_ref, vf_ref, gcf_ref, bef_ref, qb_ref, kb_ref, vb_ref, gcb_ref, beb_ref,
                    bd_ref, li_ref, ls_ref, ui_ref, us_ref, it_ref, eye_ref, lvl_ref,
                    of_ref, ob_ref, s_ref):
    @pl.when(pl.program_id(1) == 0)
    def _():
        s_ref[...] = jnp.zeros_like(s_ref)

    bd, it, eye = bd_ref[...], it_ref[...], eye_ref[...]
    n_sub = qf_ref.shape[0] // CHUNK
    dirs = ((qf_ref, kf_ref, vf_ref, gcf_ref, bef_ref, li_ref, ls_ref, of_ref, True),
            (qb_ref, kb_ref, vb_ref, gcb_ref, beb_ref, ui_ref, us_ref, ob_ref, False))
    items = []
    for q_ref, k_ref, v_ref, gc_ref, be_ref, ti_ref, ts_ref, _, fwd in dirs:
        for g in range(n_sub):
            r = slice(g * CHUNK, (g + 1) * CHUNK)
            items.append((q_ref[r, :], k_ref[r, :], v_ref[r, :], gc_ref[r, :], be_ref[r, :],
                          ti_ref[...], ts_ref[...], fwd))
    local = _dn_chunks_local(items, bd, it, eye, lvl_ref)
    s_f, s_b = s_ref[0], s_ref[1]
    for i in range(n_sub):
        gb = n_sub - 1 - i
        o, s_f = _dn_chunk_state(local[i], s_f, bd)
        of_ref[i * CHUNK:(i + 1) * CHUNK, :] = o
        o, s_b = _dn_chunk_state(local[n_sub + gb], s_b, bd)
        ob_ref[gb * CHUNK:(gb + 1) * CHUNK, :] = o
    s_ref[0] = s_f
    s_ref[1] = s_b


def _deltanet(p_dn, conv_w, a_log, dt_bias, *, tm, tc):
    bsz, t_all, _ = p_dn.shape
    nt, nct = t_all // tm, tc // tm
    al = jnp.zeros((1, 128), F32).at[0, 0:8].set(a_log.reshape(8).astype(F32))
    dtb = jnp.zeros((1, 128), F32).at[0, 0:8].set(dt_bias.reshape(8).astype(F32))
    e = np.zeros((128, 4 * BRANCH_W), np.float32)
    for jj in range(4):
        for h in range(N_HEADS):
            e[4 * jj + h, jj * BRANCH_W + h * HEAD_DIM: jj * BRANCH_W + (h + 1) * HEAD_DIM] = 1.0
    consts = [jnp.asarray(e, BF16),
              jnp.asarray(_np_tri(tm, CHUNK, True, False), BF16),
              jnp.asarray(_np_tri(tm, CHUNK, False, False), BF16),
              jnp.asarray(_np_block_diag_ones(BRANCH_W, HEAD_DIM), BF16)]
    tok = lambda b, j: (b, j, 0)
    c2 = lambda b, j: (0, 0)
    hb = tm // 8
    q, k, v, gb = pl.pallas_call(
        functools.partial(_dn_prep_kernel, tm=tm, nct=nct, nt=nt),
        grid=(bsz, nt),
        in_specs=[pl.BlockSpec((None, tm, p_dn.shape[2]), tok),
                  pl.BlockSpec((None, 8, 3 * BRANCH_W), lambda b, j: (b, jnp.maximum(j * hb - 1, 0), 0)),
                  pl.BlockSpec((None, 8, 3 * BRANCH_W), lambda b, j: (b, jnp.minimum((j + 1) * hb, nt * hb - 1), 0)),
                  pl.BlockSpec(conv_w.shape, c2), pl.BlockSpec((1, 128), c2), pl.BlockSpec((1, 128), c2)]
                 + [pl.BlockSpec(c.shape, c2) for c in consts],
        out_specs=[pl.BlockSpec((None, tm, BRANCH_W), tok)] * 3 + [pl.BlockSpec((None, tm, 4 * BRANCH_W), tok)],
        out_shape=[jax.ShapeDtypeStruct((bsz, t_all, BRANCH_W), F32)] * 3
                  + [jax.ShapeDtypeStruct((bsz, t_all, 4 * BRANCH_W), F32)],
        compiler_params=pltpu.CompilerParams(dimension_semantics=("parallel", "parallel"),
                                             vmem_limit_bytes=40 * 1024 * 1024),
        name="deltanet_prep",
    )(p_dn, p_dn, p_dn, conv_w.astype(F32), al, dtb, *consts)

    nc, ncc = nt, nct
    n4 = N_HEADS * CHUNK
    masks = [jnp.asarray(_np_block_diag_ones(n4, CHUNK), F32),
             jnp.asarray(_np_tri(n4, CHUNK, True, False), F32),
             jnp.asarray(_np_tri(n4, CHUNK, True, True), F32),
             jnp.asarray(_np_tri(n4, CHUNK, False, False), F32),
             jnp.asarray(_np_tri(n4, CHUNK, False, True), F32),
             jnp.asarray(np.tile(np.eye(CHUNK, dtype=np.float32), (1, N_HEADS)), F32),
             jnp.asarray(np.eye(n4, dtype=np.float32), F32),
             jnp.asarray(_np_doubling_masks(n4, CHUNK), F32)]
    cmap = lambda m: (lambda b, s: (0,) * m.ndim)
    blk = (None, tm, BRANCH_W)
    fmap = lambda col: (lambda b, s: (b, s, col))
    bmap = lambda col: (lambda b, s: (b, _bwd_order(s, ncc, nc), col))
    o_f, o_b = pl.pallas_call(
        _dn_scan_kernel,
        grid=(bsz, nc),
        in_specs=[pl.BlockSpec(blk, fmap(0))] * 3 + [pl.BlockSpec(blk, fmap(0)), pl.BlockSpec(blk, fmap(2))]
                 + [pl.BlockSpec(blk, bmap(0))] * 3 + [pl.BlockSpec(blk, bmap(1)), pl.BlockSpec(blk, bmap(3))]
                 + [pl.BlockSpec(m.shape, cmap(m)) for m in masks],
        out_specs=[pl.BlockSpec(blk, fmap(0)), pl.BlockSpec(blk, bmap(0))],
        out_shape=[jax.ShapeDtypeStruct((bsz, t_all, BRANCH_W), F32)] * 2,
        scratch_shapes=[pltpu.VMEM((2, n4, n4), F32)],
        compiler_params=pltpu.CompilerParams(dimension_semantics=("parallel", "arbitrary"),
                                             vmem_limit_bytes=40 * 1024 * 1024),
        name="deltanet_scan",
    )(q, k, v, gb, gb, q, k, v, gb, gb, *masks)
    return o_f, o_b


def _hg_chunk(q, f, v, lb, tri, st_prev, bd_ones, bd_mask, acc_ref, tmp_ref, fwd):
    qs = _silu(q)
    f_gate = lb + (1.0 - lb) * _sigmoid(f)
    logf = jnp.log(jnp.maximum(f_gate, LOG_FLOOR))
    kk = (1.0 - lb) * _sigmoid(-f)
    b = _dot3_left(tri, logf)
    b_last = b[CHUNK - 1:CHUNK, :] if fwd else b[0:1, :]
    acc_ref[...] = _dot_nt(_bf(qs * jnp.exp(b)), _bf(st_prev))
    tmp_ref[0], tmp_ref[1], tmp_ref[2], tmp_ref[3] = qs, kk, b, v
    for g in range(CHUNK // 8):
        lo, hi = (8 * g, CHUNK) if fwd else (0, 8 * g + 8)
        nr = hi - lo
        qg, bg = tmp_ref[0, lo:hi, :], tmp_ref[2, lo:hi, :]
        tg = lax.broadcasted_iota(jnp.int32, (nr, BRANCH_W), 0) + lo
        rows = []
        for s in range(8 * g, 8 * g + 8):
            m = (tg >= s) if fwd else (tg <= s)
            rows.append(qg * tmp_ref[1, s:s + 1, :] * jnp.exp(jnp.where(m, bg - tmp_ref[2, s:s + 1, :], NEG_BIG)))
        se = _dot(_bf(jnp.concatenate(rows, axis=0)), bd_ones)
        contrib = se[0:nr] * tmp_ref[3, 8 * g:8 * g + 1, :]
        for i in range(1, 8):
            contrib = contrib + se[i * nr:(i + 1) * nr] * tmp_ref[3, 8 * g + i:8 * g + i + 1, :]
        acc_ref[lo:hi, :] += contrib
    k_dec = kk * jnp.exp(b_last - b)
    st_new = st_prev * jnp.exp(b_last) + _dot_tn(_bf(v), _bf(k_dec)) * bd_mask
    return st_new


def _hg_scan_kernel(qf_ref, ff_ref, vf_ref, qb_ref, fb_ref, vb_ref, lg_ref, bdo_ref, bdm_ref, tl_ref, tu_ref,
                    of_ref, ob_ref, st_ref, tmp_ref, *, layer):
    @pl.when(pl.program_id(1) == 0)
    def _():
        st_ref[...] = jnp.zeros_like(st_ref)

    n_layers = lg_ref.shape[0]
    mx = lg_ref[0:1, :]
    for i in range(1, n_layers):
        mx = jnp.maximum(mx, lg_ref[i:i + 1, :])
    den = jnp.zeros_like(mx)
    num = jnp.zeros_like(mx)
    for i in range(n_layers):
        e = jnp.exp(lg_ref[i:i + 1, :] - mx)
        den = den + e
        if 1 <= i <= layer:
            num = num + e
    lb = num / den
    bdo, bdm = bdo_ref[...], bdm_ref[...]
    st_ref[0] = _hg_chunk(qf_ref[...], ff_ref[...], vf_ref[...], lb, tl_ref[...], st_ref[0], bdo, bdm, of_ref,
                          tmp_ref.at[0], True)
    st_ref[1] = _hg_chunk(qb_ref[...], fb_ref[...], vb_ref[...], lb, tu_ref[...], st_ref[1], bdo, bdm, ob_ref,
                          tmp_ref.at[1], False)


def _hgrn2(p_hg, lb_logits, layer, *, tc):
    bsz, t_all, _ = p_hg.shape
    nc, ncc = t_all // CHUNK, tc // CHUNK
    consts = [jnp.asarray(_np_block_diag_ones(BRANCH_W, HEAD_DIM), BF16),
              jnp.asarray(_np_block_diag_ones(BRANCH_W, HEAD_DIM), F32),
              jnp.asarray(_np_tri(CHUNK, CHUNK, True, False), BF16),
              jnp.asarray(_np_tri(CHUNK, CHUNK, False, False), BF16)]
    blk = (None, CHUNK, BRANCH_W)
    c2 = lambda b, s: (0, 0)
    fmap = lambda col: (lambda b, s: (b, s, col))
    bmap = lambda col: (lambda b, s: (b, _bwd_order(s, ncc, nc), col))
    return pl.pallas_call(
        functools.partial(_hg_scan_kernel, layer=layer),
        grid=(bsz, nc),
        in_specs=[pl.BlockSpec(blk, fmap(0)), pl.BlockSpec(blk, fmap(1)), pl.BlockSpec(blk, fmap(3)),
                  pl.BlockSpec(blk, bmap(0)), pl.BlockSpec(blk, bmap(2)), pl.BlockSpec(blk, bmap(3)),
                  pl.BlockSpec(lb_logits.shape, c2)] + [pl.BlockSpec(c.shape, c2) for c in consts],
        out_specs=[pl.BlockSpec(blk, fmap(0)), pl.BlockSpec(blk, bmap(0))],
        out_shape=[jax.ShapeDtypeStruct((bsz, t_all, BRANCH_W), F32)] * 2,
        scratch_shapes=[pltpu.VMEM((2, BRANCH_W, BRANCH_W), F32), pltpu.VMEM((2, 4, CHUNK, BRANCH_W), F32)],
        compiler_params=pltpu.CompilerParams(dimension_semantics=("parallel", "arbitrary"),
                                             vmem_limit_bytes=40 * 1024 * 1024),
        name="hgrn2_scan",
    )(p_hg, p_hg, p_hg, p_hg, p_hg, p_hg, lb_logits.astype(F32), *consts)


def _s5_kernel(uf_ref, ub_ref, bb_ref, cc_ref, a_ref, yf_ref, yb_ref, x_ref, h_ref, *, tt, bsz):
    @pl.when(pl.program_id(0) == 0)
    def _():
        h_ref[...] = jnp.zeros_like(h_ref)

    x_ref[0] = _dot(_bf(uf_ref[...]), bb_ref[0])
    x_ref[1] = _dot(_bf(ub_ref[...]), bb_ref[1])

    def step(i, carry):
        hfr, hfi, hbr, hbi = carry
        rf = pl.multiple_of(i * bsz, bsz)
        rb = pl.multiple_of((tt - 1 - i) * bsz, bsz)
        ar, ai = a_ref[0, 0], a_ref[0, 1]
        xr = x_ref[0, pl.ds(rf, bsz), 0:S5_LANES]
        xi = x_ref[0, pl.ds(rf, bsz), S5_LANES:2 * S5_LANES]
        nfr = ar * hfr - ai * hfi + xr
        nfi = ar * hfi + ai * hfr + xi
        x_ref[0, pl.ds(rf, bsz), 0:S5_LANES] = nfr
        x_ref[0, pl.ds(rf, bsz), S5_LANES:2 * S5_LANES] = nfi
        ar, ai = a_ref[1, 0], a_ref[1, 1]
        xr = x_ref[1, pl.ds(rb, bsz), 0:S5_LANES]
        xi = x_ref[1, pl.ds(rb, bsz), S5_LANES:2 * S5_LANES]
        nbr = ar * hbr - ai * hbi + xr
        nbi = ar * hbi + ai * hbr + xi
        x_ref[1, pl.ds(rb, bsz), 0:S5_LANES] = nbr
        x_ref[1, pl.ds(rb, bsz), S5_LANES:2 * S5_LANES] = nbi
        return nfr, nfi, nbr, nbi

    hs = lax.fori_loop(0, tt, step, (h_ref[0, 0], h_ref[0, 1], h_ref[1, 0], h_ref[1, 1]))
    h_ref[0, 0], h_ref[0, 1], h_ref[1, 0], h_ref[1, 1] = hs
    yf_ref[...] = _dot(_bf(x_ref[0]), cc_ref[0])
    yb_ref[...] = _dot(_bf(x_ref[1]), cc_ref[1])


def _s5_matrices(lam_re, lam_im, log_step, b_re, b_im, c_re, c_im):
    lam_re, lam_im = lam_re.astype(F32), lam_im.astype(F32)
    step = jnp.exp(log_step.astype(F32))[..., None]
    mag = jnp.exp(lam_re * step)
    a_re, a_im = mag * jnp.cos(lam_im * step), mag * jnp.sin(lam_im * step)
    den = lam_re * lam_re + lam_im * lam_im
    n_re = a_re - 1.0
    coef_re = (n_re * lam_re + a_im * lam_im) / den
    coef_im = (a_im * lam_re - n_re * lam_im) / den
    b_re, b_im = b_re.astype(F32), b_im.astype(F32)
    bb_re = coef_re[..., None] * b_re - coef_im[..., None] * b_im
    bb_im = coef_re[..., None] * b_im + coef_im[..., None] * b_re
    eye_g = jnp.eye(S5_GROUPS, dtype=F32)
    bb = jnp.stack([bb_re, bb_im], axis=1)
    bb = jnp.einsum('drgph,gk->dghrkp', bb, eye_g).reshape(2, BRANCH_W, 2 * S5_LANES)
    cc = jnp.stack([c_re.astype(F32), -c_im.astype(F32)], axis=0)
    cc = jnp.einsum('rgkp,gj->rjpgk', cc, eye_g).reshape(2 * S5_LANES, BRANCH_W)
    return bb.astype(BF16), cc.astype(BF16), a_re.reshape(2, S5_LANES), a_im.reshape(2, S5_LANES)


def _s5(u_tb, mats, *, bsz, tc, tt):
    bb, cc, a_re, a_im = mats
    n_rows = u_tb.shape[0]
    t_all = n_rows // bsz
    nt, nct = t_all // tt, tc // tt
    a = jnp.broadcast_to(jnp.stack([a_re, a_im], axis=1)[:, :, None, :], (2, 2, bsz, S5_LANES))
    cc2 = jnp.broadcast_to(cc[None], (2,) + cc.shape)
    blk = (tt * bsz, BRANCH_W)
    fmap = lambda s: (s, 0)
    bmap = lambda s: (_bwd_order(s, nct, nt), 0)
    return pl.pallas_call(
        functools.partial(_s5_kernel, tt=tt, bsz=bsz),
        grid=(nt,),
        in_specs=[pl.BlockSpec(blk, fmap), pl.BlockSpec(blk, bmap),
                  pl.BlockSpec(bb.shape, lambda s: (0, 0, 0)), pl.BlockSpec(cc2.shape, lambda s: (0, 0, 0)),
                  pl.BlockSpec(a.shape, lambda s: (0, 0, 0, 0))],
        out_specs=[pl.BlockSpec(blk, fmap), pl.BlockSpec(blk, bmap)],
        out_shape=[jax.ShapeDtypeStruct((n_rows, BRANCH_W), F32)] * 2,
        scratch_shapes=[pltpu.VMEM((2, tt * bsz, 2 * S5_LANES), F32), pltpu.VMEM((2, 2, bsz, S5_LANES), F32)],
        compiler_params=pltpu.CompilerParams(dimension_semantics=("arbitrary",),
                                             vmem_limit_bytes=V7X_VMEM_LIMIT),
        name="s5_scan",
    )(u_tb, u_tb, bb, cc2, a)


def _rope(x, cos, sin_signed):
    w = x.shape[-1]
    lane = lax.broadcasted_iota(jnp.int32, x.shape, 1)
    up = pltpu.roll(x, w - 16, 1)
    dn = pltpu.roll(x, 16, 1)
    return x * cos + jnp.where((lane & 31) < 16, up, dn) * sin_signed


def _attn_kernel(q_ref, kv_ref, cq_ref, sq_ref, ck_ref, sk_ref, qg_ref, kg_ref, bdq_ref, bdk_ref,
                 o_ref, kp_ref, vp_ref, *, tm, tc, nct):
    j = pl.program_id(1)
    t_all = kv_ref.shape[0]
    half = BRANCH_W // 2

    @pl.when(j == 0)
    def _():
        def pairs(x):
            lane = lax.broadcasted_iota(jnp.int32, x.shape, 1)
            sw = pltpu.roll(x, HEAD_DIM, 1)
            return _bf(jnp.where(lane < HEAD_DIM, x, sw)), _bf(jnp.where(lane < HEAD_DIM, sw, x))

        for lo, hi, rope in ((0, tc, False), (tc, t_all, True)):
            k = _head_rms(kv_ref[lo:hi, 0:half], bdk_ref[...], kg_ref[...])
            if rope:
                k = _rope(k, ck_ref[...], sk_ref[...])
            kp_ref[0, lo:hi, :], kp_ref[1, lo:hi, :] = pairs(k)
            vp_ref[0, lo:hi, :], vp_ref[1, lo:hi, :] = pairs(kv_ref[lo:hi, half:2 * half])

    def attend(q, nk):
        lane = lax.broadcasted_iota(jnp.int32, (tm, half), 1)
        outs = []
        for g in range(2):
            qp = q[:, g * half:(g + 1) * half]
            acc = jnp.zeros((tm, half), F32)
            for hh in range(2):
                sel = (lane < HEAD_DIM) if hh == 0 else (lane >= HEAD_DIM)
                s = _dot_nt(_bf(jnp.where(sel, qp, 0.0)), kp_ref[g, 0:nk, :])
                p = jnp.exp(s - jnp.max(s, axis=-1, keepdims=True))
                o = _dot(_bf(p), vp_ref[g, 0:nk, :]) / jnp.sum(p, axis=-1, keepdims=True)
                acc = acc + jnp.where(sel, o, 0.0)
            outs.append(acc)
        o_ref[...] = jnp.concatenate(outs, axis=1)

    qn = _head_rms(q_ref[...], bdq_ref[...], qg_ref[...]) * (HEAD_DIM ** -0.5)

    @pl.when(j < nct)
    def _():
        attend(qn, tc)

    @pl.when(j >= nct)
    def _():
        attend(_rope(qn, cq_ref[...], sq_ref[...]), t_all)


def _rope_tables(n_tokens):
    rows = n_tokens // GRID_W
    r, col = np.meshgrid(np.arange(rows), np.arange(GRID_W), indexing='ij')
    axis_dim = HEAD_DIM // 2
    inv = jnp.asarray(ROPE_THETA, F32) ** (-jnp.arange(0, axis_dim, 2, dtype=F32) / axis_dim)
    ang_r = jnp.asarray(r.reshape(-1, 1), F32) * inv
    ang_c = jnp.asarray(col.reshape(-1, 1), F32) * inv
    ang = jnp.concatenate([ang_r, ang_r, ang_c, ang_c], axis=-1)
    sign = np.where((np.arange(HEAD_DIM) % 32) < 16, -1.0, 1.0).astype(np.float32)
    cos, sin = jnp.cos(ang), jnp.sin(ang) * sign
    return jnp.tile(cos, (1, N_HEADS)), jnp.tile(sin, (1, N_HEADS))


def _attention(p_q, p_kv, qn_g, kn_g, cos4, sin4, *, tm, tc):
    bsz, t_all, _ = p_q.shape
    nt, nct = t_all // tm, tc // tm
    half = BRANCH_W // 2
    tok = lambda b, j: (b, j, 0)
    c2 = lambda b, j: (0, 0)
    lat = lambda b, j: (jnp.maximum(j - nct, 0), 0)
    qg = jnp.tile(qn_g.astype(F32), N_HEADS).reshape(1, BRANCH_W)
    kg = jnp.tile(kn_g.astype(F32), 2).reshape(1, half)
    bdq = jnp.asarray(_np_block_diag_ones(BRANCH_W, HEAD_DIM), BF16)
    bdk = jnp.asarray(_np_block_diag_ones(half, HEAD_DIM), BF16)
    cosk, sink = cos4[:, 0:half], sin4[:, 0:half]
    return pl.pallas_call(
        functools.partial(_attn_kernel, tm=tm, tc=tc, nct=nct),
        grid=(bsz, nt),
        in_specs=[pl.BlockSpec((None, tm, BRANCH_W), tok),
                  pl.BlockSpec((None, t_all, BRANCH_W), lambda b, j: (b, 0, 0)),
                  pl.BlockSpec((tm, BRANCH_W), lat), pl.BlockSpec((tm, BRANCH_W), lat),
                  pl.BlockSpec(cosk.shape, c2), pl.BlockSpec(sink.shape, c2),
                  pl.BlockSpec((1, BRANCH_W), c2), pl.BlockSpec((1, half), c2),
                  pl.BlockSpec(bdq.shape, c2), pl.BlockSpec(bdk.shape, c2)],
        out_specs=pl.BlockSpec((None, tm, BRANCH_W), tok),
        out_shape=jax.ShapeDtypeStruct((bsz, t_all, BRANCH_W), F32),
        scratch_shapes=[pltpu.VMEM((2, t_all, half), BF16), pltpu.VMEM((2, t_all, half), BF16)],
        compiler_params=pltpu.CompilerParams(dimension_semantics=("parallel", "arbitrary"),
                                             vmem_limit_bytes=V7X_VMEM_LIMIT),
        name="attention",
    )(p_q, p_kv, cos4, sin4, cosk, sink, qg, kg, bdq, bdk)


def _merge_kernel(x_ref, m_ref, g_ref, dnf_ref, dnb_ref, z_ref, sf_ref, sb_ref, u_ref, hf_ref, hb_ref, hgate_ref,
                  at_ref, dng_ref, hgg_ref, dsk_ref, glu_ref, bd_ref, wg_ref, wb_ref, wo_ref, o_ref):
    x = x_ref[...]
    bd = bd_ref[...]
    h = _bf(_rms(x) * g_ref[...] * (1.0 + m_ref[4:5, :]) + m_ref[3:4, :])
    y_dn = _head_rms(dnf_ref[...] + dnb_ref[...], bd, dng_ref[...]) * _silu(z_ref[...])
    ys = sf_ref[...] + sb_ref[...] + dsk_ref[...] * u_ref[...]
    ab = _dot(_bf(jax.nn.gelu(ys, approximate=True)), glu_ref[...])
    y_s5 = ab[:, 0:BRANCH_W] * _sigmoid(ab[:, BRANCH_W:2 * BRANCH_W])
    y_hg = _head_rms(hf_ref[...] + hb_ref[...], bd, hgg_ref[...]) * _sigmoid(hgate_ref[...])
    ys_all = (y_dn, y_s5, y_hg, at_ref[...])
    acc = jnp.zeros(x.shape, F32)
    for i in range(4):
        acc = acc + _sigmoid(_dot(h, wg_ref[i])) * _dot(_bf(ys_all[i]), wb_ref[i])
    o_ref[...] = x + m_ref[5:6, :] * _dot(_bf(acc), wo_ref[...])


def _merge(xj, mods, g, br, params, *, tm, nct, j0):
    bsz, t_all, _ = xj.shape
    n_tiles = t_all // tm - j0
    tok = lambda b, j: (b, j + j0, 0)
    col = lambda c: (lambda b, j: (b, j + j0, c))
    mod_map = lambda b, j: (jnp.where(j + j0 < nct, bsz, b), 0, 0)
    c2 = lambda b, j: (0, 0)
    c3 = lambda b, j: (0, 0, 0)
    tb = (None, tm, BRANCH_W)
    dn_g, hg_g, d_skip, w_glu, wg, wb, wo = params
    bd = jnp.asarray(_np_block_diag_ones(BRANCH_W, HEAD_DIM), BF16)
    return pl.pallas_call(
        _merge_kernel,
        grid=(bsz, n_tiles),
        in_specs=[pl.BlockSpec((None, tm, D_MODEL), tok),
                  pl.BlockSpec((None, N_MOD, D_MODEL), mod_map),
                  pl.BlockSpec((1, D_MODEL), c2),
                  pl.BlockSpec(tb, tok), pl.BlockSpec(tb, tok), pl.BlockSpec(tb, col(3)),
                  pl.BlockSpec(tb, tok), pl.BlockSpec(tb, tok), pl.BlockSpec(tb, tok),
                  pl.BlockSpec(tb, tok), pl.BlockSpec(tb, tok), pl.BlockSpec(tb, col(4)),
                  pl.BlockSpec(tb, tok),
                  pl.BlockSpec((1, BRANCH_W), c2), pl.BlockSpec((1, BRANCH_W), c2), pl.BlockSpec((1, BRANCH_W), c2),
                  pl.BlockSpec(w_glu.shape, c2), pl.BlockSpec(bd.shape, c2),
                  pl.BlockSpec(wg.shape, c3), pl.BlockSpec(wb.shape, c3), pl.BlockSpec(wo.shape, c2)],
        out_specs=pl.BlockSpec((None, tm, D_MODEL), lambda b, j: (b, j, 0)),
        out_shape=jax.ShapeDtypeStruct((bsz, n_tiles * tm, D_MODEL), F32),
        compiler_params=pltpu.CompilerParams(dimension_semantics=("parallel", "parallel"),
                                             vmem_limit_bytes=V7X_VMEM_LIMIT),
        name="branch_merge",
    )(xj, mods, g.reshape(1, D_MODEL), br['dn_f'], br['dn_b'], br['p_dn'], br['s5_f'], br['s5_b'], br['p_s5'],
      br['hg_f'], br['hg_b'], br['p_hg'], br['at'], dn_g, hg_g, d_skip, w_glu, bd, wg, wb, wo)


def kernel(x, c, ctx, c_ctx, ada_w, ada_b, norm_g, ffn_w1, ffn_w3, ffn_w2, w_in, dn_conv, dn_a_log, dn_dt_bias,
           dn_norm_g, s5_lam_re, s5_lam_im, s5_log_step, s5_b_re, s5_b_im, s5_c_re, s5_c_im, s5_d, s5_glu,
           hg_lb_logits, hg_norm_g, at_qn_g, at_kn_g, w_branch, w_out, final_g):
    bsz, tl, _ = x.shape
    tc = ctx.shape[1]
    t_all = tc + tl
    depth = ada_w.shape[0]
    tm = 256 if (tc % 256 == 0 and tl % 256 == 0) else 128
    assert tc % tm == 0 and tl % tm == 0 and tm % CHUNK == 0 and bsz % 8 == 0
    nct = tc // tm
    d_ff = ffn_w1.shape[-1]
    n_ff = d_ff // FF_CHUNK

    n_rows = -(-(bsz + 1) // 8) * 8
    cond = jnp.zeros((n_rows, D_MODEL), F32).at[0:bsz].set(c.astype(F32)).at[bsz].set(c_ctx.astype(F32))
    mods_all = _ada_mods(cond, ada_w.astype(F32), ada_b.astype(F32))
    cos4, sin4 = _rope_tables(tl)
    xj = jnp.concatenate([ctx, x], axis=1).astype(F32)

    def ffn_weights(l, i):
        w1 = ffn_w1[l, i].astype(BF16).reshape(D_MODEL, n_ff, FF_CHUNK).transpose(1, 0, 2)
        w3 = ffn_w3[l, i].astype(BF16).reshape(D_MODEL, n_ff, FF_CHUNK).transpose(1, 0, 2)
        w2 = ffn_w2[l, i].astype(BF16)
        return w1, w3, w2

    for l in range(depth):
        last = l == depth - 1
        mods = mods_all[l]
        xj = _ffn_half_step(xj, mods, norm_g[l, 0], *ffn_weights(l, 0), mod_base=0, tm=tm, nct=nct)

        wl = w_in[l]
        w_dn = jnp.concatenate([wl[:, 0:1040], jnp.zeros((D_MODEL, 112), wl.dtype)], axis=1)
        ws = [w.astype(BF16) for w in (w_dn, wl[:, 1040:1296], wl[:, 1296:2576], wl[:, 2576:2832], wl[:, 2832:3088])]
        p_dn, p_s5, p_hg, p_aq, p_kv = _input_projection(xj, mods, norm_g[l, 1], ws, tm=tm, nct=nct)

        dn_f, dn_b = _deltanet(p_dn, dn_conv[l], dn_a_log[l], dn_dt_bias[l], tm=tm, tc=tc)
        hg_f, hg_b = _hgrn2(p_hg, hg_lb_logits, l, tc=tc)
        mats = _s5_matrices(s5_lam_re[l], s5_lam_im[l], s5_log_step[l], s5_b_re[l], s5_b_im[l], s5_c_re[l], s5_c_im[l])
        u_tb = p_s5.transpose(1, 0, 2).reshape(t_all * bsz, BRANCH_W)
        s5_f, s5_b = _s5(u_tb, mats, bsz=bsz, tc=tc, tt=CHUNK)
        s5_f = s5_f.reshape(t_all, bsz, BRANCH_W).transpose(1, 0, 2)
        s5_b = s5_b.reshape(t_all, bsz, BRANCH_W).transpose(1, 0, 2)
        y_at = _attention(p_aq, p_kv, at_qn_g[l], at_kn_g[l], cos4, sin4, tm=tm, tc=tc)

        br = dict(dn_f=dn_f, dn_b=dn_b, p_dn=p_dn, s5_f=s5_f, s5_b=s5_b, p_s5=p_s5, hg_f=hg_f, hg_b=hg_b,
                  p_hg=p_hg, at=y_at)
        params = (jnp.tile(dn_norm_g[l].astype(F32), N_HEADS).reshape(1, BRANCH_W),
                  jnp.tile(hg_norm_g[l].astype(F32), N_HEADS).reshape(1, BRANCH_W),
                  s5_d[l].astype(F32).reshape(1, BRANCH_W),
                  s5_glu[l].astype(BF16),
                  wl[:, 3088:].astype(BF16).reshape(D_MODEL, 4, D_MODEL).transpose(1, 0, 2),
                  w_branch[l].astype(BF16),
                  w_out[l].astype(BF16))
        xj = _merge(xj, mods, norm_g[l, 1], br, params, tm=tm, nct=nct, j0=nct if last else 0)
        xj = _ffn_half_step(xj, mods, norm_g[l, 2], *ffn_weights(l, 1), mod_base=6, tm=tm,
                            nct=0 if last else nct, final_g=final_g.astype(F32) if last else None)
    return xj.astype(x.dtype)
```

```python
import functools
import math

import numpy as np
import jax
import jax.numpy as jnp
from jax import lax
from jax.experimental import pallas as pl
from jax.experimental.pallas import tpu as pltpu

F32 = jnp.float32
BF16 = jnp.bfloat16

D_MODEL = 1024
N_MOD = 9
BRANCH_W = 256
HEAD_DIM = 64
N_HEADS = 4
CHUNK = 64
EPS = 1e-6
LOG_FLOOR = 1e-30
GRID_W = 64
ROPE_THETA = 10000.0
FF_CHUNK = 256
S5_GROUPS = 16
S5_GROUP = 16
S5_STATE = 64
S5_LANES = S5_GROUPS * S5_STATE
NEG_BIG = -1e30
V7X_VMEM_LIMIT = 56 * 1024 * 1024


def _bf(x):
    return x.astype(BF16)


def _dot(a, b):
    return jnp.dot(a, b, preferred_element_type=F32)


def _dot_nt(a, b):
    return lax.dot_general(a, b, (((1,), (1,)), ((), ())), preferred_element_type=F32)


def _dot_tn(a, b):
    return lax.dot_general(a, b, (((0,), (0,)), ((), ())), preferred_element_type=F32)


def _split2(x):
    hi = _bf(x)
    lo = _bf(x - hi.astype(F32))
    return hi, lo


def _split3(x):
    x1 = _bf(x)
    r1 = x - x1.astype(F32)
    x2 = _bf(r1)
    x3 = _bf(r1 - x2.astype(F32))
    return x1, x2, x3


def _dot_x2(x, m):
    hi, lo = _split2(x)
    return _dot(hi, m) + _dot(lo, m)


def _dot_x3(x, m):
    x1, x2, x3 = _split3(x)
    return _dot(x1, m) + _dot(x2, m) + _dot(x3, m)


def _dot3_left(m, x):
    x1, x2, x3 = _split3(x)
    return _dot(m, x1) + _dot(m, x2) + _dot(m, x3)


def _dot_hp(a, b):
    a_hi, a_lo = _split2(a)
    b_hi, b_lo = _split2(b)
    return _dot(a_hi, b_hi) + _dot(a_hi, b_lo) + _dot(a_lo, b_hi)


def _sigmoid(x):
    return jax.nn.sigmoid(x)


def _silu(x):
    return x * jax.nn.sigmoid(x)


def _rms(x):
    return x * lax.rsqrt(jnp.mean(x * x, axis=-1, keepdims=True) + EPS)


def _head_rms(x, bd_ones, gain):
    ss = _dot_x2(x * x, bd_ones) * (1.0 / HEAD_DIM)
    return x * lax.rsqrt(ss + EPS) * gain


def _stack4(x):
    return jnp.concatenate([x, x, x, x], axis=0)


def _unstack4(x):
    n = x.shape[0] // 4
    return x[0:n] + x[n:2 * n] + x[2 * n:3 * n] + x[3 * n:4 * n]


def _np_block_diag_ones(n, blk):
    i = np.arange(n)
    return (i[:, None] // blk == i[None, :] // blk).astype(np.float32)


def _np_tri(n, blk, lower, strict):
    i = np.arange(n)
    same = i[:, None] // blk == i[None, :] // blk
    a, b = i[:, None] % blk, i[None, :] % blk
    if lower:
        t = (a > b) if strict else (a >= b)
    else:
        t = (a < b) if strict else (a <= b)
    return (same & t).astype(np.float32)


def _np_doubling_masks(n, blk):
    i = np.arange(n)
    r, c = i[:, None], i[None, :]
    out = [(r // 2 == c // 2)]
    s = 2
    while s < blk:
        out.append((r // (2 * s) == c // (2 * s)) & (r // s != c // s))
        s *= 2
    return np.stack(out).astype(np.float32)


def _bwd_order(s, n_ctx, n_all):
    return jnp.where(s < n_ctx, n_ctx - 1 - s, n_all + n_ctx - 1 - s)


def _ada_kernel(c_ref, w_ref, b_ref, o_ref):
    s = _silu(c_ref[...])
    w = w_ref[...]
    s_hi, s_lo = _split2(s)
    w_hi, w_lo = _split2(w)
    o_ref[...] = _dot(s_hi, w_hi) + _dot(s_hi, w_lo) + _dot(s_lo, w_hi) + b_ref[...]


def _ada_mods(cond, ada_w, ada_b):
    n_layers = ada_w.shape[0]
    r = cond.shape[0]
    n_out = ada_w.shape[2]
    tn = 1152
    out = pl.pallas_call(
        _ada_kernel,
        grid=(n_layers, n_out // tn),
        in_specs=[
            pl.BlockSpec((r, D_MODEL), lambda l, n: (0, 0)),
            pl.BlockSpec((None, D_MODEL, tn), lambda l, n: (l, 0, n)),
            pl.BlockSpec((None, 1, tn), lambda l, n: (l, 0, n)),
        ],
        out_specs=pl.BlockSpec((None, r, tn), lambda l, n: (l, 0, n)),
        out_shape=jax.ShapeDtypeStruct((n_layers, r, n_out), F32),
        compiler_params=pltpu.CompilerParams(dimension_semantics=("parallel", "parallel"),
                                             vmem_limit_bytes=40 * 1024 * 1024),
        name="ada_mods",
    )(cond, ada_w, ada_b.reshape(n_layers, 1, n_out))
    return out.reshape(n_layers, r, N_MOD, D_MODEL)


def _ffn_kernel(*refs, mod_base, final):
    if final:
        x_ref, m_ref, g_ref, w1_ref, w3_ref, w2_ref, fg_ref, o_ref = refs
    else:
        x_ref, m_ref, g_ref, w1_ref, w3_ref, w2_ref, o_ref = refs
    x = x_ref[...]
    shift = m_ref[mod_base:mod_base + 1, :]
    scale = m_ref[mod_base + 1:mod_base + 2, :]
    gate = m_ref[mod_base + 2:mod_base + 3, :]
    h = _bf(_rms(x) * g_ref[...] * (1.0 + scale) + shift)
    us = []
    for c in range(w1_ref.shape[0]):
        a = _dot(h, w1_ref[c])
        us.append(_bf(_silu(a) * _dot(h, w3_ref[c])))
    out = x + 0.5 * gate * _dot(jnp.concatenate(us, axis=1), w2_ref[...])
    if final:
        out = _rms(out) * fg_ref[...]
    o_ref[...] = out


def _ffn_half_step(xj, mods, g, w1c, w3c, w2c, *, mod_base, tm, nct, final_g=None):
    bsz, t_all, _ = xj.shape
    final = final_g is not None

    def tok_map(b, j):
        return (b, j, 0)

    def mod_map(b, j):
        return (jnp.where(j < nct, bsz, b), 0, 0)

    const3 = lambda b, j: (0, 0, 0)
    const2 = lambda b, j: (0, 0)
    in_specs = [
        pl.BlockSpec((None, tm, D_MODEL), tok_map),
        pl.BlockSpec((None, N_MOD, D_MODEL), mod_map),
        pl.BlockSpec((1, D_MODEL), const2),
        pl.BlockSpec(w1c.shape, const3),
        pl.BlockSpec(w3c.shape, const3),
        pl.BlockSpec(w2c.shape, const2),
    ]
    args = [xj, mods, g.reshape(1, D_MODEL), w1c, w3c, w2c]
    if final:
        in_specs.append(pl.BlockSpec((1, D_MODEL), const2))
        args.append(final_g.reshape(1, D_MODEL))
    return pl.pallas_call(
        functools.partial(_ffn_kernel, mod_base=mod_base, final=final),
        grid=(bsz, t_all // tm),
        in_specs=in_specs,
        out_specs=pl.BlockSpec((None, tm, D_MODEL), tok_map),
        out_shape=jax.ShapeDtypeStruct(xj.shape, F32),
        compiler_params=pltpu.CompilerParams(dimension_semantics=("parallel", "parallel"),
                                             vmem_limit_bytes=V7X_VMEM_LIMIT),
        name="ffn_half_step",
    )(*args)


def _inproj_kernel(x_ref, m_ref, g_ref, wdn_ref, ws5_ref, whg_ref, waq_ref, wkv_ref,
                   dn_ref, s5_ref, hg_ref, aq_ref, kv_ref):
    x = x_ref[...]
    h = _bf(_rms(x) * g_ref[...] * (1.0 + m_ref[4:5, :]) + m_ref[3:4, :])
    dn_ref[...] = _dot(h, wdn_ref[...])
    s5_ref[...] = _dot(h, ws5_ref[...])
    hg_ref[...] = _dot(h, whg_ref[...])
    aq_ref[...] = _dot(h, waq_ref[...])
    kv_ref[...] = _dot(h, wkv_ref[...])


def _input_projection(xj, mods, g, ws, *, tm, nct):
    bsz, t_all, _ = xj.shape
    tok_map = lambda b, j: (b, j, 0)
    mod_map = lambda b, j: (jnp.where(j < nct, bsz, b), 0, 0)
    const2 = lambda b, j: (0, 0)
    widths = [w.shape[1] for w in ws]
    return pl.pallas_call(
        _inproj_kernel,
        grid=(bsz, t_all // tm),
        in_specs=[pl.BlockSpec((None, tm, D_MODEL), tok_map),
                  pl.BlockSpec((None, N_MOD, D_MODEL), mod_map),
                  pl.BlockSpec((1, D_MODEL), const2)]
                 + [pl.BlockSpec(w.shape, const2) for w in ws],
        out_specs=[pl.BlockSpec((None, tm, n), tok_map) for n in widths],
        out_shape=[jax.ShapeDtypeStruct((bsz, t_all, n), F32) for n in widths],
        compiler_params=pltpu.CompilerParams(dimension_semantics=("parallel", "parallel"),
                                             vmem_limit_bytes=V7X_VMEM_LIMIT),
        name="input_projection",
    )(xj, mods, g.reshape(1, D_MODEL), *ws)


def _dn_prep_kernel(p_ref, prev_ref, next_ref, cw_ref, al_ref, dtb_ref, e_ref, tril_ref, triu_ref, bd_ref,
                    q_ref, k_ref, v_ref, gb_ref, *, tm, nct, nt):
    j = pl.program_id(1)
    first = jnp.logical_or(j == 0, j == nct)
    last = jnp.logical_or(j == nct - 1, j == nt - 1)
    x = p_ref[:, 0:3 * BRANCH_W]
    pv = prev_ref[...] * jnp.where(first, 0.0, 1.0)
    nx = next_ref[...] * jnp.where(last, 0.0, 1.0)
    ext = jnp.concatenate([pv, x, nx], axis=0)
    n = tm + 16
    acc = jnp.zeros((tm, 3 * BRANCH_W), F32)
    for t in range(5):
        sh = (2 - t) % n
        r = ext if sh == 0 else pltpu.roll(ext, sh, 0)
        acc = acc + r[8:8 + tm] * cw_ref[t:t + 1, :]
    h = _silu(acc)
    bd = bd_ref[...]

    def l2n(t):
        return t * lax.rsqrt(_dot_x2(t * t, bd) + EPS)

    q_ref[...] = l2n(h[:, 0:BRANCH_W]) * (HEAD_DIM ** -0.5)
    k_ref[...] = l2n(h[:, BRANCH_W:2 * BRANCH_W])
    v_ref[...] = h[:, 2 * BRANCH_W:3 * BRANCH_W]

    ab = p_ref[:, 4 * BRANCH_W:4 * BRANCH_W + 128]
    z = ab + dtb_ref[...]
    softplus = jnp.maximum(z, 0.0) + jnp.log1p(jnp.exp(-jnp.abs(z)))
    g = -jnp.exp(al_ref[...]) * softplus
    lane = lax.broadcasted_iota(jnp.int32, ab.shape, 1)
    g = jnp.where(lane < 8, g, 0.0)
    cs_f = _dot3_left(tril_ref[...], g)
    cs_b = _dot3_left(triu_ref[...], g)
    comp = jnp.where(lane < 4, cs_f, jnp.where(lane < 8, cs_b, jnp.where(lane < 16, _sigmoid(ab), 0.0)))
    gb_ref[...] = _dot_x3(comp, e_ref[...])


def _dn_chunks_local(items, bd, it, eye, lvl_ref):
    ks = [_bf(_stack4(c[1]) * bd) for c in items]
    grams = [_dot_nt(x, x) for x in ks]
    decays, a_s = [], []
    for (q, k, v, gc, be, tri_incl, tri_strict, fwd), gram in zip(items, grams):
        col_gc = jnp.sum(gc * it, axis=0, keepdims=True)
        decay = jnp.exp(jnp.where(tri_incl > 0.5, _stack4(gc) - col_gc, NEG_BIG))
        decays.append(decay)
        a_s.append(gram * decay * _stack4(be) * tri_strict)
    ts = [eye - a * lvl_ref[0] for a in a_s]
    for lv in range(1, lvl_ref.shape[0]):
        tbs = [_bf(t) for t in ts]
        xs = [_dot(_bf(a * lvl_ref[lv]), tb) for a, tb in zip(a_s, tbs)]
        ts = [t - _dot(tb, _bf(x)) for t, tb, x in zip(ts, tbs, xs)]
    tbs = [_bf(t) for t in ts]
    egcs = [jnp.exp(c[3]) for c in items]
    us = [_unstack4(_dot(tb, _bf(_stack4(c[2] * c[4]) * bd))) for tb, c in zip(tbs, items)]
    ws = [_unstack4(_dot(tb, _bf(_stack4(c[1] * c[4] * e) * bd))) for tb, c, e in zip(tbs, items, egcs)]
    attns = [_bf(_dot_nt(_bf(_stack4(c[0]) * bd), x) * dc) for c, x, dc in zip(items, ks, decays)]
    out = []
    for (q, k, v, gc, be, _, _, fwd), u, w, attn, egc in zip(items, us, ws, attns, egcs):
        total = gc[CHUNK - 1:CHUNK, :] if fwd else gc[0:1, :]
        out.append((u, _bf(w), attn, _bf(q * egc), _bf(k * jnp.exp(total - gc)), jnp.exp(total)))
    return out


def _dn_chunk_state(local, s_prev, bd):
    u, w, attn, q_dec, k_dec, g_last = local
    sb = _bf(s_prev)
    v_new = u - _dot(w, sb)
    o = _dot(q_dec, sb) + _unstack4(_dot(attn, _bf(_stack4(v_new) * bd)))
    s_new = s_prev * g_last + _dot_tn(k_dec, _bf(v_new)) * bd
    return o, s_new


def _dn_scan_kernel(qf_ref, kf_ref, vf_ref, gcf_ref, bef_ref, qb_ref, kb_ref, vb_ref, gcb_ref, beb_ref,
                    bd_ref, li_ref, ls_ref, ui_ref, us_ref, it_ref, eye_ref, lvl_ref,
                    of_ref, ob_ref, s_ref):
    @pl.when(pl.program_id(1) == 0)
    def _():
        s_ref[...] = jnp.zeros_like(s_ref)

    bd, it, eye = bd_ref[...], it_ref[...], eye_ref[...]
    n_sub = qf_ref.shape[0] // CHUNK
    dirs = ((qf_ref, kf_ref, vf_ref, gcf_ref, bef_ref, li_ref, ls_ref, of_ref, True),
            (qb_ref, kb_ref, vb_ref, gcb_ref, beb_ref, ui_ref, us_ref, ob_ref, False))
    items = []
    for q_ref, k_ref, v_ref, gc_ref, be_ref, ti_ref, ts_ref, _, fwd in dirs:
        for g in range(n_sub):
            r = slice(g * CHUNK, (g + 1) * CHUNK)
            items.append((q_ref[r, :], k_ref[r, :], v_ref[r, :], gc_ref[r, :], be_ref[r, :],
                          ti_ref[...], ts_ref[...], fwd))
    local = _dn_chunks_local(items, bd, it, eye, lvl_ref)
    s_f, s_b = s_ref[0], s_ref[1]
    for i in range(n_sub):
        gb = n_sub - 1 - i
        o, s_f = _dn_chunk_state(local[i], s_f, bd)
        of_ref[i * CHUNK:(i + 1) * CHUNK, :] = o
        o, s_b = _dn_chunk_state(local[n_sub + gb], s_b, bd)
        ob_ref[gb * CHUNK:(gb + 1) * CHUNK, :] = o
    s_ref[0] = s_f
    s_ref[1] = s_b


def _deltanet(p_dn, conv_w, a_log, dt_bias, *, tm, tc):
    bsz, t_all, _ = p_dn.shape
    nt, nct = t_all // tm, tc // tm
    al = jnp.zeros((1, 128), F32).at[0, 0:8].set(a_log.reshape(8).astype(F32))
    dtb = jnp.zeros((1, 128), F32).at[0, 0:8].set(dt_bias.reshape(8).astype(F32))
    e = np.zeros((128, 4 * BRANCH_W), np.float32)
    for jj in range(4):
        for h in range(N_HEADS):
            e[4 * jj + h, jj * BRANCH_W + h * HEAD_DIM: jj * BRANCH_W + (h + 1) * HEAD_DIM] = 1.0
    consts = [jnp.asarray(e, BF16),
              jnp.asarray(_np_tri(tm, CHUNK, True, False), BF16),
              jnp.asarray(_np_tri(tm, CHUNK, False, False), BF16),
              jnp.asarray(_np_block_diag_ones(BRANCH_W, HEAD_DIM), BF16)]
    tok = lambda b, j: (b, j, 0)
    c2 = lambda b, j: (0, 0)
    hb = tm // 8
    q, k, v, gb = pl.pallas_call(
        functools.partial(_dn_prep_kernel, tm=tm, nct=nct, nt=nt),
        grid=(bsz, nt),
        in_specs=[pl.BlockSpec((None, tm, p_dn.shape[2]), tok),
                  pl.BlockSpec((None, 8, 3 * BRANCH_W), lambda b, j: (b, jnp.maximum(j * hb - 1, 0), 0)),
                  pl.BlockSpec((None, 8, 3 * BRANCH_W), lambda b, j: (b, jnp.minimum((j + 1) * hb, nt * hb - 1), 0)),
                  pl.BlockSpec(conv_w.shape, c2), pl.BlockSpec((1, 128), c2), pl.BlockSpec((1, 128), c2)]
                 + [pl.BlockSpec(c.shape, c2) for c in consts],
        out_specs=[pl.BlockSpec((None, tm, BRANCH_W), tok)] * 3 + [pl.BlockSpec((None, tm, 4 * BRANCH_W), tok)],
        out_shape=[jax.ShapeDtypeStruct((bsz, t_all, BRANCH_W), F32)] * 3
                  + [jax.ShapeDtypeStruct((bsz, t_all, 4 * BRANCH_W), F32)],
        compiler_params=pltpu.CompilerParams(dimension_semantics=("parallel", "parallel"),
                                             vmem_limit_bytes=40 * 1024 * 1024),
        name="deltanet_prep",
    )(p_dn, p_dn, p_dn, conv_w.astype(F32), al, dtb, *consts)

    nc, ncc = nt, nct
    n4 = N_HEADS * CHUNK
    masks = [jnp.asarray(_np_block_diag_ones(n4, CHUNK), F32),
             jnp.asarray(_np_tri(n4, CHUNK, True, False), F32),
             jnp.asarray(_np_tri(n4, CHUNK, True, True), F32),
             jnp.asarray(_np_tri(n4, CHUNK, False, False), F32),
             jnp.asarray(_np_tri(n4, CHUNK, False, True), F32),
             jnp.asarray(np.tile(np.eye(CHUNK, dtype=np.float32), (1, N_HEADS)), F32),
             jnp.asarray(np.eye(n4, dtype=np.float32), F32),
             jnp.asarray(_np_doubling_masks(n4, CHUNK), F32)]
    cmap = lambda m: (lambda b, s: (0,) * m.ndim)
    blk = (None, tm, BRANCH_W)
    fmap = lambda col: (lambda b, s: (b, s, col))
    bmap = lambda col: (lambda b, s: (b, _bwd_order(s, ncc, nc), col))
    o_f, o_b = pl.pallas_call(
        _dn_scan_kernel,
        grid=(bsz, nc),
        in_specs=[pl.BlockSpec(blk, fmap(0))] * 3 + [pl.BlockSpec(blk, fmap(0)), pl.BlockSpec(blk, fmap(2))]
                 + [pl.BlockSpec(blk, bmap(0))] * 3 + [pl.BlockSpec(blk, bmap(1)), pl.BlockSpec(blk, bmap(3))]
                 + [pl.BlockSpec(m.shape, cmap(m)) for m in masks],
        out_specs=[pl.BlockSpec(blk, fmap(0)), pl.BlockSpec(blk, bmap(0))],
        out_shape=[jax.ShapeDtypeStruct((bsz, t_all, BRANCH_W), F32)] * 2,
        scratch_shapes=[pltpu.VMEM((2, n4, n4), F32)],
        compiler_params=pltpu.CompilerParams(dimension_semantics=("parallel", "arbitrary"),
                                             vmem_limit_bytes=40 * 1024 * 1024),
        name="deltanet_scan",
    )(q, k, v, gb, gb, q, k, v, gb, gb, *masks)
    return o_f, o_b


def _hg_chunk_intra(q, f, v, lb, tri, bd_ones, o_ref, tmp_ref, fwd):
    qs = _silu(q)
    f_gate = lb + (1.0 - lb) * _sigmoid(f)
    logf = jnp.log(jnp.maximum(f_gate, LOG_FLOOR))
    kk = (1.0 - lb) * _sigmoid(-f)
    b = _dot3_left(tri, logf)
    b_last = b[CHUNK - 1:CHUNK, :] if fwd else b[0:1, :]
    tmp_ref[0], tmp_ref[1], tmp_ref[2] = qs, b - jnp.log(kk), v
    tmp_ref[3] = b
    io8 = lax.broadcasted_iota(jnp.int32, (8, BRANCH_W), 0)
    n_grp = CHUNK // 8
    for g in (range(n_grp) if fwd else range(n_grp - 1, -1, -1)):
        s0 = 8 * g
        lo, hi = (s0, CHUNK) if fwd else (0, s0 + 8)
        nr = hi - lo
        rest = slice(s0 + 8, CHUNK) if fwd else slice(0, s0)
        q_d, b_d = _bf(tmp_ref[0, s0:s0 + 8, :]), tmp_ref[3, s0:s0 + 8, :]
        if nr > 8:
            q_r, b_r = _bf(tmp_ref[0, rest, :]), tmp_ref[3, rest, :]
        rows = []
        for i in range(8):
            c_s = tmp_ref[1, s0 + i:s0 + i + 1, :]
            vis = (io8 >= i) if fwd else (io8 <= i)
            diag = q_d * jnp.exp(_bf(jnp.where(vis, b_d - c_s, NEG_BIG)))
            if nr > 8:
                off = q_r * jnp.exp(_bf(b_r - c_s))
                rows.extend([diag, off] if fwd else [off, diag])
            else:
                rows.append(diag)
        se = _dot(jnp.concatenate(rows, axis=0), bd_ones)
        contrib = se[0:nr] * tmp_ref[2, s0:s0 + 1, :]
        for i in range(1, 8):
            contrib = contrib + se[i * nr:(i + 1) * nr] * tmp_ref[2, s0 + i:s0 + i + 1, :]
        if nr == CHUNK:
            o_ref[...] = contrib
        else:
            o_ref[lo:hi, :] += contrib
    return _bf(qs * jnp.exp(b)), _bf(kk * jnp.exp(b_last - b)), _bf(v), jnp.exp(b_last)


def _hg_chunk_state(carry, st_prev, bd_mask, o_ref):
    q_dec, k_dec, vb, decay = carry
    o_ref[...] += _dot_nt(q_dec, _bf(st_prev))
    return st_prev * decay + _dot_tn(vb, k_dec) * bd_mask


def _hg_scan_kernel(qf_ref, ff_ref, vf_ref, qb_ref, fb_ref, vb_ref, lg_ref, bdo_ref, bdm_ref, tl_ref, tu_ref,
                    of_ref, ob_ref, st_ref, tmp_ref, *, layer):
    @pl.when(pl.program_id(1) == 0)
    def _():
        st_ref[...] = jnp.zeros_like(st_ref)

    n_layers = lg_ref.shape[0]
    mx = lg_ref[0:1, :]
    for i in range(1, n_layers):
        mx = jnp.maximum(mx, lg_ref[i:i + 1, :])
    den = jnp.zeros_like(mx)
    num = jnp.zeros_like(mx)
    for i in range(n_layers):
        e = jnp.exp(lg_ref[i:i + 1, :] - mx)
        den = den + e
        if 1 <= i <= layer:
            num = num + e
    lb = num / den
    bdo, bdm = bdo_ref[...], bdm_ref[...]
    n_sub = qf_ref.shape[0] // CHUNK
    rows = [pl.ds(g * CHUNK, CHUNK) for g in range(n_sub)]
    carries = []
    for d, (q_ref, f_ref, v_ref, tri_ref, o_ref) in enumerate(((qf_ref, ff_ref, vf_ref, tl_ref, of_ref),
                                                               (qb_ref, fb_ref, vb_ref, tu_ref, ob_ref))):
        for g in range(n_sub):
            carries.append(_hg_chunk_intra(q_ref[rows[g], :], f_ref[rows[g], :], v_ref[rows[g], :], lb, tri_ref[...],
                                           bdo, o_ref.at[rows[g]], tmp_ref.at[d], d == 0))
    st_f, st_b = st_ref[0], st_ref[1]
    for i in range(n_sub):
        gb = n_sub - 1 - i
        st_f = _hg_chunk_state(carries[i], st_f, bdm, of_ref.at[rows[i]])
        st_b = _hg_chunk_state(carries[n_sub + gb], st_b, bdm, ob_ref.at[rows[gb]])
    st_ref[0] = st_f
    st_ref[1] = st_b


def _hgrn2(p_hg, lb_logits, layer, *, tm, tc):
    bsz, t_all, _ = p_hg.shape
    nc, ncc = t_all // tm, tc // tm
    consts = [jnp.asarray(_np_block_diag_ones(BRANCH_W, HEAD_DIM), BF16),
              jnp.asarray(_np_block_diag_ones(BRANCH_W, HEAD_DIM), F32),
              jnp.asarray(_np_tri(CHUNK, CHUNK, True, False), BF16),
              jnp.asarray(_np_tri(CHUNK, CHUNK, False, False), BF16)]
    blk = (None, tm, BRANCH_W)
    c2 = lambda b, s: (0, 0)
    fmap = lambda col: (lambda b, s: (b, s, col))
    bmap = lambda col: (lambda b, s: (b, _bwd_order(s, ncc, nc), col))
    return pl.pallas_call(
        functools.partial(_hg_scan_kernel, layer=layer),
        grid=(bsz, nc),
        in_specs=[pl.BlockSpec(blk, fmap(0)), pl.BlockSpec(blk, fmap(1)), pl.BlockSpec(blk, fmap(3)),
                  pl.BlockSpec(blk, bmap(0)), pl.BlockSpec(blk, bmap(2)), pl.BlockSpec(blk, bmap(3)),
                  pl.BlockSpec(lb_logits.shape, c2)] + [pl.BlockSpec(c.shape, c2) for c in consts],
        out_specs=[pl.BlockSpec(blk, fmap(0)), pl.BlockSpec(blk, bmap(0))],
        out_shape=[jax.ShapeDtypeStruct((bsz, t_all, BRANCH_W), F32)] * 2,
        scratch_shapes=[pltpu.VMEM((2, BRANCH_W, BRANCH_W), F32), pltpu.VMEM((2, 4, CHUNK, BRANCH_W), F32)],
        compiler_params=pltpu.CompilerParams(dimension_semantics=("parallel", "arbitrary"),
                                             vmem_limit_bytes=40 * 1024 * 1024),
        name="hgrn2_scan",
    )(p_hg, p_hg, p_hg, p_hg, p_hg, p_hg, lb_logits.astype(F32), *consts)


def _s5_kernel(uf_ref, ub_ref, bb_ref, cc_ref, a_ref, yf_ref, yb_ref, x_ref, h_ref, *, tt, bsz):
    @pl.when(pl.program_id(0) == 0)
    def _():
        h_ref[...] = jnp.zeros_like(h_ref)

    x_ref[0] = _dot(_bf(uf_ref[...]), bb_ref[0])
    x_ref[1] = _dot(_bf(ub_ref[...]), bb_ref[1])

    def step(i, carry):
        hfr, hfi, hbr, hbi = carry
        rf = pl.multiple_of(i * bsz, bsz)
        rb = pl.multiple_of((tt - 1 - i) * bsz, bsz)
        ar, ai = a_ref[0, 0], a_ref[0, 1]
        xr = x_ref[0, pl.ds(rf, bsz), 0:S5_LANES]
        xi = x_ref[0, pl.ds(rf, bsz), S5_LANES:2 * S5_LANES]
        nfr = ar * hfr - ai * hfi + xr
        nfi = ar * hfi + ai * hfr + xi
        x_ref[0, pl.ds(rf, bsz), 0:S5_LANES] = nfr
        x_ref[0, pl.ds(rf, bsz), S5_LANES:2 * S5_LANES] = nfi
        ar, ai = a_ref[1, 0], a_ref[1, 1]
        xr = x_ref[1, pl.ds(rb, bsz), 0:S5_LANES]
        xi = x_ref[1, pl.ds(rb, bsz), S5_LANES:2 * S5_LANES]
        nbr = ar * hbr - ai * hbi + xr
        nbi = ar * hbi + ai * hbr + xi
        x_ref[1, pl.ds(rb, bsz), 0:S5_LANES] = nbr
        x_ref[1, pl.ds(rb, bsz), S5_LANES:2 * S5_LANES] = nbi
        return nfr, nfi, nbr, nbi

    hs = lax.fori_loop(0, tt, step, (h_ref[0, 0], h_ref[0, 1], h_ref[1, 0], h_ref[1, 1]))
    h_ref[0, 0], h_ref[0, 1], h_ref[1, 0], h_ref[1, 1] = hs
    yf_ref[...] = _dot(_bf(x_ref[0]), cc_ref[0])
    yb_ref[...] = _dot(_bf(x_ref[1]), cc_ref[1])


def _s5_matrices(lam_re, lam_im, log_step, b_re, b_im, c_re, c_im):
    lam_re, lam_im = lam_re.astype(F32), lam_im.astype(F32)
    step = jnp.exp(log_step.astype(F32))[..., None]
    mag = jnp.exp(lam_re * step)
    a_re, a_im = mag * jnp.cos(lam_im * step), mag * jnp.sin(lam_im * step)
    den = lam_re * lam_re + lam_im * lam_im
    n_re = a_re - 1.0
    coef_re = (n_re * lam_re + a_im * lam_im) / den
    coef_im = (a_im * lam_re - n_re * lam_im) / den
    b_re, b_im = b_re.astype(F32), b_im.astype(F32)
    bb_re = coef_re[..., None] * b_re - coef_im[..., None] * b_im
    bb_im = coef_re[..., None] * b_im + coef_im[..., None] * b_re
    eye_g = jnp.eye(S5_GROUPS, dtype=F32)
    bb = jnp.stack([bb_re, bb_im], axis=1)
    bb = jnp.einsum('drgph,gk->dghrkp', bb, eye_g).reshape(2, BRANCH_W, 2 * S5_LANES)
    cc = jnp.stack([c_re.astype(F32), -c_im.astype(F32)], axis=0)
    cc = jnp.einsum('rgkp,gj->rjpgk', cc, eye_g).reshape(2 * S5_LANES, BRANCH_W)
    return bb.astype(BF16), cc.astype(BF16), a_re.reshape(2, S5_LANES), a_im.reshape(2, S5_LANES)


def _s5(u_tb, mats, *, bsz, tc, tt):
    bb, cc, a_re, a_im = mats
    n_rows = u_tb.shape[0]
    t_all = n_rows // bsz
    nt, nct = t_all // tt, tc // tt
    a = jnp.broadcast_to(jnp.stack([a_re, a_im], axis=1)[:, :, None, :], (2, 2, bsz, S5_LANES))
    cc2 = jnp.broadcast_to(cc[None], (2,) + cc.shape)
    blk = (tt * bsz, BRANCH_W)
    fmap = lambda s: (s, 0)
    bmap = lambda s: (_bwd_order(s, nct, nt), 0)
    return pl.pallas_call(
        functools.partial(_s5_kernel, tt=tt, bsz=bsz),
        grid=(nt,),
        in_specs=[pl.BlockSpec(blk, fmap), pl.BlockSpec(blk, bmap),
                  pl.BlockSpec(bb.shape, lambda s: (0, 0, 0)), pl.BlockSpec(cc2.shape, lambda s: (0, 0, 0)),
                  pl.BlockSpec(a.shape, lambda s: (0, 0, 0, 0))],
        out_specs=[pl.BlockSpec(blk, fmap), pl.BlockSpec(blk, bmap)],
        out_shape=[jax.ShapeDtypeStruct((n_rows, BRANCH_W), F32)] * 2,
        scratch_shapes=[pltpu.VMEM((2, tt * bsz, 2 * S5_LANES), F32), pltpu.VMEM((2, 2, bsz, S5_LANES), F32)],
        compiler_params=pltpu.CompilerParams(dimension_semantics=("arbitrary",),
                                             vmem_limit_bytes=V7X_VMEM_LIMIT),
        name="s5_scan",
    )(u_tb, u_tb, bb, cc2, a)


def _rope(x, cos, sin_signed):
    w = x.shape[-1]
    lane = lax.broadcasted_iota(jnp.int32, x.shape, 1)
    up = pltpu.roll(x, w - 16, 1)
    dn = pltpu.roll(x, 16, 1)
    return x * cos + jnp.where((lane & 31) < 16, up, dn) * sin_signed


def _attn_kernel(q_ref, kv_ref, cq_ref, sq_ref, ck_ref, sk_ref, qg_ref, kg_ref, bdq_ref, bdk_ref,
                 o_ref, kp_ref, vp_ref, *, tm, tc, nct):
    j = pl.program_id(1)
    t_all = kv_ref.shape[0]
    half = BRANCH_W // 2

    @pl.when(j == 0)
    def _():
        def pairs(x):
            lane = lax.broadcasted_iota(jnp.int32, x.shape, 1)
            sw = pltpu.roll(x, HEAD_DIM, 1)
            return _bf(jnp.where(lane < HEAD_DIM, x, sw)), _bf(jnp.where(lane < HEAD_DIM, sw, x))

        for lo, hi, rope in ((0, tc, False), (tc, t_all, True)):
            k = _head_rms(kv_ref[lo:hi, 0:half], bdk_ref[...], kg_ref[...])
            if rope:
                k = _rope(k, ck_ref[...], sk_ref[...])
            kp_ref[0, lo:hi, :], kp_ref[1, lo:hi, :] = pairs(k)
            v = kv_ref[lo:hi, half:2 * half]
            lane = lax.broadcasted_iota(jnp.int32, v.shape, 1)
            sw = pltpu.roll(v, HEAD_DIM, 1)
            for g in range(2):
                v_lo, v_hi = (v, sw) if g == 0 else (sw, v)
                vp_ref[g, 0, lo:hi, :] = _bf(jnp.where(lane < HEAD_DIM, v_lo, 1.0))
                vp_ref[g, 1, lo:hi, :] = _bf(jnp.where(lane < HEAD_DIM, 1.0, v_hi))

    def attend(q, nk):
        lane = lax.broadcasted_iota(jnp.int32, (tm, half), 1)
        outs = []
        for g in range(2):
            qp = q[:, g * half:(g + 1) * half]
            acc = None
            for hh in range(2):
                sel = (lane < HEAD_DIM) if hh == 0 else (lane >= HEAD_DIM)
                s = _dot_nt(_bf(jnp.where(sel, qp, 0.0)), kp_ref[g, 0:nk, :])
                p = jnp.exp(_bf(s - jnp.max(s, axis=-1, keepdims=True)))
                ol = _dot(p, vp_ref[g, hh, 0:nk, :])
                res = ol / pltpu.roll(ol, HEAD_DIM, 1)
                acc = res if acc is None else jnp.where(sel, res, acc)
            outs.append(acc)
        o_ref[...] = jnp.concatenate(outs, axis=1)

    qn = _head_rms(q_ref[...], bdq_ref[...], qg_ref[...]) * (HEAD_DIM ** -0.5)

    @pl.when(j < nct)
    def _():
        attend(qn, tc)

    @pl.when(j >= nct)
    def _():
        attend(_rope(qn, cq_ref[...], sq_ref[...]), t_all)


def _rope_tables(n_tokens):
    rows = n_tokens // GRID_W
    r, col = np.meshgrid(np.arange(rows), np.arange(GRID_W), indexing='ij')
    axis_dim = HEAD_DIM // 2
    inv = jnp.asarray(ROPE_THETA, F32) ** (-jnp.arange(0, axis_dim, 2, dtype=F32) / axis_dim)
    ang_r = jnp.asarray(r.reshape(-1, 1), F32) * inv
    ang_c = jnp.asarray(col.reshape(-1, 1), F32) * inv
    ang = jnp.concatenate([ang_r, ang_r, ang_c, ang_c], axis=-1)
    sign = np.where((np.arange(HEAD_DIM) % 32) < 16, -1.0, 1.0).astype(np.float32)
    cos, sin = jnp.cos(ang), jnp.sin(ang) * sign
    return jnp.tile(cos, (1, N_HEADS)), jnp.tile(sin, (1, N_HEADS))


def _attention(p_q, p_kv, qn_g, kn_g, cos4, sin4, *, tm, tc):
    bsz, t_all, _ = p_q.shape
    nt, nct = t_all // tm, tc // tm
    half = BRANCH_W // 2
    tok = lambda b, j: (b, j, 0)
    c2 = lambda b, j: (0, 0)
    lat = lambda b, j: (jnp.maximum(j - nct, 0), 0)
    qg = jnp.tile(qn_g.astype(F32), N_HEADS).reshape(1, BRANCH_W)
    kg = jnp.tile(kn_g.astype(F32), 2).reshape(1, half)
    bdq = jnp.asarray(_np_block_diag_ones(BRANCH_W, HEAD_DIM), BF16)
    bdk = jnp.asarray(_np_block_diag_ones(half, HEAD_DIM), BF16)
    cosk, sink = cos4[:, 0:half], sin4[:, 0:half]
    return pl.pallas_call(
        functools.partial(_attn_kernel, tm=tm, tc=tc, nct=nct),
        grid=(bsz, nt),
        in_specs=[pl.BlockSpec((None, tm, BRANCH_W), tok),
                  pl.BlockSpec((None, t_all, BRANCH_W), lambda b, j: (b, 0, 0)),
                  pl.BlockSpec((tm, BRANCH_W), lat), pl.BlockSpec((tm, BRANCH_W), lat),
                  pl.BlockSpec(cosk.shape, c2), pl.BlockSpec(sink.shape, c2),
                  pl.BlockSpec((1, BRANCH_W), c2), pl.BlockSpec((1, half), c2),
                  pl.BlockSpec(bdq.shape, c2), pl.BlockSpec(bdk.shape, c2)],
        out_specs=pl.BlockSpec((None, tm, BRANCH_W), tok),
        out_shape=jax.ShapeDtypeStruct((bsz, t_all, BRANCH_W), F32),
        scratch_shapes=[pltpu.VMEM((2, t_all, half), BF16), pltpu.VMEM((2, 2, t_all, half), BF16)],
        compiler_params=pltpu.CompilerParams(dimension_semantics=("parallel", "arbitrary"),
                                             vmem_limit_bytes=V7X_VMEM_LIMIT),
        name="attention",
    )(p_q, p_kv, cos4, sin4, cosk, sink, qg, kg, bdq, bdk)


def _merge_kernel(x_ref, m_ref, g_ref, dnf_ref, dnb_ref, z_ref, sf_ref, sb_ref, u_ref, hf_ref, hb_ref, hgate_ref,
                  at_ref, dng_ref, hgg_ref, dsk_ref, glu_ref, bd_ref, wg_ref, wb_ref, wo_ref, o_ref):
    x = x_ref[...]
    bd = bd_ref[...]
    h = _bf(_rms(x) * g_ref[...] * (1.0 + m_ref[4:5, :]) + m_ref[3:4, :])
    y_dn = _head_rms(dnf_ref[...] + dnb_ref[...], bd, dng_ref[...]) * _silu(z_ref[...])
    ys = sf_ref[...] + sb_ref[...] + dsk_ref[...] * u_ref[...]
    ab = _dot(_bf(jax.nn.gelu(ys, approximate=True)), glu_ref[...])
    y_s5 = ab[:, 0:BRANCH_W] * _sigmoid(ab[:, BRANCH_W:2 * BRANCH_W])
    y_hg = _head_rms(hf_ref[...] + hb_ref[...], bd, hgg_ref[...]) * _sigmoid(hgate_ref[...])
    ys_all = (y_dn, y_s5, y_hg, at_ref[...])
    acc = jnp.zeros(x.shape, F32)
    for i in range(4):
        acc = acc + _sigmoid(_dot(h, wg_ref[i])) * _dot(_bf(ys_all[i]), wb_ref[i])
    o_ref[...] = x + m_ref[5:6, :] * _dot(_bf(acc), wo_ref[...])


def _merge(xj, mods, g, br, params, *, tm, nct, j0):
    bsz, t_all, _ = xj.shape
    n_tiles = t_all // tm - j0
    tok = lambda b, j: (b, j + j0, 0)
    col = lambda c: (lambda b, j: (b, j + j0, c))
    mod_map = lambda b, j: (jnp.where(j + j0 < nct, bsz, b), 0, 0)
    c2 = lambda b, j: (0, 0)
    c3 = lambda b, j: (0, 0, 0)
    tb = (None, tm, BRANCH_W)
    dn_g, hg_g, d_skip, w_glu, wg, wb, wo = params
    bd = jnp.asarray(_np_block_diag_ones(BRANCH_W, HEAD_DIM), BF16)
    return pl.pallas_call(
        _merge_kernel,
        grid=(bsz, n_tiles),
        in_specs=[pl.BlockSpec((None, tm, D_MODEL), tok),
                  pl.BlockSpec((None, N_MOD, D_MODEL), mod_map),
                  pl.BlockSpec((1, D_MODEL), c2),
                  pl.BlockSpec(tb, tok), pl.BlockSpec(tb, tok), pl.BlockSpec(tb, col(3)),
                  pl.BlockSpec(tb, tok), pl.BlockSpec(tb, tok), pl.BlockSpec(tb, tok),
                  pl.BlockSpec(tb, tok), pl.BlockSpec(tb, tok), pl.BlockSpec(tb, col(4)),
                  pl.BlockSpec(tb, tok),
                  pl.BlockSpec((1, BRANCH_W), c2), pl.BlockSpec((1, BRANCH_W), c2), pl.BlockSpec((1, BRANCH_W), c2),
                  pl.BlockSpec(w_glu.shape, c2), pl.BlockSpec(bd.shape, c2),
                  pl.BlockSpec(wg.shape, c3), pl.BlockSpec(wb.shape, c3), pl.BlockSpec(wo.shape, c2)],
        out_specs=pl.BlockSpec((None, tm, D_MODEL), lambda b, j: (b, j, 0)),
        out_shape=jax.ShapeDtypeStruct((bsz, n_tiles * tm, D_MODEL), F32),
        compiler_params=pltpu.CompilerParams(dimension_semantics=("parallel", "parallel"),
                                             vmem_limit_bytes=V7X_VMEM_LIMIT),
        name="branch_merge",
    )(xj, mods, g.reshape(1, D_MODEL), br['dn_f'], br['dn_b'], br['p_dn'], br['s5_f'], br['s5_b'], br['p_s5'],
      br['hg_f'], br['hg_b'], br['p_hg'], br['at'], dn_g, hg_g, d_skip, w_glu, bd, wg, wb, wo)


def kernel(x, c, ctx, c_ctx, ada_w, ada_b, norm_g, ffn_w1, ffn_w3, ffn_w2, w_in, dn_conv, dn_a_log, dn_dt_bias,
           dn_norm_g, s5_lam_re, s5_lam_im, s5_log_step, s5_b_re, s5_b_im, s5_c_re, s5_c_im, s5_d, s5_glu,
           hg_lb_logits, hg_norm_g, at_qn_g, at_kn_g, w_branch, w_out, final_g):
    bsz, tl, _ = x.shape
    tc = ctx.shape[1]
    t_all = tc + tl
    depth = ada_w.shape[0]
    tm = 256 if (tc % 256 == 0 and tl % 256 == 0) else 128
    assert tc % tm == 0 and tl % tm == 0 and tm % CHUNK == 0 and bsz % 8 == 0
    nct = tc // tm
    d_ff = ffn_w1.shape[-1]
    n_ff = d_ff // FF_CHUNK

    n_rows = -(-(bsz + 1) // 8) * 8
    cond = jnp.zeros((n_rows, D_MODEL), F32).at[0:bsz].set(c.astype(F32)).at[bsz].set(c_ctx.astype(F32))
    mods_all = _ada_mods(cond, ada_w.astype(F32), ada_b.astype(F32))
    cos4, sin4 = _rope_tables(tl)
    xj = jnp.concatenate([ctx, x], axis=1).astype(F32)

    def ffn_weights(l, i):
        w1 = ffn_w1[l, i].astype(BF16).reshape(D_MODEL, n_ff, FF_CHUNK).transpose(1, 0, 2)
        w3 = ffn_w3[l, i].astype(BF16).reshape(D_MODEL, n_ff, FF_CHUNK).transpose(1, 0, 2)
        w2 = ffn_w2[l, i].astype(BF16)
        return w1, w3, w2

    for l in range(depth):
        last = l == depth - 1
        mods = mods_all[l]
        xj = _ffn_half_step(xj, mods, norm_g[l, 0], *ffn_weights(l, 0), mod_base=0, tm=tm, nct=nct)

        wl = w_in[l]
        w_dn = jnp.concatenate([wl[:, 0:1040], jnp.zeros((D_MODEL, 112), wl.dtype)], axis=1)
        ws = [w.astype(BF16) for w in (w_dn, wl[:, 1040:1296], wl[:, 1296:2576], wl[:, 2576:2832], wl[:, 2832:3088])]
        p_dn, p_s5, p_hg, p_aq, p_kv = _input_projection(xj, mods, norm_g[l, 1], ws, tm=tm, nct=nct)

        dn_f, dn_b = _deltanet(p_dn, dn_conv[l], dn_a_log[l], dn_dt_bias[l], tm=tm, tc=tc)
        hg_f, hg_b = _hgrn2(p_hg, hg_lb_logits, l, tm=tm, tc=tc)
        mats = _s5_matrices(s5_lam_re[l], s5_lam_im[l], s5_log_step[l], s5_b_re[l], s5_b_im[l], s5_c_re[l], s5_c_im[l])
        u_tb = p_s5.transpose(1, 0, 2).reshape(t_all * bsz, BRANCH_W)
        s5_f, s5_b = _s5(u_tb, mats, bsz=bsz, tc=tc, tt=CHUNK)
        s5_f = s5_f.reshape(t_all, bsz, BRANCH_W).transpose(1, 0, 2)
        s5_b = s5_b.reshape(t_all, bsz, BRANCH_W).transpose(1, 0, 2)
        y_at = _attention(p_aq, p_kv, at_qn_g[l], at_kn_g[l], cos4, sin4, tm=tm, tc=tc)

        br = dict(dn_f=dn_f, dn_b=dn_b, p_dn=p_dn, s5_f=s5_f, s5_b=s5_b, p_s5=p_s5, hg_f=hg_f, hg_b=hg_b,
                  p_hg=p_hg, at=y_at)
        params = (jnp.tile(dn_norm_g[l].astype(F32), N_HEADS).reshape(1, BRANCH_W),
                  jnp.tile(hg_norm_g[l].astype(F32), N_HEADS).reshape(1, BRANCH_W),
                  s5_d[l].astype(F32).reshape(1, BRANCH_W),
                  s5_glu[l].astype(BF16),
                  wl[:, 3088:].astype(BF16).reshape(D_MODEL, 4, D_MODEL).transpose(1, 0, 2),
                  w_branch[l].astype(BF16),
                  w_out[l].astype(BF16))
        xj = _merge(xj, mods, norm_g[l, 1], br, params, tm=tm, nct=nct, j0=nct if last else 0)
        xj = _ffn_half_step(xj, mods, norm_g[l, 2], *ffn_weights(l, 1), mod_base=6, tm=tm,
                            nct=0 if last else nct, final_g=final_g.astype(F32) if last else None)
    return xj.astype(x.dtype)
```

```python
import functools
import math

import numpy as np
import jax
import jax.numpy as jnp
from jax import lax
from jax.experimental import pallas as pl
from jax.experimental.pallas import tpu as pltpu

F32 = jnp.float32
BF16 = jnp.bfloat16

D_MODEL = 1024
N_MOD = 9
BRANCH_W = 256
HEAD_DIM = 64
N_HEADS = 4
CHUNK = 64
HG_BLK = 16
EPS = 1e-6
LOG_FLOOR = 1e-30
GRID_W = 64
ROPE_THETA = 10000.0
FF_CHUNK = 256
S5_GROUPS = 16
S5_GROUP = 16
S5_STATE = 64
S5_LANES = S5_GROUPS * S5_STATE
NEG_BIG = -1e30
V7X_VMEM_LIMIT = 56 * 1024 * 1024


def _bf(x):
    return x.astype(BF16)


def _dot(a, b):
    return jnp.dot(a, b, preferred_element_type=F32)


def _dot_nt(a, b):
    return lax.dot_general(a, b, (((1,), (1,)), ((), ())), preferred_element_type=F32)


def _dot_tn(a, b):
    return lax.dot_general(a, b, (((0,), (0,)), ((), ())), preferred_element_type=F32)


def _split2(x):
    hi = _bf(x)
    lo = _bf(x - hi.astype(F32))
    return hi, lo


def _split3(x):
    x1 = _bf(x)
    r1 = x - x1.astype(F32)
    x2 = _bf(r1)
    x3 = _bf(r1 - x2.astype(F32))
    return x1, x2, x3


def _dot_x2(x, m):
    hi, lo = _split2(x)
    return _dot(hi, m) + _dot(lo, m)


def _dot_x3(x, m):
    x1, x2, x3 = _split3(x)
    return _dot(x1, m) + _dot(x2, m) + _dot(x3, m)


def _dot3_left(m, x):
    x1, x2, x3 = _split3(x)
    return _dot(m, x1) + _dot(m, x2) + _dot(m, x3)


def _dot_hp(a, b):
    a_hi, a_lo = _split2(a)
    b_hi, b_lo = _split2(b)
    return _dot(a_hi, b_hi) + _dot(a_hi, b_lo) + _dot(a_lo, b_hi)


def _sigmoid(x):
    return jax.nn.sigmoid(x)


def _silu(x):
    return x * jax.nn.sigmoid(x)


def _rms(x):
    return x * lax.rsqrt(jnp.mean(x * x, axis=-1, keepdims=True) + EPS)


def _head_rms(x, bd_ones, gain):
    ss = _dot_x2(x * x, bd_ones) * (1.0 / HEAD_DIM)
    return x * lax.rsqrt(ss + EPS) * gain


def _stack4(x):
    return jnp.concatenate([x, x, x, x], axis=0)


def _unstack4(x):
    n = x.shape[0] // 4
    return x[0:n] + x[n:2 * n] + x[2 * n:3 * n] + x[3 * n:4 * n]


def _np_block_diag_ones(n, blk):
    i = np.arange(n)
    return (i[:, None] // blk == i[None, :] // blk).astype(np.float32)


def _np_tri(n, blk, lower, strict):
    i = np.arange(n)
    same = i[:, None] // blk == i[None, :] // blk
    a, b = i[:, None] % blk, i[None, :] % blk
    if lower:
        t = (a > b) if strict else (a >= b)
    else:
        t = (a < b) if strict else (a <= b)
    return (same & t).astype(np.float32)


def _np_doubling_masks(n, blk):
    i = np.arange(n)
    r, c = i[:, None], i[None, :]
    out = [(r // 2 == c // 2)]
    s = 2
    while s < blk:
        out.append((r // (2 * s) == c // (2 * s)) & (r // s != c // s))
        s *= 2
    return np.stack(out).astype(np.float32)


def _bwd_order(s, n_ctx, n_all):
    return jnp.where(s < n_ctx, n_ctx - 1 - s, n_all + n_ctx - 1 - s)


def _ada_kernel(c_ref, w_ref, b_ref, o_ref):
    s = _silu(c_ref[...])
    w = w_ref[...]
    s_hi, s_lo = _split2(s)
    w_hi, w_lo = _split2(w)
    o_ref[...] = _dot(s_hi, w_hi) + _dot(s_hi, w_lo) + _dot(s_lo, w_hi) + b_ref[...]


def _ada_mods(cond, ada_w, ada_b):
    n_layers = ada_w.shape[0]
    r = cond.shape[0]
    n_out = ada_w.shape[2]
    tn = 1152
    out = pl.pallas_call(
        _ada_kernel,
        grid=(n_layers, n_out // tn),
        in_specs=[
            pl.BlockSpec((r, D_MODEL), lambda l, n: (0, 0)),
            pl.BlockSpec((None, D_MODEL, tn), lambda l, n: (l, 0, n)),
            pl.BlockSpec((None, 1, tn), lambda l, n: (l, 0, n)),
        ],
        out_specs=pl.BlockSpec((None, r, tn), lambda l, n: (l, 0, n)),
        out_shape=jax.ShapeDtypeStruct((n_layers, r, n_out), F32),
        compiler_params=pltpu.CompilerParams(dimension_semantics=("parallel", "parallel"),
                                             vmem_limit_bytes=40 * 1024 * 1024),
        name="ada_mods",
    )(cond, ada_w, ada_b.reshape(n_layers, 1, n_out))
    return out.reshape(n_layers, r, N_MOD, D_MODEL)


def _ffn_kernel(*refs, mod_base, final):
    if final:
        x_ref, m_ref, g_ref, w1_ref, w3_ref, w2_ref, fg_ref, o_ref = refs
    else:
        x_ref, m_ref, g_ref, w1_ref, w3_ref, w2_ref, o_ref = refs
    x = x_ref[...]
    shift = m_ref[mod_base:mod_base + 1, :]
    scale = m_ref[mod_base + 1:mod_base + 2, :]
    gate = m_ref[mod_base + 2:mod_base + 3, :]
    h = _bf(_rms(x) * g_ref[...] * (1.0 + scale) + shift)
    us = []
    for c in range(w1_ref.shape[0]):
        a = _dot(h, w1_ref[c])
        us.append(_bf(_silu(a) * _dot(h, w3_ref[c])))
    out = x + 0.5 * gate * _dot(jnp.concatenate(us, axis=1), w2_ref[...])
    if final:
        out = _rms(out) * fg_ref[...]
    o_ref[...] = out


def _ffn_half_step(xj, mods, g, w1c, w3c, w2c, *, mod_base, tm, nct, final_g=None):
    bsz, t_all, _ = xj.shape
    final = final_g is not None

    def tok_map(b, j):
        return (b, j, 0)

    def mod_map(b, j):
        return (jnp.where(j < nct, bsz, b), 0, 0)

    const3 = lambda b, j: (0, 0, 0)
    const2 = lambda b, j: (0, 0)
    in_specs = [
        pl.BlockSpec((None, tm, D_MODEL), tok_map),
        pl.BlockSpec((None, N_MOD, D_MODEL), mod_map),
        pl.BlockSpec((1, D_MODEL), const2),
        pl.BlockSpec(w1c.shape, const3),
        pl.BlockSpec(w3c.shape, const3),
        pl.BlockSpec(w2c.shape, const2),
    ]
    args = [xj, mods, g.reshape(1, D_MODEL), w1c, w3c, w2c]
    if final:
        in_specs.append(pl.BlockSpec((1, D_MODEL), const2))
        args.append(final_g.reshape(1, D_MODEL))
    return pl.pallas_call(
        functools.partial(_ffn_kernel, mod_base=mod_base, final=final),
        grid=(bsz, t_all // tm),
        in_specs=in_specs,
        out_specs=pl.BlockSpec((None, tm, D_MODEL), tok_map),
        out_shape=jax.ShapeDtypeStruct(xj.shape, F32),
        compiler_params=pltpu.CompilerParams(dimension_semantics=("parallel", "parallel"),
                                             vmem_limit_bytes=V7X_VMEM_LIMIT),
        name="ffn_half_step",
    )(*args)


def _inproj_kernel(x_ref, m_ref, g_ref, wdn_ref, ws5_ref, whg_ref, waq_ref, wkv_ref,
                   dn_ref, s5_ref, hg_ref, aq_ref, kv_ref):
    x = x_ref[...]
    h = _bf(_rms(x) * g_ref[...] * (1.0 + m_ref[4:5, :]) + m_ref[3:4, :])
    dn_ref[...] = _dot(h, wdn_ref[...])
    s5_ref[...] = _dot(h, ws5_ref[...])
    hg_ref[...] = _dot(h, whg_ref[...])
    aq_ref[...] = _dot(h, waq_ref[...])
    kv_ref[...] = _dot(h, wkv_ref[...])


def _input_projection(xj, mods, g, ws, *, tm, nct):
    bsz, t_all, _ = xj.shape
    tok_map = lambda b, j: (b, j, 0)
    mod_map = lambda b, j: (jnp.where(j < nct, bsz, b), 0, 0)
    const2 = lambda b, j: (0, 0)
    widths = [w.shape[1] for w in ws]
    return pl.pallas_call(
        _inproj_kernel,
        grid=(bsz, t_all // tm),
        in_specs=[pl.BlockSpec((None, tm, D_MODEL), tok_map),
                  pl.BlockSpec((None, N_MOD, D_MODEL), mod_map),
                  pl.BlockSpec((1, D_MODEL), const2)]
                 + [pl.BlockSpec(w.shape, const2) for w in ws],
        out_specs=[pl.BlockSpec((None, tm, n), tok_map) for n in widths],
        out_shape=[jax.ShapeDtypeStruct((bsz, t_all, n), F32) for n in widths],
        compiler_params=pltpu.CompilerParams(dimension_semantics=("parallel", "parallel"),
                                             vmem_limit_bytes=V7X_VMEM_LIMIT),
        name="input_projection",
    )(xj, mods, g.reshape(1, D_MODEL), *ws)


def _dn_prep_kernel(p_ref, prev_ref, next_ref, cw_ref, al_ref, dtb_ref, e_ref, tril_ref, triu_ref, bd_ref,
                    q_ref, k_ref, v_ref, gb_ref, *, tm, nct, nt):
    j = pl.program_id(1)
    first = jnp.logical_or(j == 0, j == nct)
    last = jnp.logical_or(j == nct - 1, j == nt - 1)
    x = p_ref[:, 0:3 * BRANCH_W]
    pv = prev_ref[...] * jnp.where(first, 0.0, 1.0)
    nx = next_ref[...] * jnp.where(last, 0.0, 1.0)
    ext = jnp.concatenate([pv, x, nx], axis=0)
    n = tm + 16
    acc = jnp.zeros((tm, 3 * BRANCH_W), F32)
    for t in range(5):
        sh = (2 - t) % n
        r = ext if sh == 0 else pltpu.roll(ext, sh, 0)
        acc = acc + r[8:8 + tm] * cw_ref[t:t + 1, :]
    h = _silu(acc)
    bd = bd_ref[...]

    def l2n(t):
        return t * lax.rsqrt(_dot_x2(t * t, bd) + EPS)

    q_ref[...] = l2n(h[:, 0:BRANCH_W]) * (HEAD_DIM ** -0.5)
    k_ref[...] = l2n(h[:, BRANCH_W:2 * BRANCH_W])
    v_ref[...] = h[:, 2 * BRANCH_W:3 * BRANCH_W]

    ab = p_ref[:, 4 * BRANCH_W:4 * BRANCH_W + 128]
    z = ab + dtb_ref[...]
    softplus = jnp.maximum(z, 0.0) + jnp.log1p(jnp.exp(-jnp.abs(z)))
    g = -jnp.exp(al_ref[...]) * softplus
    lane = lax.broadcasted_iota(jnp.int32, ab.shape, 1)
    g = jnp.where(lane < 8, g, 0.0)
    cs_f = _dot3_left(tril_ref[...], g)
    cs_b = _dot3_left(triu_ref[...], g)
    comp = jnp.where(lane < 4, cs_f, jnp.where(lane < 8, cs_b, jnp.where(lane < 16, _sigmoid(ab), 0.0)))
    gb_ref[...] = _dot_x3(comp, e_ref[...])


def _dn_chunks_local(items, bd, it, eye, lvl_ref):
    ks = [_bf(_stack4(c[1]) * bd) for c in items]
    grams = [_dot_nt(x, x) for x in ks]
    decays, a_s = [], []
    for (q, k, v, gc, be, tri_incl, tri_strict, fwd), gram in zip(items, grams):
        col_gc = jnp.sum(gc * it, axis=0, keepdims=True)
        decay = jnp.exp(jnp.where(tri_incl > 0.5, _stack4(gc) - col_gc, NEG_BIG))
        decays.append(decay)
        a_s.append(gram * decay * _stack4(be) * tri_strict)
    ts = [eye - a * lvl_ref[0] for a in a_s]
    for lv in range(1, lvl_ref.shape[0]):
        tbs = [_bf(t) for t in ts]
        xs = [_dot(_bf(a * lvl_ref[lv]), tb) for a, tb in zip(a_s, tbs)]
        ts = [t - _dot(tb, _bf(x)) for t, tb, x in zip(ts, tbs, xs)]
    tbs = [_bf(t) for t in ts]
    egcs = [jnp.exp(c[3]) for c in items]
    us = [_unstack4(_dot(tb, _bf(_stack4(c[2] * c[4]) * bd))) for tb, c in zip(tbs, items)]
    ws = [_unstack4(_dot(tb, _bf(_stack4(c[1] * c[4] * e) * bd))) for tb, c, e in zip(tbs, items, egcs)]
    attns = [_bf(_dot_nt(_bf(_stack4(c[0]) * bd), x) * dc) for c, x, dc in zip(items, ks, decays)]
    out = []
    for (q, k, v, gc, be, _, _, fwd), u, w, attn, egc in zip(items, us, ws, attns, egcs):
        total = gc[CHUNK - 1:CHUNK, :] if fwd else gc[0:1, :]
        out.append((u, _bf(w), attn, _bf(q * egc), _bf(k * jnp.exp(total - gc)), jnp.exp(total)))
    return out


def _dn_chunk_state(local, s_prev, bd):
    u, w, attn, q_dec, k_dec, g_last = local
    sb = _bf(s_prev)
    v_new = u - _dot(w, sb)
    o = _dot(q_dec, sb) + _unstack4(_dot(attn, _bf(_stack4(v_new) * bd)))
    s_new = s_prev * g_last + _dot_tn(k_dec, _bf(v_new)) * bd
    return o, s_new


def _dn_scan_kernel(qf_ref, kf_ref, vf_ref, gcf_ref, bef_ref, qb_ref, kb_ref, vb_ref, gcb_ref, beb_ref,
                    bd_ref, li_ref, ls_ref, ui_ref, us_ref, it_ref, eye_ref, lvl_ref,
                    of_ref, ob_ref, s_ref):
    @pl.when(pl.program_id(1) == 0)
    def _():
        s_ref[...] = jnp.zeros_like(s_ref)

    bd, it, eye = bd_ref[...], it_ref[...], eye_ref[...]
    n_sub = qf_ref.shape[0] // CHUNK
    dirs = ((qf_ref, kf_ref, vf_ref, gcf_ref, bef_ref, li_ref, ls_ref, of_ref, True),
            (qb_ref, kb_ref, vb_ref, gcb_ref, beb_ref, ui_ref, us_ref, ob_ref, False))
    items = []
    for q_ref, k_ref, v_ref, gc_ref, be_ref, ti_ref, ts_ref, _, fwd in dirs:
        for g in range(n_sub):
            r = slice(g * CHUNK, (g + 1) * CHUNK)
            items.append((q_ref[r, :], k_ref[r, :], v_ref[r, :], gc_ref[r, :], be_ref[r, :],
                          ti_ref[...], ts_ref[...], fwd))
    local = _dn_chunks_local(items, bd, it, eye, lvl_ref)
    s_f, s_b = s_ref[0], s_ref[1]
    for i in range(n_sub):
        gb = n_sub - 1 - i
        o, s_f = _dn_chunk_state(local[i], s_f, bd)
        of_ref[i * CHUNK:(i + 1) * CHUNK, :] = o
        o, s_b = _dn_chunk_state(local[n_sub + gb], s_b, bd)
        ob_ref[gb * CHUNK:(gb + 1) * CHUNK, :] = o
    s_ref[0] = s_f
    s_ref[1] = s_b


def _deltanet(p_dn, conv_w, a_log, dt_bias, *, tm, tc):
    bsz, t_all, _ = p_dn.shape
    nt, nct = t_all // tm, tc // tm
    al = jnp.zeros((1, 128), F32).at[0, 0:8].set(a_log.reshape(8).astype(F32))
    dtb = jnp.zeros((1, 128), F32).at[0, 0:8].set(dt_bias.reshape(8).astype(F32))
    e = np.zeros((128, 4 * BRANCH_W), np.float32)
    for jj in range(4):
        for h in range(N_HEADS):
            e[4 * jj + h, jj * BRANCH_W + h * HEAD_DIM: jj * BRANCH_W + (h + 1) * HEAD_DIM] = 1.0
    consts = [jnp.asarray(e, BF16),
              jnp.asarray(_np_tri(tm, CHUNK, True, False), BF16),
              jnp.asarray(_np_tri(tm, CHUNK, False, False), BF16),
              jnp.asarray(_np_block_diag_ones(BRANCH_W, HEAD_DIM), BF16)]
    tok = lambda b, j: (b, j, 0)
    c2 = lambda b, j: (0, 0)
    hb = tm // 8
    q, k, v, gb = pl.pallas_call(
        functools.partial(_dn_prep_kernel, tm=tm, nct=nct, nt=nt),
        grid=(bsz, nt),
        in_specs=[pl.BlockSpec((None, tm, p_dn.shape[2]), tok),
                  pl.BlockSpec((None, 8, 3 * BRANCH_W), lambda b, j: (b, jnp.maximum(j * hb - 1, 0), 0)),
                  pl.BlockSpec((None, 8, 3 * BRANCH_W), lambda b, j: (b, jnp.minimum((j + 1) * hb, nt * hb - 1), 0)),
                  pl.BlockSpec(conv_w.shape, c2), pl.BlockSpec((1, 128), c2), pl.BlockSpec((1, 128), c2)]
                 + [pl.BlockSpec(c.shape, c2) for c in consts],
        out_specs=[pl.BlockSpec((None, tm, BRANCH_W), tok)] * 3 + [pl.BlockSpec((None, tm, 4 * BRANCH_W), tok)],
        out_shape=[jax.ShapeDtypeStruct((bsz, t_all, BRANCH_W), F32)] * 3
                  + [jax.ShapeDtypeStruct((bsz, t_all, 4 * BRANCH_W), F32)],
        compiler_params=pltpu.CompilerParams(dimension_semantics=("parallel", "parallel"),
                                             vmem_limit_bytes=40 * 1024 * 1024),
        name="deltanet_prep",
    )(p_dn, p_dn, p_dn, conv_w.astype(F32), al, dtb, *consts)

    nc, ncc = nt, nct
    n4 = N_HEADS * CHUNK
    masks = [jnp.asarray(_np_block_diag_ones(n4, CHUNK), F32),
             jnp.asarray(_np_tri(n4, CHUNK, True, False), F32),
             jnp.asarray(_np_tri(n4, CHUNK, True, True), F32),
             jnp.asarray(_np_tri(n4, CHUNK, False, False), F32),
             jnp.asarray(_np_tri(n4, CHUNK, False, True), F32),
             jnp.asarray(np.tile(np.eye(CHUNK, dtype=np.float32), (1, N_HEADS)), F32),
             jnp.asarray(np.eye(n4, dtype=np.float32), F32),
             jnp.asarray(_np_doubling_masks(n4, CHUNK), F32)]
    cmap = lambda m: (lambda b, s: (0,) * m.ndim)
    blk = (None, tm, BRANCH_W)
    fmap = lambda col: (lambda b, s: (b, s, col))
    bmap = lambda col: (lambda b, s: (b, _bwd_order(s, ncc, nc), col))
    o_f, o_b = pl.pallas_call(
        _dn_scan_kernel,
        grid=(bsz, nc),
        in_specs=[pl.BlockSpec(blk, fmap(0))] * 3 + [pl.BlockSpec(blk, fmap(0)), pl.BlockSpec(blk, fmap(2))]
                 + [pl.BlockSpec(blk, bmap(0))] * 3 + [pl.BlockSpec(blk, bmap(1)), pl.BlockSpec(blk, bmap(3))]
                 + [pl.BlockSpec(m.shape, cmap(m)) for m in masks],
        out_specs=[pl.BlockSpec(blk, fmap(0)), pl.BlockSpec(blk, bmap(0))],
        out_shape=[jax.ShapeDtypeStruct((bsz, t_all, BRANCH_W), F32)] * 2,
        scratch_shapes=[pltpu.VMEM((2, n4, n4), F32)],
        compiler_params=pltpu.CompilerParams(dimension_semantics=("parallel", "arbitrary"),
                                             vmem_limit_bytes=40 * 1024 * 1024),
        name="deltanet_scan",
    )(q, k, v, gb, gb, q, k, v, gb, gb, *masks)
    return o_f, o_b


def _hg_tile_prepare(q, f, v, lb, tri_incl, tri_excl, tmp_ref):
    qs = _silu(q)
    logf = jnp.log(jnp.maximum(lb + (1.0 - lb) * _sigmoid(f), LOG_FLOOR))
    kk = (1.0 - lb) * _sigmoid(-f)
    bl = _dot3_left(tri_incl, logf)
    suf = _dot3_left(tri_excl, logf)
    tmp_ref[0], tmp_ref[1], tmp_ref[2], tmp_ref[3] = qs, bl - jnp.log(kk), v, bl
    return _bf(qs * jnp.exp(bl)), _bf(kk * jnp.exp(suf)), _bf(v), bl


def _hg_block_pairs(tmp_ref, r0, bd_ones, fwd):
    io = lax.broadcasted_iota(jnp.int32, (HG_BLK, BRANCH_W), 0)
    q_b = _bf(tmp_ref[0, r0:r0 + HG_BLK, :])
    b_b = tmp_ref[3, r0:r0 + HG_BLK, :]
    rows = []
    for i in range(HG_BLK):
        c_s = jnp.broadcast_to(tmp_ref[1, r0 + i:r0 + i + 1, :], (HG_BLK, BRANCH_W))
        vis = (io >= i) if fwd else (io <= i)
        rows.append(q_b * jnp.exp(_bf(jnp.where(vis, b_b - c_s, NEG_BIG))))
    se = _dot(jnp.concatenate(rows, axis=0), bd_ones)
    out = None
    for i in range(HG_BLK):
        term = se[i * HG_BLK:(i + 1) * HG_BLK] * jnp.broadcast_to(tmp_ref[2, r0 + i:r0 + i + 1, :], (HG_BLK, BRANCH_W))
        out = term if out is None else out + term
    return out


def _hg_scan_kernel(qf_ref, ff_ref, vf_ref, qb_ref, fb_ref, vb_ref, lg_ref, bdo_ref, bdm_ref,
                    tli_ref, tls_ref, tui_ref, tus_ref, of_ref, ob_ref, st_ref, tmp_ref, *, layer):
    @pl.when(pl.program_id(1) == 0)
    def _():
        st_ref[...] = jnp.zeros_like(st_ref)

    n_layers = lg_ref.shape[0]
    mx = lg_ref[0:1, :]
    for i in range(1, n_layers):
        mx = jnp.maximum(mx, lg_ref[i:i + 1, :])
    den = jnp.zeros_like(mx)
    num = jnp.zeros_like(mx)
    for i in range(n_layers):
        e = jnp.exp(lg_ref[i:i + 1, :] - mx)
        den = den + e
        if 1 <= i <= layer:
            num = num + e
    lb = num / den
    bdo, bdm = bdo_ref[...], bdm_ref[...]
    n_blk = qf_ref.shape[0] // HG_BLK
    prep = (_hg_tile_prepare(qf_ref[...], ff_ref[...], vf_ref[...], lb, tli_ref[...], tus_ref[...], tmp_ref.at[0]),
            _hg_tile_prepare(qb_ref[...], fb_ref[...], vb_ref[...], lb, tui_ref[...], tls_ref[...], tmp_ref.at[1]))
    pairs = [[_hg_block_pairs(tmp_ref.at[d], i * HG_BLK, bdo, d == 0) for i in range(n_blk)] for d in range(2)]
    st = [st_ref[0], st_ref[1]]
    for step in range(n_blk):
        for d, o_ref in ((0, of_ref), (1, ob_ref)):
            i = step if d == 0 else n_blk - 1 - step
            r = slice(i * HG_BLK, (i + 1) * HG_BLK)
            q_t, k_t, v_b, bl = prep[d]
            o_ref[r, :] = pairs[d][i] + _dot_nt(q_t[r], _bf(st[d]))
            last = (i + 1) * HG_BLK - 1 if d == 0 else i * HG_BLK
            st[d] = st[d] * jnp.exp(bl[last:last + 1, :]) + _dot_tn(v_b[r], k_t[r]) * bdm
    st_ref[0], st_ref[1] = st


def _hgrn2(p_hg, lb_logits, layer, *, tm, tc):
    bsz, t_all, _ = p_hg.shape
    nc, ncc = t_all // tm, tc // tm
    consts = [jnp.asarray(_np_block_diag_ones(BRANCH_W, HEAD_DIM), BF16),
              jnp.asarray(_np_block_diag_ones(BRANCH_W, HEAD_DIM), F32),
              jnp.asarray(_np_tri(tm, HG_BLK, True, False), BF16),
              jnp.asarray(_np_tri(tm, HG_BLK, True, True), BF16),
              jnp.asarray(_np_tri(tm, HG_BLK, False, False), BF16),
              jnp.asarray(_np_tri(tm, HG_BLK, False, True), BF16)]
    blk = (None, tm, BRANCH_W)
    c2 = lambda b, s: (0, 0)
    fmap = lambda col: (lambda b, s: (b, s, col))
    bmap = lambda col: (lambda b, s: (b, _bwd_order(s, ncc, nc), col))
    return pl.pallas_call(
        functools.partial(_hg_scan_kernel, layer=layer),
        grid=(bsz, nc),
        in_specs=[pl.BlockSpec(blk, fmap(0)), pl.BlockSpec(blk, fmap(1)), pl.BlockSpec(blk, fmap(3)),
                  pl.BlockSpec(blk, bmap(0)), pl.BlockSpec(blk, bmap(2)), pl.BlockSpec(blk, bmap(3)),
                  pl.BlockSpec(lb_logits.shape, c2)] + [pl.BlockSpec(c.shape, c2) for c in consts],
        out_specs=[pl.BlockSpec(blk, fmap(0)), pl.BlockSpec(blk, bmap(0))],
        out_shape=[jax.ShapeDtypeStruct((bsz, t_all, BRANCH_W), F32)] * 2,
        scratch_shapes=[pltpu.VMEM((2, BRANCH_W, BRANCH_W), F32), pltpu.VMEM((2, 4, tm, BRANCH_W), F32)],
        compiler_params=pltpu.CompilerParams(dimension_semantics=("parallel", "arbitrary"),
                                             vmem_limit_bytes=40 * 1024 * 1024),
        name="hgrn2_scan",
    )(p_hg, p_hg, p_hg, p_hg, p_hg, p_hg, lb_logits.astype(F32), *consts)


def _s5_kernel(uf_ref, ub_ref, bb_ref, cc_ref, a_ref, yf_ref, yb_ref, x_ref, h_ref, *, tt, bsz):
    @pl.when(pl.program_id(0) == 0)
    def _():
        h_ref[...] = jnp.zeros_like(h_ref)

    x_ref[0] = _dot(_bf(uf_ref[...]), bb_ref[0])
    x_ref[1] = _dot(_bf(ub_ref[...]), bb_ref[1])

    def step(i, carry):
        hfr, hfi, hbr, hbi = carry
        rf = pl.multiple_of(i * bsz, bsz)
        rb = pl.multiple_of((tt - 1 - i) * bsz, bsz)
        ar, ai = a_ref[0, 0], a_ref[0, 1]
        xr = x_ref[0, pl.ds(rf, bsz), 0:S5_LANES]
        xi = x_ref[0, pl.ds(rf, bsz), S5_LANES:2 * S5_LANES]
        nfr = ar * hfr - ai * hfi + xr
        nfi = ar * hfi + ai * hfr + xi
        x_ref[0, pl.ds(rf, bsz), 0:S5_LANES] = nfr
        x_ref[0, pl.ds(rf, bsz), S5_LANES:2 * S5_LANES] = nfi
        ar, ai = a_ref[1, 0], a_ref[1, 1]
        xr = x_ref[1, pl.ds(rb, bsz), 0:S5_LANES]
        xi = x_ref[1, pl.ds(rb, bsz), S5_LANES:2 * S5_LANES]
        nbr = ar * hbr - ai * hbi + xr
        nbi = ar * hbi + ai * hbr + xi
        x_ref[1, pl.ds(rb, bsz), 0:S5_LANES] = nbr
        x_ref[1, pl.ds(rb, bsz), S5_LANES:2 * S5_LANES] = nbi
        return nfr, nfi, nbr, nbi

    hs = lax.fori_loop(0, tt, step, (h_ref[0, 0], h_ref[0, 1], h_ref[1, 0], h_ref[1, 1]))
    h_ref[0, 0], h_ref[0, 1], h_ref[1, 0], h_ref[1, 1] = hs
    yf_ref[...] = _dot(_bf(x_ref[0]), cc_ref[0])
    yb_ref[...] = _dot(_bf(x_ref[1]), cc_ref[1])


def _s5_matrices(lam_re, lam_im, log_step, b_re, b_im, c_re, c_im):
    lam_re, lam_im = lam_re.astype(F32), lam_im.astype(F32)
    step = jnp.exp(log_step.astype(F32))[..., None]
    mag = jnp.exp(lam_re * step)
    a_re, a_im = mag * jnp.cos(lam_im * step), mag * jnp.sin(lam_im * step)
    den = lam_re * lam_re + lam_im * lam_im
    n_re = a_re - 1.0
    coef_re = (n_re * lam_re + a_im * lam_im) / den
    coef_im = (a_im * lam_re - n_re * lam_im) / den
    b_re, b_im = b_re.astype(F32), b_im.astype(F32)
    bb_re = coef_re[..., None] * b_re - coef_im[..., None] * b_im
    bb_im = coef_re[..., None] * b_im + coef_im[..., None] * b_re
    eye_g = jnp.eye(S5_GROUPS, dtype=F32)
    bb = jnp.stack([bb_re, bb_im], axis=1)
    bb = jnp.einsum('drgph,gk->dghrkp', bb, eye_g).reshape(2, BRANCH_W, 2 * S5_LANES)
    cc = jnp.stack([c_re.astype(F32), -c_im.astype(F32)], axis=0)
    cc = jnp.einsum('rgkp,gj->rjpgk', cc, eye_g).reshape(2 * S5_LANES, BRANCH_W)
    return bb.astype(BF16), cc.astype(BF16), a_re.reshape(2, S5_LANES), a_im.reshape(2, S5_LANES)


def _s5(u_tb, mats, *, bsz, tc, tt):
    bb, cc, a_re, a_im = mats
    n_rows = u_tb.shape[0]
    t_all = n_rows // bsz
    nt, nct = t_all // tt, tc // tt
    a = jnp.broadcast_to(jnp.stack([a_re, a_im], axis=1)[:, :, None, :], (2, 2, bsz, S5_LANES))
    cc2 = jnp.broadcast_to(cc[None], (2,) + cc.shape)
    blk = (tt * bsz, BRANCH_W)
    fmap = lambda s: (s, 0)
    bmap = lambda s: (_bwd_order(s, nct, nt), 0)
    return pl.pallas_call(
        functools.partial(_s5_kernel, tt=tt, bsz=bsz),
        grid=(nt,),
        in_specs=[pl.BlockSpec(blk, fmap), pl.BlockSpec(blk, bmap),
                  pl.BlockSpec(bb.shape, lambda s: (0, 0, 0)), pl.BlockSpec(cc2.shape, lambda s: (0, 0, 0)),
                  pl.BlockSpec(a.shape, lambda s: (0, 0, 0, 0))],
        out_specs=[pl.BlockSpec(blk, fmap), pl.BlockSpec(blk, bmap)],
        out_shape=[jax.ShapeDtypeStruct((n_rows, BRANCH_W), F32)] * 2,
        scratch_shapes=[pltpu.VMEM((2, tt * bsz, 2 * S5_LANES), F32), pltpu.VMEM((2, 2, bsz, S5_LANES), F32)],
        compiler_params=pltpu.CompilerParams(dimension_semantics=("arbitrary",),
                                             vmem_limit_bytes=V7X_VMEM_LIMIT),
        name="s5_scan",
    )(u_tb, u_tb, bb, cc2, a)


def _rope(x, cos, sin_signed):
    w = x.shape[-1]
    lane = lax.broadcasted_iota(jnp.int32, x.shape, 1)
    up = pltpu.roll(x, w - 16, 1)
    dn = pltpu.roll(x, 16, 1)
    return x * cos + jnp.where((lane & 31) < 16, up, dn) * sin_signed


def _attn_kernel(q_ref, kv_ref, cq_ref, sq_ref, ck_ref, sk_ref, qg_ref, kg_ref, bdq_ref, bdk_ref,
                 o_ref, kp_ref, vp_ref, *, tm, tc, nct):
    j = pl.program_id(1)
    t_all = kv_ref.shape[0]
    half = BRANCH_W // 2

    @pl.when(j == 0)
    def _():
        def pairs(x):
            lane = lax.broadcasted_iota(jnp.int32, x.shape, 1)
            sw = pltpu.roll(x, HEAD_DIM, 1)
            return _bf(jnp.where(lane < HEAD_DIM, x, sw)), _bf(jnp.where(lane < HEAD_DIM, sw, x))

        for lo, hi, rope in ((0, tc, False), (tc, t_all, True)):
            k = _head_rms(kv_ref[lo:hi, 0:half], bdk_ref[...], kg_ref[...])
            if rope:
                k = _rope(k, ck_ref[...], sk_ref[...])
            kp_ref[0, lo:hi, :], kp_ref[1, lo:hi, :] = pairs(k)
            v = kv_ref[lo:hi, half:2 * half]
            lane = lax.broadcasted_iota(jnp.int32, v.shape, 1)
            sw = pltpu.roll(v, HEAD_DIM, 1)
            for g in range(2):
                v_lo, v_hi = (v, sw) if g == 0 else (sw, v)
                vp_ref[g, 0, lo:hi, :] = _bf(jnp.where(lane < HEAD_DIM, v_lo, 1.0))
                vp_ref[g, 1, lo:hi, :] = _bf(jnp.where(lane < HEAD_DIM, 1.0, v_hi))

    def attend(q, nk):
        lane = lax.broadcasted_iota(jnp.int32, (tm, half), 1)
        outs = []
        for g in range(2):
            qp = q[:, g * half:(g + 1) * half]
            acc = None
            for hh in range(2):
                sel = (lane < HEAD_DIM) if hh == 0 else (lane >= HEAD_DIM)
                s = _dot_nt(_bf(jnp.where(sel, qp, 0.0)), kp_ref[g, 0:nk, :])
                p = jnp.exp(_bf(s - jnp.max(s, axis=-1, keepdims=True)))
                ol = _dot(p, vp_ref[g, hh, 0:nk, :])
                res = ol / pltpu.roll(ol, HEAD_DIM, 1)
                acc = res if acc is None else jnp.where(sel, res, acc)
            outs.append(acc)
        o_ref[...] = jnp.concatenate(outs, axis=1)

    qn = _head_rms(q_ref[...], bdq_ref[...], qg_ref[...]) * (HEAD_DIM ** -0.5)

    @pl.when(j < nct)
    def _():
        attend(qn, tc)

    @pl.when(j >= nct)
    def _():
        attend(_rope(qn, cq_ref[...], sq_ref[...]), t_all)


def _rope_tables(n_tokens):
    rows = n_tokens // GRID_W
    r, col = np.meshgrid(np.arange(rows), np.arange(GRID_W), indexing='ij')
    axis_dim = HEAD_DIM // 2
    inv = jnp.asarray(ROPE_THETA, F32) ** (-jnp.arange(0, axis_dim, 2, dtype=F32) / axis_dim)
    ang_r = jnp.asarray(r.reshape(-1, 1), F32) * inv
    ang_c = jnp.asarray(col.reshape(-1, 1), F32) * inv
    ang = jnp.concatenate([ang_r, ang_r, ang_c, ang_c], axis=-1)
    sign = np.where((np.arange(HEAD_DIM) % 32) < 16, -1.0, 1.0).astype(np.float32)
    cos, sin = jnp.cos(ang), jnp.sin(ang) * sign
    return jnp.tile(cos, (1, N_HEADS)), jnp.tile(sin, (1, N_HEADS))


def _attention(p_q, p_kv, qn_g, kn_g, cos4, sin4, *, tm, tc):
    bsz, t_all, _ = p_q.shape
    nt, nct = t_all // tm, tc // tm
    half = BRANCH_W // 2
    tok = lambda b, j: (b, j, 0)
    c2 = lambda b, j: (0, 0)
    lat = lambda b, j: (jnp.maximum(j - nct, 0), 0)
    qg = jnp.tile(qn_g.astype(F32), N_HEADS).reshape(1, BRANCH_W)
    kg = jnp.tile(kn_g.astype(F32), 2).reshape(1, half)
    bdq = jnp.asarray(_np_block_diag_ones(BRANCH_W, HEAD_DIM), BF16)
    bdk = jnp.asarray(_np_block_diag_ones(half, HEAD_DIM), BF16)
    cosk, sink = cos4[:, 0:half], sin4[:, 0:half]
    return pl.pallas_call(
        functools.partial(_attn_kernel, tm=tm, tc=tc, nct=nct),
        grid=(bsz, nt),
        in_specs=[pl.BlockSpec((None, tm, BRANCH_W), tok),
                  pl.BlockSpec((None, t_all, BRANCH_W), lambda b, j: (b, 0, 0)),
                  pl.BlockSpec((tm, BRANCH_W), lat), pl.BlockSpec((tm, BRANCH_W), lat),
                  pl.BlockSpec(cosk.shape, c2), pl.BlockSpec(sink.shape, c2),
                  pl.BlockSpec((1, BRANCH_W), c2), pl.BlockSpec((1, half), c2),
                  pl.BlockSpec(bdq.shape, c2), pl.BlockSpec(bdk.shape, c2)],
        out_specs=pl.BlockSpec((None, tm, BRANCH_W), tok),
        out_shape=jax.ShapeDtypeStruct((bsz, t_all, BRANCH_W), F32),
        scratch_shapes=[pltpu.VMEM((2, t_all, half), BF16), pltpu.VMEM((2, 2, t_all, half), BF16)],
        compiler_params=pltpu.CompilerParams(dimension_semantics=("parallel", "arbitrary"),
                                             vmem_limit_bytes=V7X_VMEM_LIMIT),
        name="attention",
    )(p_q, p_kv, cos4, sin4, cosk, sink, qg, kg, bdq, bdk)


def _merge_kernel(x_ref, m_ref, g_ref, dnf_ref, dnb_ref, z_ref, sf_ref, sb_ref, u_ref, hf_ref, hb_ref, hgate_ref,
                  at_ref, dng_ref, hgg_ref, dsk_ref, glu_ref, bd_ref, wg_ref, wb_ref, wo_ref, o_ref):
    x = x_ref[...]
    bd = bd_ref[...]
    h = _bf(_rms(x) * g_ref[...] * (1.0 + m_ref[4:5, :]) + m_ref[3:4, :])
    y_dn = _head_rms(dnf_ref[...] + dnb_ref[...], bd, dng_ref[...]) * _silu(z_ref[...])
    ys = sf_ref[...] + sb_ref[...] + dsk_ref[...] * u_ref[...]
    ab = _dot(_bf(jax.nn.gelu(ys, approximate=True)), glu_ref[...])
    y_s5 = ab[:, 0:BRANCH_W] * _sigmoid(ab[:, BRANCH_W:2 * BRANCH_W])
    y_hg = _head_rms(hf_ref[...] + hb_ref[...], bd, hgg_ref[...]) * _sigmoid(hgate_ref[...])
    ys_all = (y_dn, y_s5, y_hg, at_ref[...])
    acc = jnp.zeros(x.shape, F32)
    for i in range(4):
        acc = acc + _sigmoid(_dot(h, wg_ref[i])) * _dot(_bf(ys_all[i]), wb_ref[i])
    o_ref[...] = x + m_ref[5:6, :] * _dot(_bf(acc), wo_ref[...])


def _merge(xj, mods, g, br, params, *, tm, nct, j0):
    bsz, t_all, _ = xj.shape
    n_tiles = t_all // tm - j0
    tok = lambda b, j: (b, j + j0, 0)
    col = lambda c: (lambda b, j: (b, j + j0, c))
    mod_map = lambda b, j: (jnp.where(j + j0 < nct, bsz, b), 0, 0)
    c2 = lambda b, j: (0, 0)
    c3 = lambda b, j: (0, 0, 0)
    tb = (None, tm, BRANCH_W)
    dn_g, hg_g, d_skip, w_glu, wg, wb, wo = params
    bd = jnp.asarray(_np_block_diag_ones(BRANCH_W, HEAD_DIM), BF16)
    return pl.pallas_call(
        _merge_kernel,
        grid=(bsz, n_tiles),
        in_specs=[pl.BlockSpec((None, tm, D_MODEL), tok),
                  pl.BlockSpec((None, N_MOD, D_MODEL), mod_map),
                  pl.BlockSpec((1, D_MODEL), c2),
                  pl.BlockSpec(tb, tok), pl.BlockSpec(tb, tok), pl.BlockSpec(tb, col(3)),
                  pl.BlockSpec(tb, tok), pl.BlockSpec(tb, tok), pl.BlockSpec(tb, tok),
                  pl.BlockSpec(tb, tok), pl.BlockSpec(tb, tok), pl.BlockSpec(tb, col(4)),
                  pl.BlockSpec(tb, tok),
                  pl.BlockSpec((1, BRANCH_W), c2), pl.BlockSpec((1, BRANCH_W), c2), pl.BlockSpec((1, BRANCH_W), c2),
                  pl.BlockSpec(w_glu.shape, c2), pl.BlockSpec(bd.shape, c2),
                  pl.BlockSpec(wg.shape, c3), pl.BlockSpec(wb.shape, c3), pl.BlockSpec(wo.shape, c2)],
        out_specs=pl.BlockSpec((None, tm, D_MODEL), lambda b, j: (b, j, 0)),
        out_shape=jax.ShapeDtypeStruct((bsz, n_tiles * tm, D_MODEL), F32),
        compiler_params=pltpu.CompilerParams(dimension_semantics=("parallel", "parallel"),
                                             vmem_limit_bytes=V7X_VMEM_LIMIT),
        name="branch_merge",
    )(xj, mods, g.reshape(1, D_MODEL), br['dn_f'], br['dn_b'], br['p_dn'], br['s5_f'], br['s5_b'], br['p_s5'],
      br['hg_f'], br['hg_b'], br['p_hg'], br['at'], dn_g, hg_g, d_skip, w_glu, bd, wg, wb, wo)


def kernel(x, c, ctx, c_ctx, ada_w, ada_b, norm_g, ffn_w1, ffn_w3, ffn_w2, w_in, dn_conv, dn_a_log, dn_dt_bias,
           dn_norm_g, s5_lam_re, s5_lam_im, s5_log_step, s5_b_re, s5_b_im, s5_c_re, s5_c_im, s5_d, s5_glu,
           hg_lb_logits, hg_norm_g, at_qn_g, at_kn_g, w_branch, w_out, final_g):
    bsz, tl, _ = x.shape
    tc = ctx.shape[1]
    t_all = tc + tl
    depth = ada_w.shape[0]
    tm = 256 if (tc % 256 == 0 and tl % 256 == 0) else 128
    assert tc % tm == 0 and tl % tm == 0 and tm % CHUNK == 0 and bsz % 8 == 0
    nct = tc // tm
    d_ff = ffn_w1.shape[-1]
    n_ff = d_ff // FF_CHUNK

    n_rows = -(-(bsz + 1) // 8) * 8
    cond = jnp.zeros((n_rows, D_MODEL), F32).at[0:bsz].set(c.astype(F32)).at[bsz].set(c_ctx.astype(F32))
    mods_all = _ada_mods(cond, ada_w.astype(F32), ada_b.astype(F32))
    cos4, sin4 = _rope_tables(tl)
    xj = jnp.concatenate([ctx, x], axis=1).astype(F32)

    def ffn_weights(l, i):
        w1 = ffn_w1[l, i].astype(BF16).reshape(D_MODEL, n_ff, FF_CHUNK).transpose(1, 0, 2)
        w3 = ffn_w3[l, i].astype(BF16).reshape(D_MODEL, n_ff, FF_CHUNK).transpose(1, 0, 2)
        w2 = ffn_w2[l, i].astype(BF16)
        return w1, w3, w2

    for l in range(depth):
        last = l == depth - 1
        mods = mods_all[l]
        xj = _ffn_half_step(xj, mods, norm_g[l, 0], *ffn_weights(l, 0), mod_base=0, tm=tm, nct=nct)

        wl = w_in[l]
        w_dn = jnp.concatenate([wl[:, 0:1040], jnp.zeros((D_MODEL, 112), wl.dtype)], axis=1)
        ws = [w.astype(BF16) for w in (w_dn, wl[:, 1040:1296], wl[:, 1296:2576], wl[:, 2576:2832], wl[:, 2832:3088])]
        p_dn, p_s5, p_hg, p_aq, p_kv = _input_projection(xj, mods, norm_g[l, 1], ws, tm=tm, nct=nct)

        dn_f, dn_b = _deltanet(p_dn, dn_conv[l], dn_a_log[l], dn_dt_bias[l], tm=tm, tc=tc)
        hg_f, hg_b = _hgrn2(p_hg, hg_lb_logits, l, tm=tm, tc=tc)
        mats = _s5_matrices(s5_lam_re[l], s5_lam_im[l], s5_log_step[l], s5_b_re[l], s5_b_im[l], s5_c_re[l], s5_c_im[l])
        u_tb = p_s5.transpose(1, 0, 2).reshape(t_all * bsz, BRANCH_W)
        s5_f, s5_b = _s5(u_tb, mats, bsz=bsz, tc=tc, tt=CHUNK)
        s5_f = s5_f.reshape(t_all, bsz, BRANCH_W).transpose(1, 0, 2)
        s5_b = s5_b.reshape(t_all, bsz, BRANCH_W).transpose(1, 0, 2)
        y_at = _attention(p_aq, p_kv, at_qn_g[l], at_kn_g[l], cos4, sin4, tm=tm, tc=tc)

        br = dict(dn_f=dn_f, dn_b=dn_b, p_dn=p_dn, s5_f=s5_f, s5_b=s5_b, p_s5=p_s5, hg_f=hg_f, hg_b=hg_b,
                  p_hg=p_hg, at=y_at)
        params = (jnp.tile(dn_norm_g[l].astype(F32), N_HEADS).reshape(1, BRANCH_W),
                  jnp.tile(hg_norm_g[l].astype(F32), N_HEADS).reshape(1, BRANCH_W),
                  s5_d[l].astype(F32).reshape(1, BRANCH_W),
                  s5_glu[l].astype(BF16),
                  wl[:, 3088:].astype(BF16).reshape(D_MODEL, 4, D_MODEL).transpose(1, 0, 2),
                  w_branch[l].astype(BF16),
                  w_out[l].astype(BF16))
        xj = _merge(xj, mods, norm_g[l, 1], br, params, tm=tm, nct=nct, j0=nct if last else 0)
        xj = _ffn_half_step(xj, mods, norm_g[l, 2], *ffn_weights(l, 1), mod_base=6, tm=tm,
                            nct=0 if last else nct, final_g=final_g.astype(F32) if last else None)
    return xj.astype(x.dtype)
```

```python
import functools
import math

import numpy as np
import jax
import jax.numpy as jnp
from jax import lax
from jax.experimental import pallas as pl
from jax.experimental.pallas import tpu as pltpu

F32 = jnp.float32
BF16 = jnp.bfloat16

D_MODEL = 1024
N_MOD = 9
BRANCH_W = 256
HEAD_DIM = 64
N_HEADS = 4
CHUNK = 64
HG_BLK = 16
EPS = 1e-6
LOG_FLOOR = 1e-30
GRID_W = 64
ROPE_THETA = 10000.0
FF_CHUNK = 256
S5_GROUPS = 16
S5_GROUP = 16
S5_STATE = 64
S5_LANES = S5_GROUPS * S5_STATE
NEG_BIG = -1e30
V7X_VMEM_LIMIT = 56 * 1024 * 1024


def _bf(x):
    return x.astype(BF16)


def _dot(a, b):
    return jnp.dot(a, b, preferred_element_type=F32)


def _dot_nt(a, b):
    return lax.dot_general(a, b, (((1,), (1,)), ((), ())), preferred_element_type=F32)


def _dot_tn(a, b):
    return lax.dot_general(a, b, (((0,), (0,)), ((), ())), preferred_element_type=F32)


def _split2(x):
    hi = _bf(x)
    lo = _bf(x - hi.astype(F32))
    return hi, lo


def _split3(x):
    x1 = _bf(x)
    r1 = x - x1.astype(F32)
    x2 = _bf(r1)
    x3 = _bf(r1 - x2.astype(F32))
    return x1, x2, x3


def _dot_x2(x, m):
    hi, lo = _split2(x)
    return _dot(hi, m) + _dot(lo, m)


def _dot_x3(x, m):
    x1, x2, x3 = _split3(x)
    return _dot(x1, m) + _dot(x2, m) + _dot(x3, m)


def _dot3_left(m, x):
    x1, x2, x3 = _split3(x)
    return _dot(m, x1) + _dot(m, x2) + _dot(m, x3)


def _dot_hp(a, b):
    a_hi, a_lo = _split2(a)
    b_hi, b_lo = _split2(b)
    return _dot(a_hi, b_hi) + _dot(a_hi, b_lo) + _dot(a_lo, b_hi)


def _sigmoid(x):
    return jax.nn.sigmoid(x)


def _silu(x):
    return x * jax.nn.sigmoid(x)


def _rms(x):
    return x * lax.rsqrt(jnp.mean(x * x, axis=-1, keepdims=True) + EPS)


def _head_rms(x, bd_ones, gain):
    ss = _dot_x2(x * x, bd_ones) * (1.0 / HEAD_DIM)
    return x * lax.rsqrt(ss + EPS) * gain


def _stack4(x):
    return jnp.concatenate([x, x, x, x], axis=0)


def _unstack4(x):
    n = x.shape[0] // 4
    return x[0:n] + x[n:2 * n] + x[2 * n:3 * n] + x[3 * n:4 * n]


def _np_block_diag_ones(n, blk):
    i = np.arange(n)
    return (i[:, None] // blk == i[None, :] // blk).astype(np.float32)


def _np_tri(n, blk, lower, strict):
    i = np.arange(n)
    same = i[:, None] // blk == i[None, :] // blk
    a, b = i[:, None] % blk, i[None, :] % blk
    if lower:
        t = (a > b) if strict else (a >= b)
    else:
        t = (a < b) if strict else (a <= b)
    return (same & t).astype(np.float32)


def _np_doubling_masks(n, blk):
    i = np.arange(n)
    r, c = i[:, None], i[None, :]
    out = [(r // 2 == c // 2)]
    s = 2
    while s < blk:
        out.append((r // (2 * s) == c // (2 * s)) & (r // s != c // s))
        s *= 2
    return np.stack(out).astype(np.float32)


def _bwd_order(s, n_ctx, n_all):
    return jnp.where(s < n_ctx, n_ctx - 1 - s, n_all + n_ctx - 1 - s)


def _ada_kernel(c_ref, w_ref, b_ref, o_ref):
    s = _silu(c_ref[...])
    w = w_ref[...]
    s_hi, s_lo = _split2(s)
    w_hi, w_lo = _split2(w)
    o_ref[...] = _dot(s_hi, w_hi) + _dot(s_hi, w_lo) + _dot(s_lo, w_hi) + b_ref[...]


def _ada_mods(cond, ada_w, ada_b):
    n_layers = ada_w.shape[0]
    r = cond.shape[0]
    n_out = ada_w.shape[2]
    tn = 1152
    out = pl.pallas_call(
        _ada_kernel,
        grid=(n_layers, n_out // tn),
        in_specs=[
            pl.BlockSpec((r, D_MODEL), lambda l, n: (0, 0)),
            pl.BlockSpec((None, D_MODEL, tn), lambda l, n: (l, 0, n)),
            pl.BlockSpec((None, 1, tn), lambda l, n: (l, 0, n)),
        ],
        out_specs=pl.BlockSpec((None, r, tn), lambda l, n: (l, 0, n)),
        out_shape=jax.ShapeDtypeStruct((n_layers, r, n_out), F32),
        compiler_params=pltpu.CompilerParams(dimension_semantics=("parallel", "parallel"),
                                             vmem_limit_bytes=40 * 1024 * 1024),
        name="ada_mods",
    )(cond, ada_w, ada_b.reshape(n_layers, 1, n_out))
    return out.reshape(n_layers, r, N_MOD, D_MODEL)


def _ffn_kernel(*refs, mod_base, final):
    if final:
        x_ref, m_ref, g_ref, w1_ref, w3_ref, w2_ref, fg_ref, o_ref = refs
    else:
        x_ref, m_ref, g_ref, w1_ref, w3_ref, w2_ref, o_ref = refs
    x = x_ref[...]
    shift = m_ref[mod_base:mod_base + 1, :]
    scale = m_ref[mod_base + 1:mod_base + 2, :]
    gate = m_ref[mod_base + 2:mod_base + 3, :]
    h = _bf(_rms(x) * g_ref[...] * (1.0 + scale) + shift)
    us = []
    for c in range(w1_ref.shape[0]):
        a = _dot(h, w1_ref[c])
        us.append(_bf(_silu(a) * _dot(h, w3_ref[c])))
    out = x + 0.5 * gate * _dot(jnp.concatenate(us, axis=1), w2_ref[...])
    if final:
        out = _rms(out) * fg_ref[...]
    o_ref[...] = out


def _ffn_half_step(xj, mods, g, w1c, w3c, w2c, *, mod_base, tm, nct, final_g=None):
    bsz, t_all, _ = xj.shape
    final = final_g is not None

    def tok_map(b, j):
        return (b, j, 0)

    def mod_map(b, j):
        return (jnp.where(j < nct, bsz, b), 0, 0)

    const3 = lambda b, j: (0, 0, 0)
    const2 = lambda b, j: (0, 0)
    in_specs = [
        pl.BlockSpec((None, tm, D_MODEL), tok_map),
        pl.BlockSpec((None, N_MOD, D_MODEL), mod_map),
        pl.BlockSpec((1, D_MODEL), const2),
        pl.BlockSpec(w1c.shape, const3),
        pl.BlockSpec(w3c.shape, const3),
        pl.BlockSpec(w2c.shape, const2),
    ]
    args = [xj, mods, g.reshape(1, D_MODEL), w1c, w3c, w2c]
    if final:
        in_specs.append(pl.BlockSpec((1, D_MODEL), const2))
        args.append(final_g.reshape(1, D_MODEL))
    return pl.pallas_call(
        functools.partial(_ffn_kernel, mod_base=mod_base, final=final),
        grid=(bsz, t_all // tm),
        in_specs=in_specs,
        out_specs=pl.BlockSpec((None, tm, D_MODEL), tok_map),
        out_shape=jax.ShapeDtypeStruct(xj.shape, F32),
        compiler_params=pltpu.CompilerParams(dimension_semantics=("parallel", "parallel"),
                                             vmem_limit_bytes=V7X_VMEM_LIMIT),
        name="ffn_half_step",
    )(*args)


def _inproj_kernel(x_ref, m_ref, g_ref, wdn_ref, ws5_ref, whg_ref, waq_ref, wkv_ref,
                   dn_ref, s5_ref, hg_ref, aq_ref, kv_ref):
    x = x_ref[...]
    h = _bf(_rms(x) * g_ref[...] * (1.0 + m_ref[4:5, :]) + m_ref[3:4, :])
    dn_ref[...] = _dot(h, wdn_ref[...])
    s5_ref[...] = _dot(h, ws5_ref[...])
    hg_ref[...] = _dot(h, whg_ref[...])
    aq_ref[...] = _dot(h, waq_ref[...])
    kv_ref[...] = _dot(h, wkv_ref[...])


def _input_projection(xj, mods, g, ws, *, tm, nct):
    bsz, t_all, _ = xj.shape
    tok_map = lambda b, j: (b, j, 0)
    mod_map = lambda b, j: (jnp.where(j < nct, bsz, b), 0, 0)
    const2 = lambda b, j: (0, 0)
    widths = [w.shape[1] for w in ws]
    return pl.pallas_call(
        _inproj_kernel,
        grid=(bsz, t_all // tm),
        in_specs=[pl.BlockSpec((None, tm, D_MODEL), tok_map),
                  pl.BlockSpec((None, N_MOD, D_MODEL), mod_map),
                  pl.BlockSpec((1, D_MODEL), const2)]
                 + [pl.BlockSpec(w.shape, const2) for w in ws],
        out_specs=[pl.BlockSpec((None, tm, n), tok_map) for n in widths],
        out_shape=[jax.ShapeDtypeStruct((bsz, t_all, n), F32) for n in widths],
        compiler_params=pltpu.CompilerParams(dimension_semantics=("parallel", "parallel"),
                                             vmem_limit_bytes=V7X_VMEM_LIMIT),
        name="input_projection",
    )(xj, mods, g.reshape(1, D_MODEL), *ws)


def _dn_prep_kernel(p_ref, prev_ref, next_ref, cw_ref, al_ref, dtb_ref, e_ref, tril_ref, triu_ref, bd_ref,
                    q_ref, k_ref, v_ref, gb_ref, *, tm, nct, nt):
    j = pl.program_id(1)
    first = jnp.logical_or(j == 0, j == nct)
    last = jnp.logical_or(j == nct - 1, j == nt - 1)
    x = p_ref[:, 0:3 * BRANCH_W]
    pv = prev_ref[...] * jnp.where(first, 0.0, 1.0)
    nx = next_ref[...] * jnp.where(last, 0.0, 1.0)
    ext = jnp.concatenate([pv, x, nx], axis=0)
    n = tm + 16
    acc = jnp.zeros((tm, 3 * BRANCH_W), F32)
    for t in range(5):
        sh = (2 - t) % n
        r = ext if sh == 0 else pltpu.roll(ext, sh, 0)
        acc = acc + r[8:8 + tm] * cw_ref[t:t + 1, :]
    h = _silu(acc)
    bd = bd_ref[...]

    def l2n(t):
        return t * lax.rsqrt(_dot_x2(t * t, bd) + EPS)

    q_ref[...] = l2n(h[:, 0:BRANCH_W]) * (HEAD_DIM ** -0.5)
    k_ref[...] = l2n(h[:, BRANCH_W:2 * BRANCH_W])
    v_ref[...] = h[:, 2 * BRANCH_W:3 * BRANCH_W]

    ab = p_ref[:, 4 * BRANCH_W:4 * BRANCH_W + 128]
    z = ab + dtb_ref[...]
    softplus = jnp.maximum(z, 0.0) + jnp.log1p(jnp.exp(-jnp.abs(z)))
    g = -jnp.exp(al_ref[...]) * softplus
    lane = lax.broadcasted_iota(jnp.int32, ab.shape, 1)
    g = jnp.where(lane < 8, g, 0.0)
    cs_f = _dot3_left(tril_ref[...], g)
    cs_b = _dot3_left(triu_ref[...], g)
    comp = jnp.where(lane < 4, cs_f, jnp.where(lane < 8, cs_b, jnp.where(lane < 16, _sigmoid(ab), 0.0)))
    gb_ref[...] = _dot_x3(comp, e_ref[...])


def _dn_chunks_local(items, bd, it, lvl_ref):
    def stk(x):
        return _stack4(_bf(x)) * bd

    ks = [stk(c[1]) for c in items]
    gq = [_dot_nt(_bf(jnp.concatenate([c[1] * c[4], c[0]], axis=0)), x) for c, x in zip(items, ks)]
    decays, a_s = [], []
    for (q, k, v, gc, be, neg_incl, tri_strict, fwd), g in zip(items, gq):
        col_gc = jnp.sum(gc * it, axis=0, keepdims=True)
        decay = jnp.exp(gc - col_gc + neg_incl)
        decays.append(decay)
        a_s.append(g[0:CHUNK] * decay * tri_strict)
    ts = [it - a * lvl_ref[0] for a in a_s]
    for lv in range(1, lvl_ref.shape[0]):
        xs = [_dot(_bf(a * lvl_ref[lv]), stk(t)) for a, t in zip(a_s, ts)]
        ts = [t - _dot(_bf(t), stk(x)) for t, x in zip(ts, xs)]
    egcs = [jnp.exp(c[3]) for c in items]
    uws = [_dot(_bf(t), jnp.concatenate([stk(c[2] * c[4]), stk(c[1] * c[4] * e)], axis=1))
           for t, c, e in zip(ts, items, egcs)]
    out = []
    for (q, k, v, gc, be, _, _, fwd), uw, g, dc, egc in zip(items, uws, gq, decays, egcs):
        total = gc[CHUNK - 1:CHUNK, :] if fwd else gc[0:1, :]
        out.append((uw[:, 0:BRANCH_W], _bf(uw[:, BRANCH_W:2 * BRANCH_W]), _bf(g[CHUNK:2 * CHUNK] * dc),
                    _bf(q * egc), _bf(k * jnp.exp(total - gc)), jnp.exp(total)))
    return out


def _dn_chunk_state(local, s_prev, bd, bd_f32):
    u, w, attn, q_dec, k_dec, g_last = local
    ws = _dot(jnp.concatenate([w, q_dec], axis=0), _bf(s_prev))
    v_new = _bf(u - ws[0:CHUNK])
    o = ws[CHUNK:2 * CHUNK] + _dot(attn, _stack4(v_new) * bd)
    s_new = s_prev * g_last + _dot_tn(k_dec, v_new) * bd_f32
    return o, s_new


def _dn_scan_kernel(qf_ref, kf_ref, vf_ref, gcf_ref, bef_ref, qb_ref, kb_ref, vb_ref, gcb_ref, beb_ref,
                    bdf_ref, li_ref, ls_ref, ui_ref, us_ref, it_ref, lvl_ref, bd_ref,
                    of_ref, ob_ref, s_ref):
    @pl.when(pl.program_id(1) == 0)
    def _():
        s_ref[...] = jnp.zeros_like(s_ref)

    bd, bd_f32, it = bd_ref[...], bdf_ref[...], it_ref[...]
    n_sub = qf_ref.shape[0] // CHUNK
    dirs = ((qf_ref, kf_ref, vf_ref, gcf_ref, bef_ref, li_ref, ls_ref, of_ref, True),
            (qb_ref, kb_ref, vb_ref, gcb_ref, beb_ref, ui_ref, us_ref, ob_ref, False))
    items = []
    for q_ref, k_ref, v_ref, gc_ref, be_ref, ti_ref, ts_ref, _, fwd in dirs:
        for g in range(n_sub):
            r = slice(g * CHUNK, (g + 1) * CHUNK)
            items.append((q_ref[r, :], k_ref[r, :], v_ref[r, :], gc_ref[r, :], be_ref[r, :],
                          ti_ref[...], ts_ref[...], fwd))
    local = _dn_chunks_local(items, bd, it, lvl_ref)
    s_f, s_b = s_ref[0], s_ref[1]
    for i in range(n_sub):
        gb = n_sub - 1 - i
        o, s_f = _dn_chunk_state(local[i], s_f, bd, bd_f32)
        of_ref[i * CHUNK:(i + 1) * CHUNK, :] = o
        o, s_b = _dn_chunk_state(local[n_sub + gb], s_b, bd, bd_f32)
        ob_ref[gb * CHUNK:(gb + 1) * CHUNK, :] = o
    s_ref[0] = s_f
    s_ref[1] = s_b


def _deltanet(p_dn, conv_w, a_log, dt_bias, *, tm, tc):
    bsz, t_all, _ = p_dn.shape
    nt, nct = t_all // tm, tc // tm
    al = jnp.zeros((1, 128), F32).at[0, 0:8].set(a_log.reshape(8).astype(F32))
    dtb = jnp.zeros((1, 128), F32).at[0, 0:8].set(dt_bias.reshape(8).astype(F32))
    e = np.zeros((128, 4 * BRANCH_W), np.float32)
    for jj in range(4):
        for h in range(N_HEADS):
            e[4 * jj + h, jj * BRANCH_W + h * HEAD_DIM: jj * BRANCH_W + (h + 1) * HEAD_DIM] = 1.0
    consts = [jnp.asarray(e, BF16),
              jnp.asarray(_np_tri(tm, CHUNK, True, False), BF16),
              jnp.asarray(_np_tri(tm, CHUNK, False, False), BF16),
              jnp.asarray(_np_block_diag_ones(BRANCH_W, HEAD_DIM), BF16)]
    tok = lambda b, j: (b, j, 0)
    c2 = lambda b, j: (0, 0)
    hb = tm // 8
    q, k, v, gb = pl.pallas_call(
        functools.partial(_dn_prep_kernel, tm=tm, nct=nct, nt=nt),
        grid=(bsz, nt),
        in_specs=[pl.BlockSpec((None, tm, p_dn.shape[2]), tok),
                  pl.BlockSpec((None, 8, 3 * BRANCH_W), lambda b, j: (b, jnp.maximum(j * hb - 1, 0), 0)),
                  pl.BlockSpec((None, 8, 3 * BRANCH_W), lambda b, j: (b, jnp.minimum((j + 1) * hb, nt * hb - 1), 0)),
                  pl.BlockSpec(conv_w.shape, c2), pl.BlockSpec((1, 128), c2), pl.BlockSpec((1, 128), c2)]
                 + [pl.BlockSpec(c.shape, c2) for c in consts],
        out_specs=[pl.BlockSpec((None, tm, BRANCH_W), tok)] * 3 + [pl.BlockSpec((None, tm, 4 * BRANCH_W), tok)],
        out_shape=[jax.ShapeDtypeStruct((bsz, t_all, BRANCH_W), F32)] * 3
                  + [jax.ShapeDtypeStruct((bsz, t_all, 4 * BRANCH_W), F32)],
        compiler_params=pltpu.CompilerParams(dimension_semantics=("parallel", "parallel"),
                                             vmem_limit_bytes=40 * 1024 * 1024),
        name="deltanet_prep",
    )(p_dn, p_dn, p_dn, conv_w.astype(F32), al, dtb, *consts)

    nc, ncc = nt, nct
    n4 = N_HEADS * CHUNK
    neg = lambda tri: (1.0 - tri) * NEG_BIG
    per_head = lambda m: np.tile(m, (1,) * (m.ndim - 1) + (N_HEADS,))
    masks = [jnp.asarray(_np_block_diag_ones(n4, CHUNK), F32),
             jnp.asarray(per_head(neg(_np_tri(CHUNK, CHUNK, True, False))), F32),
             jnp.asarray(per_head(_np_tri(CHUNK, CHUNK, True, True)), F32),
             jnp.asarray(per_head(neg(_np_tri(CHUNK, CHUNK, False, False))), F32),
             jnp.asarray(per_head(_np_tri(CHUNK, CHUNK, False, True)), F32),
             jnp.asarray(per_head(np.eye(CHUNK, dtype=np.float32)), F32),
             jnp.asarray(per_head(_np_doubling_masks(CHUNK, CHUNK)), F32),
             jnp.asarray(_np_block_diag_ones(n4, CHUNK), BF16)]
    cmap = lambda m: (lambda b, s: (0,) * m.ndim)
    blk = (None, tm, BRANCH_W)
    fmap = lambda col: (lambda b, s: (b, s, col))
    bmap = lambda col: (lambda b, s: (b, _bwd_order(s, ncc, nc), col))
    o_f, o_b = pl.pallas_call(
        _dn_scan_kernel,
        grid=(bsz, nc),
        in_specs=[pl.BlockSpec(blk, fmap(0))] * 3 + [pl.BlockSpec(blk, fmap(0)), pl.BlockSpec(blk, fmap(2))]
                 + [pl.BlockSpec(blk, bmap(0))] * 3 + [pl.BlockSpec(blk, bmap(1)), pl.BlockSpec(blk, bmap(3))]
                 + [pl.BlockSpec(m.shape, cmap(m)) for m in masks],
        out_specs=[pl.BlockSpec(blk, fmap(0)), pl.BlockSpec(blk, bmap(0))],
        out_shape=[jax.ShapeDtypeStruct((bsz, t_all, BRANCH_W), F32)] * 2,
        scratch_shapes=[pltpu.VMEM((2, n4, n4), F32)],
        compiler_params=pltpu.CompilerParams(dimension_semantics=("parallel", "arbitrary"),
                                             vmem_limit_bytes=40 * 1024 * 1024),
        name="deltanet_scan",
    )(q, k, v, gb, gb, q, k, v, gb, gb, *masks)
    return o_f, o_b


def _hg_tile_prepare(q, f, v, lb, tri_incl, tri_excl, tmp_ref):
    qs = _silu(q)
    logf = jnp.log(jnp.maximum(lb + (1.0 - lb) * _sigmoid(f), LOG_FLOOR))
    kk = (1.0 - lb) * _sigmoid(-f)
    bl = _dot3_left(tri_incl, logf)
    suf = _dot3_left(tri_excl, logf)
    tmp_ref[0], tmp_ref[1], tmp_ref[2], tmp_ref[3] = qs, bl - jnp.log(kk), v, bl
    return _bf(qs * jnp.exp(bl)), _bf(kk * jnp.exp(suf)), _bf(v), bl


def _hg_block_pairs(tmp_ref, r0, bd_ones, fwd):
    io = lax.broadcasted_iota(jnp.int32, (HG_BLK, BRANCH_W), 0)
    q_b = _bf(tmp_ref[0, r0:r0 + HG_BLK, :])
    b_b = tmp_ref[3, r0:r0 + HG_BLK, :]
    rows = []
    for i in range(HG_BLK):
        c_s = jnp.broadcast_to(tmp_ref[1, r0 + i:r0 + i + 1, :], (HG_BLK, BRANCH_W))
        vis = (io >= i) if fwd else (io <= i)
        rows.append(q_b * jnp.exp(_bf(jnp.where(vis, b_b - c_s, NEG_BIG))))
    se = _dot(jnp.concatenate(rows, axis=0), bd_ones)
    out = None
    for i in range(HG_BLK):
        term = se[i * HG_BLK:(i + 1) * HG_BLK] * jnp.broadcast_to(tmp_ref[2, r0 + i:r0 + i + 1, :], (HG_BLK, BRANCH_W))
        out = term if out is None else out + term
    return out


def _hg_scan_kernel(qf_ref, ff_ref, vf_ref, qb_ref, fb_ref, vb_ref, lg_ref, bdo_ref, bdm_ref,
                    tli_ref, tls_ref, tui_ref, tus_ref, of_ref, ob_ref, st_ref, tmp_ref, *, layer):
    @pl.when(pl.program_id(1) == 0)
    def _():
        st_ref[...] = jnp.zeros_like(st_ref)

    n_layers = lg_ref.shape[0]
    mx = lg_ref[0:1, :]
    for i in range(1, n_layers):
        mx = jnp.maximum(mx, lg_ref[i:i + 1, :])
    den = jnp.zeros_like(mx)
    num = jnp.zeros_like(mx)
    for i in range(n_layers):
        e = jnp.exp(lg_ref[i:i + 1, :] - mx)
        den = den + e
        if 1 <= i <= layer:
            num = num + e
    lb = num / den
    bdo, bdm = bdo_ref[...], bdm_ref[...]
    n_blk = qf_ref.shape[0] // HG_BLK
    prep = (_hg_tile_prepare(qf_ref[...], ff_ref[...], vf_ref[...], lb, tli_ref[...], tus_ref[...], tmp_ref.at[0]),
            _hg_tile_prepare(qb_ref[...], fb_ref[...], vb_ref[...], lb, tui_ref[...], tls_ref[...], tmp_ref.at[1]))
    pairs = [[_hg_block_pairs(tmp_ref.at[d], i * HG_BLK, bdo, d == 0) for i in range(n_blk)] for d in range(2)]
    st = [st_ref[0], st_ref[1]]
    for step in range(n_blk):
        for d, o_ref in ((0, of_ref), (1, ob_ref)):
            i = step if d == 0 else n_blk - 1 - step
            r = slice(i * HG_BLK, (i + 1) * HG_BLK)
            q_t, k_t, v_b, bl = prep[d]
            o_ref[r, :] = pairs[d][i] + _dot_nt(q_t[r], _bf(st[d]))
            last = (i + 1) * HG_BLK - 1 if d == 0 else i * HG_BLK
            st[d] = st[d] * jnp.exp(bl[last:last + 1, :]) + _dot_tn(v_b[r], k_t[r]) * bdm
    st_ref[0], st_ref[1] = st


def _hgrn2(p_hg, lb_logits, layer, *, tm, tc):
    bsz, t_all, _ = p_hg.shape
    nc, ncc = t_all // tm, tc // tm
    consts = [jnp.asarray(_np_block_diag_ones(BRANCH_W, HEAD_DIM), BF16),
              jnp.asarray(_np_block_diag_ones(BRANCH_W, HEAD_DIM), F32),
              jnp.asarray(_np_tri(tm, HG_BLK, True, False), BF16),
              jnp.asarray(_np_tri(tm, HG_BLK, True, True), BF16),
              jnp.asarray(_np_tri(tm, HG_BLK, False, False), BF16),
              jnp.asarray(_np_tri(tm, HG_BLK, False, True), BF16)]
    blk = (None, tm, BRANCH_W)
    c2 = lambda b, s: (0, 0)
    fmap = lambda col: (lambda b, s: (b, s, col))
    bmap = lambda col: (lambda b, s: (b, _bwd_order(s, ncc, nc), col))
    return pl.pallas_call(
        functools.partial(_hg_scan_kernel, layer=layer),
        grid=(bsz, nc),
        in_specs=[pl.BlockSpec(blk, fmap(0)), pl.BlockSpec(blk, fmap(1)), pl.BlockSpec(blk, fmap(3)),
                  pl.BlockSpec(blk, bmap(0)), pl.BlockSpec(blk, bmap(2)), pl.BlockSpec(blk, bmap(3)),
                  pl.BlockSpec(lb_logits.shape, c2)] + [pl.BlockSpec(c.shape, c2) for c in consts],
        out_specs=[pl.BlockSpec(blk, fmap(0)), pl.BlockSpec(blk, bmap(0))],
        out_shape=[jax.ShapeDtypeStruct((bsz, t_all, BRANCH_W), F32)] * 2,
        scratch_shapes=[pltpu.VMEM((2, BRANCH_W, BRANCH_W), F32), pltpu.VMEM((2, 4, tm, BRANCH_W), F32)],
        compiler_params=pltpu.CompilerParams(dimension_semantics=("parallel", "arbitrary"),
                                             vmem_limit_bytes=40 * 1024 * 1024),
        name="hgrn2_scan",
    )(p_hg, p_hg, p_hg, p_hg, p_hg, p_hg, lb_logits.astype(F32), *consts)


def _s5_kernel(uf_ref, ub_ref, bb_ref, cc_ref, a_ref, yf_ref, yb_ref, x_ref, h_ref, *, tt, bsz):
    @pl.when(pl.program_id(0) == 0)
    def _():
        h_ref[...] = jnp.zeros_like(h_ref)

    x_ref[0] = _dot(_bf(uf_ref[...]), bb_ref[0])
    x_ref[1] = _dot(_bf(ub_ref[...]), bb_ref[1])

    def step(i, carry):
        hfr, hfi, hbr, hbi = carry
        rf = pl.multiple_of(i * bsz, bsz)
        rb = pl.multiple_of((tt - 1 - i) * bsz, bsz)
        ar, ai = a_ref[0, 0], a_ref[0, 1]
        xr = x_ref[0, pl.ds(rf, bsz), 0:S5_LANES]
        xi = x_ref[0, pl.ds(rf, bsz), S5_LANES:2 * S5_LANES]
        nfr = ar * hfr - ai * hfi + xr
        nfi = ar * hfi + ai * hfr + xi
        x_ref[0, pl.ds(rf, bsz), 0:S5_LANES] = nfr
        x_ref[0, pl.ds(rf, bsz), S5_LANES:2 * S5_LANES] = nfi
        ar, ai = a_ref[1, 0], a_ref[1, 1]
        xr = x_ref[1, pl.ds(rb, bsz), 0:S5_LANES]
        xi = x_ref[1, pl.ds(rb, bsz), S5_LANES:2 * S5_LANES]
        nbr = ar * hbr - ai * hbi + xr
        nbi = ar * hbi + ai * hbr + xi
        x_ref[1, pl.ds(rb, bsz), 0:S5_LANES] = nbr
        x_ref[1, pl.ds(rb, bsz), S5_LANES:2 * S5_LANES] = nbi
        return nfr, nfi, nbr, nbi

    hs = lax.fori_loop(0, tt, step, (h_ref[0, 0], h_ref[0, 1], h_ref[1, 0], h_ref[1, 1]))
    h_ref[0, 0], h_ref[0, 1], h_ref[1, 0], h_ref[1, 1] = hs
    yf_ref[...] = _dot(_bf(x_ref[0]), cc_ref[0])
    yb_ref[...] = _dot(_bf(x_ref[1]), cc_ref[1])


def _s5_matrices(lam_re, lam_im, log_step, b_re, b_im, c_re, c_im):
    lam_re, lam_im = lam_re.astype(F32), lam_im.astype(F32)
    step = jnp.exp(log_step.astype(F32))[..., None]
    mag = jnp.exp(lam_re * step)
    a_re, a_im = mag * jnp.cos(lam_im * step), mag * jnp.sin(lam_im * step)
    den = lam_re * lam_re + lam_im * lam_im
    n_re = a_re - 1.0
    coef_re = (n_re * lam_re + a_im * lam_im) / den
    coef_im = (a_im * lam_re - n_re * lam_im) / den
    b_re, b_im = b_re.astype(F32), b_im.astype(F32)
    bb_re = coef_re[..., None] * b_re - coef_im[..., None] * b_im
    bb_im = coef_re[..., None] * b_im + coef_im[..., None] * b_re
    eye_g = jnp.eye(S5_GROUPS, dtype=F32)
    bb = jnp.stack([bb_re, bb_im], axis=1)
    bb = jnp.einsum('drgph,gk->dghrkp', bb, eye_g).reshape(2, BRANCH_W, 2 * S5_LANES)
    cc = jnp.stack([c_re.astype(F32), -c_im.astype(F32)], axis=0)
    cc = jnp.einsum('rgkp,gj->rjpgk', cc, eye_g).reshape(2 * S5_LANES, BRANCH_W)
    return bb.astype(BF16), cc.astype(BF16), a_re.reshape(2, S5_LANES), a_im.reshape(2, S5_LANES)


def _s5(u_tb, mats, *, bsz, tc, tt):
    bb, cc, a_re, a_im = mats
    n_rows = u_tb.shape[0]
    t_all = n_rows // bsz
    nt, nct = t_all // tt, tc // tt
    a = jnp.broadcast_to(jnp.stack([a_re, a_im], axis=1)[:, :, None, :], (2, 2, bsz, S5_LANES))
    cc2 = jnp.broadcast_to(cc[None], (2,) + cc.shape)
    blk = (tt * bsz, BRANCH_W)
    fmap = lambda s: (s, 0)
    bmap = lambda s: (_bwd_order(s, nct, nt), 0)
    return pl.pallas_call(
        functools.partial(_s5_kernel, tt=tt, bsz=bsz),
        grid=(nt,),
        in_specs=[pl.BlockSpec(blk, fmap), pl.BlockSpec(blk, bmap),
                  pl.BlockSpec(bb.shape, lambda s: (0, 0, 0)), pl.BlockSpec(cc2.shape, lambda s: (0, 0, 0)),
                  pl.BlockSpec(a.shape, lambda s: (0, 0, 0, 0))],
        out_specs=[pl.BlockSpec(blk, fmap), pl.BlockSpec(blk, bmap)],
        out_shape=[jax.ShapeDtypeStruct((n_rows, BRANCH_W), F32)] * 2,
        scratch_shapes=[pltpu.VMEM((2, tt * bsz, 2 * S5_LANES), F32), pltpu.VMEM((2, 2, bsz, S5_LANES), F32)],
        compiler_params=pltpu.CompilerParams(dimension_semantics=("arbitrary",),
                                             vmem_limit_bytes=V7X_VMEM_LIMIT),
        name="s5_scan",
    )(u_tb, u_tb, bb, cc2, a)


def _rope(x, cos, sin_signed):
    w = x.shape[-1]
    lane = lax.broadcasted_iota(jnp.int32, x.shape, 1)
    up = pltpu.roll(x, w - 16, 1)
    dn = pltpu.roll(x, 16, 1)
    return x * cos + jnp.where((lane & 31) < 16, up, dn) * sin_signed


def _attn_kernel(q_ref, kv_ref, cq_ref, sq_ref, ck_ref, sk_ref, qg_ref, kg_ref, bdq_ref, bdk_ref,
                 o_ref, kp_ref, vp_ref, *, tm, tc, nct):
    j = pl.program_id(1)
    t_all = kv_ref.shape[0]
    half = BRANCH_W // 2

    @pl.when(j == 0)
    def _():
        def pairs(x):
            lane = lax.broadcasted_iota(jnp.int32, x.shape, 1)
            sw = pltpu.roll(x, HEAD_DIM, 1)
            return _bf(jnp.where(lane < HEAD_DIM, x, sw)), _bf(jnp.where(lane < HEAD_DIM, sw, x))

        for lo, hi, rope in ((0, tc, False), (tc, t_all, True)):
            k = _head_rms(kv_ref[lo:hi, 0:half], bdk_ref[...], kg_ref[...])
            if rope:
                k = _rope(k, ck_ref[...], sk_ref[...])
            kp_ref[0, lo:hi, :], kp_ref[1, lo:hi, :] = pairs(k)
            v = kv_ref[lo:hi, half:2 * half]
            lane = lax.broadcasted_iota(jnp.int32, v.shape, 1)
            sw = pltpu.roll(v, HEAD_DIM, 1)
            for g in range(2):
                v_lo, v_hi = (v, sw) if g == 0 else (sw, v)
                vp_ref[g, 0, lo:hi, :] = _bf(jnp.where(lane < HEAD_DIM, v_lo, 1.0))
                vp_ref[g, 1, lo:hi, :] = _bf(jnp.where(lane < HEAD_DIM, 1.0, v_hi))

    def attend(q, nk):
        lane = lax.broadcasted_iota(jnp.int32, (tm, half), 1)
        outs = []
        for g in range(2):
            qp = q[:, g * half:(g + 1) * half]
            acc = None
            for hh in range(2):
                sel = (lane < HEAD_DIM) if hh == 0 else (lane >= HEAD_DIM)
                s = _dot_nt(_bf(jnp.where(sel, qp, 0.0)), kp_ref[g, 0:nk, :])
                p = jnp.exp(_bf(s - jnp.max(s, axis=-1, keepdims=True)))
                ol = _dot(p, vp_ref[g, hh, 0:nk, :])
                res = ol / pltpu.roll(ol, HEAD_DIM, 1)
                acc = res if acc is None else jnp.where(sel, res, acc)
            outs.append(acc)
        o_ref[...] = jnp.concatenate(outs, axis=1)

    qn = _head_rms(q_ref[...], bdq_ref[...], qg_ref[...]) * (HEAD_DIM ** -0.5)

    @pl.when(j < nct)
    def _():
        attend(qn, tc)

    @pl.when(j >= nct)
    def _():
        attend(_rope(qn, cq_ref[...], sq_ref[...]), t_all)


def _rope_tables(n_tokens):
    rows = n_tokens // GRID_W
    r, col = np.meshgrid(np.arange(rows), np.arange(GRID_W), indexing='ij')
    axis_dim = HEAD_DIM // 2
    inv = jnp.asarray(ROPE_THETA, F32) ** (-jnp.arange(0, axis_dim, 2, dtype=F32) / axis_dim)
    ang_r = jnp.asarray(r.reshape(-1, 1), F32) * inv
    ang_c = jnp.asarray(col.reshape(-1, 1), F32) * inv
    ang = jnp.concatenate([ang_r, ang_r, ang_c, ang_c], axis=-1)
    sign = np.where((np.arange(HEAD_DIM) % 32) < 16, -1.0, 1.0).astype(np.float32)
    cos, sin = jnp.cos(ang), jnp.sin(ang) * sign
    return jnp.tile(cos, (1, N_HEADS)), jnp.tile(sin, (1, N_HEADS))


def _attention(p_q, p_kv, qn_g, kn_g, cos4, sin4, *, tm, tc):
    bsz, t_all, _ = p_q.shape
    nt, nct = t_all // tm, tc // tm
    half = BRANCH_W // 2
    tok = lambda b, j: (b, j, 0)
    c2 = lambda b, j: (0, 0)
    lat = lambda b, j: (jnp.maximum(j - nct, 0), 0)
    qg = jnp.tile(qn_g.astype(F32), N_HEADS).reshape(1, BRANCH_W)
    kg = jnp.tile(kn_g.astype(F32), 2).reshape(1, half)
    bdq = jnp.asarray(_np_block_diag_ones(BRANCH_W, HEAD_DIM), BF16)
    bdk = jnp.asarray(_np_block_diag_ones(half, HEAD_DIM), BF16)
    cosk, sink = cos4[:, 0:half], sin4[:, 0:half]
    return pl.pallas_call(
        functools.partial(_attn_kernel, tm=tm, tc=tc, nct=nct),
        grid=(bsz, nt),
        in_specs=[pl.BlockSpec((None, tm, BRANCH_W), tok),
                  pl.BlockSpec((None, t_all, BRANCH_W), lambda b, j: (b, 0, 0)),
                  pl.BlockSpec((tm, BRANCH_W), lat), pl.BlockSpec((tm, BRANCH_W), lat),
                  pl.BlockSpec(cosk.shape, c2), pl.BlockSpec(sink.shape, c2),
                  pl.BlockSpec((1, BRANCH_W), c2), pl.BlockSpec((1, half), c2),
                  pl.BlockSpec(bdq.shape, c2), pl.BlockSpec(bdk.shape, c2)],
        out_specs=pl.BlockSpec((None, tm, BRANCH_W), tok),
        out_shape=jax.ShapeDtypeStruct((bsz, t_all, BRANCH_W), F32),
        scratch_shapes=[pltpu.VMEM((2, t_all, half), BF16), pltpu.VMEM((2, 2, t_all, half), BF16)],
        compiler_params=pltpu.CompilerParams(dimension_semantics=("parallel", "arbitrary"),
                                             vmem_limit_bytes=V7X_VMEM_LIMIT),
        name="attention",
    )(p_q, p_kv, cos4, sin4, cosk, sink, qg, kg, bdq, bdk)


def _merge_kernel(x_ref, m_ref, g_ref, dnf_ref, dnb_ref, z_ref, sf_ref, sb_ref, u_ref, hf_ref, hb_ref, hgate_ref,
                  at_ref, dng_ref, hgg_ref, dsk_ref, glu_ref, bd_ref, wg_ref, wb_ref, wo_ref, o_ref):
    x = x_ref[...]
    bd = bd_ref[...]
    h = _bf(_rms(x) * g_ref[...] * (1.0 + m_ref[4:5, :]) + m_ref[3:4, :])
    y_dn = _head_rms(dnf_ref[...] + dnb_ref[...], bd, dng_ref[...]) * _silu(z_ref[...])
    ys = sf_ref[...] + sb_ref[...] + dsk_ref[...] * u_ref[...]
    ab = _dot(_bf(jax.nn.gelu(ys, approximate=True)), glu_ref[...])
    y_s5 = ab[:, 0:BRANCH_W] * _sigmoid(ab[:, BRANCH_W:2 * BRANCH_W])
    y_hg = _head_rms(hf_ref[...] + hb_ref[...], bd, hgg_ref[...]) * _sigmoid(hgate_ref[...])
    ys_all = (y_dn, y_s5, y_hg, at_ref[...])
    acc = jnp.zeros(x.shape, F32)
    for i in range(4):
        acc = acc + _sigmoid(_dot(h, wg_ref[i])) * _dot(_bf(ys_all[i]), wb_ref[i])
    o_ref[...] = x + m_ref[5:6, :] * _dot(_bf(acc), wo_ref[...])


def _merge(xj, mods, g, br, params, *, tm, nct, j0):
    bsz, t_all, _ = xj.shape
    n_tiles = t_all // tm - j0
    tok = lambda b, j: (b, j + j0, 0)
    col = lambda c: (lambda b, j: (b, j + j0, c))
    mod_map = lambda b, j: (jnp.where(j + j0 < nct, bsz, b), 0, 0)
    c2 = lambda b, j: (0, 0)
    c3 = lambda b, j: (0, 0, 0)
    tb = (None, tm, BRANCH_W)
    dn_g, hg_g, d_skip, w_glu, wg, wb, wo = params
    bd = jnp.asarray(_np_block_diag_ones(BRANCH_W, HEAD_DIM), BF16)
    return pl.pallas_call(
        _merge_kernel,
        grid=(bsz, n_tiles),
        in_specs=[pl.BlockSpec((None, tm, D_MODEL), tok),
                  pl.BlockSpec((None, N_MOD, D_MODEL), mod_map),
                  pl.BlockSpec((1, D_MODEL), c2),
                  pl.BlockSpec(tb, tok), pl.BlockSpec(tb, tok), pl.BlockSpec(tb, col(3)),
                  pl.BlockSpec(tb, tok), pl.BlockSpec(tb, tok), pl.BlockSpec(tb, tok),
                  pl.BlockSpec(tb, tok), pl.BlockSpec(tb, tok), pl.BlockSpec(tb, col(4)),
                  pl.BlockSpec(tb, tok),
                  pl.BlockSpec((1, BRANCH_W), c2), pl.BlockSpec((1, BRANCH_W), c2), pl.BlockSpec((1, BRANCH_W), c2),
                  pl.BlockSpec(w_glu.shape, c2), pl.BlockSpec(bd.shape, c2),
                  pl.BlockSpec(wg.shape, c3), pl.BlockSpec(wb.shape, c3), pl.BlockSpec(wo.shape, c2)],
        out_specs=pl.BlockSpec((None, tm, D_MODEL), lambda b, j: (b, j, 0)),
        out_shape=jax.ShapeDtypeStruct((bsz, n_tiles * tm, D_MODEL), F32),
        compiler_params=pltpu.CompilerParams(dimension_semantics=("parallel", "parallel"),
                                             vmem_limit_bytes=V7X_VMEM_LIMIT),
        name="branch_merge",
    )(xj, mods, g.reshape(1, D_MODEL), br['dn_f'], br['dn_b'], br['p_dn'], br['s5_f'], br['s5_b'], br['p_s5'],
      br['hg_f'], br['hg_b'], br['p_hg'], br['at'], dn_g, hg_g, d_skip, w_glu, bd, wg, wb, wo)


def kernel(x, c, ctx, c_ctx, ada_w, ada_b, norm_g, ffn_w1, ffn_w3, ffn_w2, w_in, dn_conv, dn_a_log, dn_dt_bias,
           dn_norm_g, s5_lam_re, s5_lam_im, s5_log_step, s5_b_re, s5_b_im, s5_c_re, s5_c_im, s5_d, s5_glu,
           hg_lb_logits, hg_norm_g, at_qn_g, at_kn_g, w_branch, w_out, final_g):
    bsz, tl, _ = x.shape
    tc = ctx.shape[1]
    t_all = tc + tl
    depth = ada_w.shape[0]
    tm = 256 if (tc % 256 == 0 and tl % 256 == 0) else 128
    assert tc % tm == 0 and tl % tm == 0 and tm % CHUNK == 0 and bsz % 8 == 0
    nct = tc // tm
    d_ff = ffn_w1.shape[-1]
    n_ff = d_ff // FF_CHUNK

    n_rows = -(-(bsz + 1) // 8) * 8
    cond = jnp.zeros((n_rows, D_MODEL), F32).at[0:bsz].set(c.astype(F32)).at[bsz].set(c_ctx.astype(F32))
    mods_all = _ada_mods(cond, ada_w.astype(F32), ada_b.astype(F32))
    cos4, sin4 = _rope_tables(tl)
    xj = jnp.concatenate([ctx, x], axis=1).astype(F32)

    def ffn_weights(l, i):
        w1 = ffn_w1[l, i].astype(BF16).reshape(D_MODEL, n_ff, FF_CHUNK).transpose(1, 0, 2)
        w3 = ffn_w3[l, i].astype(BF16).reshape(D_MODEL, n_ff, FF_CHUNK).transpose(1, 0, 2)
        w2 = ffn_w2[l, i].astype(BF16)
        return w1, w3, w2

    for l in range(depth):
        last = l == depth - 1
        mods = mods_all[l]
        xj = _ffn_half_step(xj, mods, norm_g[l, 0], *ffn_weights(l, 0), mod_base=0, tm=tm, nct=nct)

        wl = w_in[l]
        w_dn = jnp.concatenate([wl[:, 0:1040], jnp.zeros((D_MODEL, 112), wl.dtype)], axis=1)
        ws = [w.astype(BF16) for w in (w_dn, wl[:, 1040:1296], wl[:, 1296:2576], wl[:, 2576:2832], wl[:, 2832:3088])]
        p_dn, p_s5, p_hg, p_aq, p_kv = _input_projection(xj, mods, norm_g[l, 1], ws, tm=tm, nct=nct)

        dn_f, dn_b = _deltanet(p_dn, dn_conv[l], dn_a_log[l], dn_dt_bias[l], tm=tm, tc=tc)
        hg_f, hg_b = _hgrn2(p_hg, hg_lb_logits, l, tm=tm, tc=tc)
        mats = _s5_matrices(s5_lam_re[l], s5_lam_im[l], s5_log_step[l], s5_b_re[l], s5_b_im[l], s5_c_re[l], s5_c_im[l])
        u_tb = p_s5.transpose(1, 0, 2).reshape(t_all * bsz, BRANCH_W)
        s5_f, s5_b = _s5(u_tb, mats, bsz=bsz, tc=tc, tt=CHUNK)
        s5_f = s5_f.reshape(t_all, bsz, BRANCH_W).transpose(1, 0, 2)
        s5_b = s5_b.reshape(t_all, bsz, BRANCH_W).transpose(1, 0, 2)
        y_at = _attention(p_aq, p_kv, at_qn_g[l], at_kn_g[l], cos4, sin4, tm=tm, tc=tc)

        br = dict(dn_f=dn_f, dn_b=dn_b, p_dn=p_dn, s5_f=s5_f, s5_b=s5_b, p_s5=p_s5, hg_f=hg_f, hg_b=hg_b,
                  p_hg=p_hg, at=y_at)
        params = (jnp.tile(dn_norm_g[l].astype(F32), N_HEADS).reshape(1, BRANCH_W),
                  jnp.tile(hg_norm_g[l].astype(F32), N_HEADS).reshape(1, BRANCH_W),
                  s5_d[l].astype(F32).reshape(1, BRANCH_W),
                  s5_glu[l].astype(BF16),
                  wl[:, 3088:].astype(BF16).reshape(D_MODEL, 4, D_MODEL).transpose(1, 0, 2),
                  w_branch[l].astype(BF16),
                  w_out[l].astype(BF16))
        xj = _merge(xj, mods, norm_g[l, 1], br, params, tm=tm, nct=nct, j0=nct if last else 0)
        xj = _ffn_half_step(xj, mods, norm_g[l, 2], *ffn_weights(l, 1), mod_base=6, tm=tm,
                            nct=0 if last else nct, final_g=final_g.astype(F32) if last else None)
    return xj.astype(x.dtype)
```

```python
import functools
import math

import numpy as np
import jax
import jax.numpy as jnp
from jax import lax
from jax.experimental import pallas as pl
from jax.experimental.pallas import tpu as pltpu

F32 = jnp.float32
BF16 = jnp.bfloat16

D_MODEL = 1024
N_MOD = 9
BRANCH_W = 256
HEAD_DIM = 64
N_HEADS = 4
CHUNK = 64
HG_BLK = 16
EPS = 1e-6
LOG_FLOOR = 1e-30
GRID_W = 64
ROPE_THETA = 10000.0
FF_CHUNK = 256
S5_GROUPS = 16
S5_GROUP = 16
S5_STATE = 64
S5_LANES = S5_GROUPS * S5_STATE
NEG_BIG = -1e30
V7X_VMEM_LIMIT = 56 * 1024 * 1024


def _bf(x):
    return x.astype(BF16)


def _dot(a, b):
    return jnp.dot(a, b, preferred_element_type=F32)


def _dot_nt(a, b):
    return lax.dot_general(a, b, (((1,), (1,)), ((), ())), preferred_element_type=F32)


def _dot_tn(a, b):
    return lax.dot_general(a, b, (((0,), (0,)), ((), ())), preferred_element_type=F32)


def _split2(x):
    hi = _bf(x)
    lo = _bf(x - hi.astype(F32))
    return hi, lo


def _split3(x):
    x1 = _bf(x)
    r1 = x - x1.astype(F32)
    x2 = _bf(r1)
    x3 = _bf(r1 - x2.astype(F32))
    return x1, x2, x3


def _dot_x2(x, m):
    hi, lo = _split2(x)
    return _dot(hi, m) + _dot(lo, m)


def _dot_x3(x, m):
    x1, x2, x3 = _split3(x)
    return _dot(x1, m) + _dot(x2, m) + _dot(x3, m)


def _dot3_left(m, x):
    x1, x2, x3 = _split3(x)
    return _dot(m, x1) + _dot(m, x2) + _dot(m, x3)


def _dot_hp(a, b):
    a_hi, a_lo = _split2(a)
    b_hi, b_lo = _split2(b)
    return _dot(a_hi, b_hi) + _dot(a_hi, b_lo) + _dot(a_lo, b_hi)


def _sigmoid(x):
    return jax.nn.sigmoid(x)


def _silu(x):
    return x * jax.nn.sigmoid(x)


def _rms(x):
    return x * lax.rsqrt(jnp.mean(x * x, axis=-1, keepdims=True) + EPS)


def _head_rms(x, bd_ones, gain):
    ss = _dot_x2(x * x, bd_ones) * (1.0 / HEAD_DIM)
    return x * lax.rsqrt(ss + EPS) * gain


def _stack4(x):
    return jnp.concatenate([x, x, x, x], axis=0)


def _unstack4(x):
    n = x.shape[0] // 4
    return x[0:n] + x[n:2 * n] + x[2 * n:3 * n] + x[3 * n:4 * n]


def _np_block_diag_ones(n, blk):
    i = np.arange(n)
    return (i[:, None] // blk == i[None, :] // blk).astype(np.float32)


def _np_tri(n, blk, lower, strict):
    i = np.arange(n)
    same = i[:, None] // blk == i[None, :] // blk
    a, b = i[:, None] % blk, i[None, :] % blk
    if lower:
        t = (a > b) if strict else (a >= b)
    else:
        t = (a < b) if strict else (a <= b)
    return (same & t).astype(np.float32)


def _np_doubling_masks(n, blk):
    i = np.arange(n)
    r, c = i[:, None], i[None, :]
    out = [(r // 2 == c // 2)]
    s = 2
    while s < blk:
        out.append((r // (2 * s) == c // (2 * s)) & (r // s != c // s))
        s *= 2
    return np.stack(out).astype(np.float32)


def _bwd_order(s, n_ctx, n_all):
    return jnp.where(s < n_ctx, n_ctx - 1 - s, n_all + n_ctx - 1 - s)


def _ada_kernel(c_ref, w_ref, b_ref, o_ref):
    s = _silu(c_ref[...])
    w = w_ref[...]
    s_hi, s_lo = _split2(s)
    w_hi, w_lo = _split2(w)
    o_ref[...] = _dot(s_hi, w_hi) + _dot(s_hi, w_lo) + _dot(s_lo, w_hi) + b_ref[...]


def _ada_mods(cond, ada_w, ada_b):
    n_layers = ada_w.shape[0]
    r = cond.shape[0]
    n_out = ada_w.shape[2]
    tn = 1152
    out = pl.pallas_call(
        _ada_kernel,
        grid=(n_layers, n_out // tn),
        in_specs=[
            pl.BlockSpec((r, D_MODEL), lambda l, n: (0, 0)),
            pl.BlockSpec((None, D_MODEL, tn), lambda l, n: (l, 0, n)),
            pl.BlockSpec((None, 1, tn), lambda l, n: (l, 0, n)),
        ],
        out_specs=pl.BlockSpec((None, r, tn), lambda l, n: (l, 0, n)),
        out_shape=jax.ShapeDtypeStruct((n_layers, r, n_out), F32),
        compiler_params=pltpu.CompilerParams(dimension_semantics=("parallel", "parallel"),
                                             vmem_limit_bytes=40 * 1024 * 1024),
        name="ada_mods",
    )(cond, ada_w, ada_b.reshape(n_layers, 1, n_out))
    return out.reshape(n_layers, r, N_MOD, D_MODEL)


def _ffn_kernel(*refs, mod_base, final, n_ctx_tiles):
    fg_ref = c_ref = None
    if final:
        x_ref, m_ref, g_ref, w1_ref, w3_ref, w2_ref, fg_ref, o_ref = refs
    elif n_ctx_tiles:
        x_ref, c_ref, m_ref, g_ref, w1_ref, w3_ref, w2_ref, o_ref = refs
    else:
        x_ref, m_ref, g_ref, w1_ref, w3_ref, w2_ref, o_ref = refs
    x = x_ref[...]
    if c_ref is not None:
        x = jnp.where(pl.program_id(1) < n_ctx_tiles, c_ref[...], x)
    shift = m_ref[mod_base:mod_base + 1, :]
    scale = m_ref[mod_base + 1:mod_base + 2, :]
    gate = m_ref[mod_base + 2:mod_base + 3, :]
    h = _bf(_rms(x) * g_ref[...] * (1.0 + scale) + shift)
    us = []
    for c in range(w1_ref.shape[0]):
        a = _dot(h, w1_ref[c])
        us.append(_bf(_silu(a) * _dot(h, w3_ref[c])))
    out = x + 0.5 * gate * _dot(jnp.concatenate(us, axis=1), w2_ref[...])
    if final:
        out = _rms(out) * fg_ref[...]
    o_ref[...] = out


def _ffn_half_step(xj, mods, g, w1c, w3c, w2c, *, mod_base, tm, nct, final_g=None, ctx=None):
    bsz, t_all, _ = xj.shape
    final = final_g is not None
    if ctx is not None:
        t_all += ctx.shape[1]

    def tok_map(b, j):
        return (b, j, 0)

    def mod_map(b, j):
        return (jnp.where(j < nct, bsz, b), 0, 0)

    const3 = lambda b, j: (0, 0, 0)
    const2 = lambda b, j: (0, 0)
    if ctx is None:
        tok_specs, tok_args = [pl.BlockSpec((None, tm, D_MODEL), tok_map)], [xj]
    else:
        tok_specs = [pl.BlockSpec((None, tm, D_MODEL), lambda b, j: (b, jnp.maximum(j - nct, 0), 0)),
                     pl.BlockSpec((None, tm, D_MODEL), lambda b, j: (b, jnp.minimum(j, nct - 1), 0))]
        tok_args = [xj, ctx]
    in_specs = tok_specs + [
        pl.BlockSpec((None, N_MOD, D_MODEL), mod_map),
        pl.BlockSpec((1, D_MODEL), const2),
        pl.BlockSpec(w1c.shape, const3),
        pl.BlockSpec(w3c.shape, const3),
        pl.BlockSpec(w2c.shape, const2),
    ]
    args = tok_args + [mods, g.reshape(1, D_MODEL), w1c, w3c, w2c]
    if final:
        in_specs.append(pl.BlockSpec((1, D_MODEL), const2))
        args.append(final_g.reshape(1, D_MODEL))
    return pl.pallas_call(
        functools.partial(_ffn_kernel, mod_base=mod_base, final=final, n_ctx_tiles=nct if ctx is not None else 0),
        grid=(bsz, t_all // tm),
        in_specs=in_specs,
        out_specs=pl.BlockSpec((None, tm, D_MODEL), tok_map),
        out_shape=jax.ShapeDtypeStruct((bsz, t_all, D_MODEL), F32),
        compiler_params=pltpu.CompilerParams(dimension_semantics=("parallel", "parallel"),
                                             vmem_limit_bytes=V7X_VMEM_LIMIT),
        name="ffn_half_step",
    )(*args)


def _inproj_kernel(x_ref, m_ref, g_ref, wdn_ref, ws5_ref, whg_ref, waq_ref, wkv_ref,
                   dn_ref, s5_ref, hg_ref, aq_ref, kv_ref):
    x = x_ref[...]
    h = _bf(_rms(x) * g_ref[...] * (1.0 + m_ref[4:5, :]) + m_ref[3:4, :])
    dn_ref[...] = _dot(h, wdn_ref[...])
    s5_ref[...] = _dot(h, ws5_ref[...])
    hg_ref[...] = _dot(h, whg_ref[...])
    aq_ref[...] = _dot(h, waq_ref[...])
    kv_ref[...] = _dot(h, wkv_ref[...])


def _input_projection(xj, mods, g, ws, *, tm, nct):
    bsz, t_all, _ = xj.shape
    tok_map = lambda b, j: (b, j, 0)
    mod_map = lambda b, j: (jnp.where(j < nct, bsz, b), 0, 0)
    const2 = lambda b, j: (0, 0)
    widths = [w.shape[1] for w in ws]
    return pl.pallas_call(
        _inproj_kernel,
        grid=(bsz, t_all // tm),
        in_specs=[pl.BlockSpec((None, tm, D_MODEL), tok_map),
                  pl.BlockSpec((None, N_MOD, D_MODEL), mod_map),
                  pl.BlockSpec((1, D_MODEL), const2)]
                 + [pl.BlockSpec(w.shape, const2) for w in ws],
        out_specs=[pl.BlockSpec((tm, n), lambda b, j: (j, b)) if i == 1 else pl.BlockSpec((None, tm, n), tok_map)
                   for i, n in enumerate(widths)],
        out_shape=[jax.ShapeDtypeStruct((t_all, bsz * n) if i == 1 else (bsz, t_all, n), F32)
                   for i, n in enumerate(widths)],
        compiler_params=pltpu.CompilerParams(dimension_semantics=("parallel", "parallel"),
                                             vmem_limit_bytes=V7X_VMEM_LIMIT),
        name="input_projection",
    )(xj, mods, g.reshape(1, D_MODEL), *ws)


def _dn_prep_kernel(p_ref, prev_ref, next_ref, cw_ref, al_ref, dtb_ref, e_ref, tril_ref, triu_ref, bd_ref,
                    q_ref, k_ref, v_ref, gb_ref, *, tm, nct, nt):
    j = pl.program_id(1)
    first = jnp.logical_or(j == 0, j == nct)
    last = jnp.logical_or(j == nct - 1, j == nt - 1)
    x = p_ref[:, 0:3 * BRANCH_W]
    pv = prev_ref[...] * jnp.where(first, 0.0, 1.0)
    nx = next_ref[...] * jnp.where(last, 0.0, 1.0)
    ext = jnp.concatenate([pv, x, nx], axis=0)
    n = tm + 16
    acc = jnp.zeros((tm, 3 * BRANCH_W), F32)
    for t in range(5):
        sh = (2 - t) % n
        r = ext if sh == 0 else pltpu.roll(ext, sh, 0)
        acc = acc + r[8:8 + tm] * cw_ref[t:t + 1, :]
    h = _silu(acc)
    bd = bd_ref[...]

    def l2n(t):
        return t * lax.rsqrt(_dot_x2(t * t, bd) + EPS)

    q_ref[...] = l2n(h[:, 0:BRANCH_W]) * (HEAD_DIM ** -0.5)
    k_ref[...] = l2n(h[:, BRANCH_W:2 * BRANCH_W])
    v_ref[...] = h[:, 2 * BRANCH_W:3 * BRANCH_W]

    ab = p_ref[:, 4 * BRANCH_W:4 * BRANCH_W + 128]
    z = ab + dtb_ref[...]
    softplus = jnp.maximum(z, 0.0) + jnp.log1p(jnp.exp(-jnp.abs(z)))
    g = -jnp.exp(al_ref[...]) * softplus
    lane = lax.broadcasted_iota(jnp.int32, ab.shape, 1)
    g = jnp.where(lane < 8, g, 0.0)
    cs_f = _dot3_left(tril_ref[...], g)
    cs_b = _dot3_left(triu_ref[...], g)
    comp = jnp.where(lane < 4, cs_f, jnp.where(lane < 8, cs_b, jnp.where(lane < 16, _sigmoid(ab), 0.0)))
    gb_ref[...] = _dot_x3(comp, e_ref[...])


def _dn_chunks_local(items, bd, it, lvl_ref):
    def stk(x):
        return _stack4(_bf(x)) * bd

    ks = [stk(c[1]) for c in items]
    gq = [_dot_nt(_bf(jnp.concatenate([c[1] * c[4], c[0]], axis=0)), x) for c, x in zip(items, ks)]
    decays, a_s = [], []
    for (q, k, v, gc, be, neg_incl, tri_strict, fwd), g in zip(items, gq):
        col_gc = jnp.sum(gc * it, axis=0, keepdims=True)
        decay = jnp.exp(gc - col_gc + neg_incl)
        decays.append(decay)
        a_s.append(g[0:CHUNK] * decay * tri_strict)
    ts = [it - a * lvl_ref[0] for a in a_s]
    for lv in range(1, lvl_ref.shape[0]):
        xs = [_dot(_bf(a * lvl_ref[lv]), stk(t)) for a, t in zip(a_s, ts)]
        ts = [t - _dot(_bf(t), stk(x)) for t, x in zip(ts, xs)]
    egcs = [jnp.exp(c[3]) for c in items]
    uws = [_dot(_bf(t), jnp.concatenate([stk(c[2] * c[4]), stk(c[1] * c[4] * e)], axis=1))
           for t, c, e in zip(ts, items, egcs)]
    out = []
    for (q, k, v, gc, be, _, _, fwd), uw, g, dc, egc in zip(items, uws, gq, decays, egcs):
        total = gc[CHUNK - 1:CHUNK, :] if fwd else gc[0:1, :]
        out.append((uw[:, 0:BRANCH_W], _bf(uw[:, BRANCH_W:2 * BRANCH_W]), _bf(g[CHUNK:2 * CHUNK] * dc),
                    _bf(q * egc), _bf(k * jnp.exp(total - gc)), jnp.exp(total)))
    return out


def _dn_chunk_state(local, s_prev, bd, bd_f32):
    u, w, attn, q_dec, k_dec, g_last = local
    ws = _dot(jnp.concatenate([w, q_dec], axis=0), _bf(s_prev))
    v_new = _bf(u - ws[0:CHUNK])
    o = ws[CHUNK:2 * CHUNK] + _dot(attn, _stack4(v_new) * bd)
    s_new = s_prev * g_last + _dot_tn(k_dec, v_new) * bd_f32
    return o, s_new


def _dn_scan_kernel(qf_ref, kf_ref, vf_ref, gcf_ref, bef_ref, qb_ref, kb_ref, vb_ref, gcb_ref, beb_ref,
                    bdf_ref, li_ref, ls_ref, ui_ref, us_ref, it_ref, lvl_ref, bd_ref,
                    of_ref, ob_ref, s_ref):
    @pl.when(pl.program_id(1) == 0)
    def _():
        s_ref[...] = jnp.zeros_like(s_ref)

    bd, bd_f32, it = bd_ref[...], bdf_ref[...], it_ref[...]
    n_sub = qf_ref.shape[0] // CHUNK
    dirs = ((qf_ref, kf_ref, vf_ref, gcf_ref, bef_ref, li_ref, ls_ref, of_ref, True),
            (qb_ref, kb_ref, vb_ref, gcb_ref, beb_ref, ui_ref, us_ref, ob_ref, False))
    items = []
    for q_ref, k_ref, v_ref, gc_ref, be_ref, ti_ref, ts_ref, _, fwd in dirs:
        for g in range(n_sub):
            r = slice(g * CHUNK, (g + 1) * CHUNK)
            items.append((q_ref[r, :], k_ref[r, :], v_ref[r, :], gc_ref[r, :], be_ref[r, :],
                          ti_ref[...], ts_ref[...], fwd))
    local = _dn_chunks_local(items, bd, it, lvl_ref)
    s_f, s_b = s_ref[0], s_ref[1]
    for i in range(n_sub):
        gb = n_sub - 1 - i
        o, s_f = _dn_chunk_state(local[i], s_f, bd, bd_f32)
        of_ref[i * CHUNK:(i + 1) * CHUNK, :] = o
        o, s_b = _dn_chunk_state(local[n_sub + gb], s_b, bd, bd_f32)
        ob_ref[gb * CHUNK:(gb + 1) * CHUNK, :] = o
    s_ref[0] = s_f
    s_ref[1] = s_b


def _deltanet(p_dn, conv_w, a_log, dt_bias, *, tm, tc):
    bsz, t_all, _ = p_dn.shape
    nt, nct = t_all // tm, tc // tm
    al = jnp.zeros((1, 128), F32).at[0, 0:8].set(a_log.reshape(8).astype(F32))
    dtb = jnp.zeros((1, 128), F32).at[0, 0:8].set(dt_bias.reshape(8).astype(F32))
    e = np.zeros((128, 4 * BRANCH_W), np.float32)
    for jj in range(4):
        for h in range(N_HEADS):
            e[4 * jj + h, jj * BRANCH_W + h * HEAD_DIM: jj * BRANCH_W + (h + 1) * HEAD_DIM] = 1.0
    consts = [jnp.asarray(e, BF16),
              jnp.asarray(_np_tri(tm, CHUNK, True, False), BF16),
              jnp.asarray(_np_tri(tm, CHUNK, False, False), BF16),
              jnp.asarray(_np_block_diag_ones(BRANCH_W, HEAD_DIM), BF16)]
    tok = lambda b, j: (b, j, 0)
    c2 = lambda b, j: (0, 0)
    hb = tm // 8
    q, k, v, gb = pl.pallas_call(
        functools.partial(_dn_prep_kernel, tm=tm, nct=nct, nt=nt),
        grid=(bsz, nt),
        in_specs=[pl.BlockSpec((None, tm, p_dn.shape[2]), tok),
                  pl.BlockSpec((None, 8, 3 * BRANCH_W), lambda b, j: (b, jnp.maximum(j * hb - 1, 0), 0)),
                  pl.BlockSpec((None, 8, 3 * BRANCH_W), lambda b, j: (b, jnp.minimum((j + 1) * hb, nt * hb - 1), 0)),
                  pl.BlockSpec(conv_w.shape, c2), pl.BlockSpec((1, 128), c2), pl.BlockSpec((1, 128), c2)]
                 + [pl.BlockSpec(c.shape, c2) for c in consts],
        out_specs=[pl.BlockSpec((None, tm, BRANCH_W), tok)] * 3 + [pl.BlockSpec((None, tm, 4 * BRANCH_W), tok)],
        out_shape=[jax.ShapeDtypeStruct((bsz, t_all, BRANCH_W), F32)] * 3
                  + [jax.ShapeDtypeStruct((bsz, t_all, 4 * BRANCH_W), F32)],
        compiler_params=pltpu.CompilerParams(dimension_semantics=("parallel", "parallel"),
                                             vmem_limit_bytes=40 * 1024 * 1024),
        name="deltanet_prep",
    )(p_dn, p_dn, p_dn, conv_w.astype(F32), al, dtb, *consts)

    nc, ncc = nt, nct
    n4 = N_HEADS * CHUNK
    neg = lambda tri: (1.0 - tri) * NEG_BIG
    per_head = lambda m: np.tile(m, (1,) * (m.ndim - 1) + (N_HEADS,))
    masks = [jnp.asarray(_np_block_diag_ones(n4, CHUNK), F32),
             jnp.asarray(per_head(neg(_np_tri(CHUNK, CHUNK, True, False))), F32),
             jnp.asarray(per_head(_np_tri(CHUNK, CHUNK, True, True)), F32),
             jnp.asarray(per_head(neg(_np_tri(CHUNK, CHUNK, False, False))), F32),
             jnp.asarray(per_head(_np_tri(CHUNK, CHUNK, False, True)), F32),
             jnp.asarray(per_head(np.eye(CHUNK, dtype=np.float32)), F32),
             jnp.asarray(per_head(_np_doubling_masks(CHUNK, CHUNK)), F32),
             jnp.asarray(_np_block_diag_ones(n4, CHUNK), BF16)]
    cmap = lambda m: (lambda b, s: (0,) * m.ndim)
    blk = (None, tm, BRANCH_W)
    fmap = lambda col: (lambda b, s: (b, s, col))
    bmap = lambda col: (lambda b, s: (b, _bwd_order(s, ncc, nc), col))
    o_f, o_b = pl.pallas_call(
        _dn_scan_kernel,
        grid=(bsz, nc),
        in_specs=[pl.BlockSpec(blk, fmap(0))] * 3 + [pl.BlockSpec(blk, fmap(0)), pl.BlockSpec(blk, fmap(2))]
                 + [pl.BlockSpec(blk, bmap(0))] * 3 + [pl.BlockSpec(blk, bmap(1)), pl.BlockSpec(blk, bmap(3))]
                 + [pl.BlockSpec(m.shape, cmap(m)) for m in masks],
        out_specs=[pl.BlockSpec(blk, fmap(0)), pl.BlockSpec(blk, bmap(0))],
        out_shape=[jax.ShapeDtypeStruct((bsz, t_all, BRANCH_W), F32)] * 2,
        scratch_shapes=[pltpu.VMEM((2, n4, n4), F32)],
        compiler_params=pltpu.CompilerParams(dimension_semantics=("parallel", "arbitrary"),
                                             vmem_limit_bytes=40 * 1024 * 1024),
        name="deltanet_scan",
    )(q, k, v, gb, gb, q, k, v, gb, gb, *masks)
    return o_f, o_b


def _hg_tile_prepare(q, f, v, lb, tri_incl, tri_excl, tmp_ref):
    qs = _silu(q)
    logf = jnp.log(jnp.maximum(lb + (1.0 - lb) * _sigmoid(f), LOG_FLOOR))
    kk = (1.0 - lb) * _sigmoid(-f)
    bl = _dot3_left(tri_incl, logf)
    suf = _dot3_left(tri_excl, logf)
    tmp_ref[0], tmp_ref[1], tmp_ref[2], tmp_ref[3] = qs, bl - jnp.log(kk), v, bl
    return _bf(qs * jnp.exp(bl)), _bf(kk * jnp.exp(suf)), _bf(v), bl


def _hg_block_pairs(tmp_ref, r0, bd_ones, fwd):
    io = lax.broadcasted_iota(jnp.int32, (HG_BLK, BRANCH_W), 0)
    q_b = _bf(tmp_ref[0, r0:r0 + HG_BLK, :])
    b_b = tmp_ref[3, r0:r0 + HG_BLK, :]
    rows = []
    for i in range(HG_BLK):
        c_s = jnp.broadcast_to(tmp_ref[1, r0 + i:r0 + i + 1, :], (HG_BLK, BRANCH_W))
        vis = (io >= i) if fwd else (io <= i)
        rows.append(q_b * jnp.exp(_bf(jnp.where(vis, b_b - c_s, NEG_BIG))))
    se = _dot(jnp.concatenate(rows, axis=0), bd_ones)
    out = None
    for i in range(HG_BLK):
        term = se[i * HG_BLK:(i + 1) * HG_BLK] * jnp.broadcast_to(tmp_ref[2, r0 + i:r0 + i + 1, :], (HG_BLK, BRANCH_W))
        out = term if out is None else out + term
    return out


def _hg_scan_kernel(qf_ref, ff_ref, vf_ref, qb_ref, fb_ref, vb_ref, lg_ref, bdo_ref, bdm_ref,
                    tli_ref, tls_ref, tui_ref, tus_ref, of_ref, ob_ref, st_ref, tmp_ref, *, layer):
    @pl.when(pl.program_id(1) == 0)
    def _():
        st_ref[...] = jnp.zeros_like(st_ref)

    n_layers = lg_ref.shape[0]
    mx = lg_ref[0:1, :]
    for i in range(1, n_layers):
        mx = jnp.maximum(mx, lg_ref[i:i + 1, :])
    den = jnp.zeros_like(mx)
    num = jnp.zeros_like(mx)
    for i in range(n_layers):
        e = jnp.exp(lg_ref[i:i + 1, :] - mx)
        den = den + e
        if 1 <= i <= layer:
            num = num + e
    lb = num / den
    bdo, bdm = bdo_ref[...], bdm_ref[...]
    n_blk = qf_ref.shape[0] // HG_BLK
    prep = (_hg_tile_prepare(qf_ref[...], ff_ref[...], vf_ref[...], lb, tli_ref[...], tus_ref[...], tmp_ref.at[0]),
            _hg_tile_prepare(qb_ref[...], fb_ref[...], vb_ref[...], lb, tui_ref[...], tls_ref[...], tmp_ref.at[1]))
    pairs = [[_hg_block_pairs(tmp_ref.at[d], i * HG_BLK, bdo, d == 0) for i in range(n_blk)] for d in range(2)]
    st = [st_ref[0], st_ref[1]]
    for step in range(n_blk):
        for d, o_ref in ((0, of_ref), (1, ob_ref)):
            i = step if d == 0 else n_blk - 1 - step
            r = slice(i * HG_BLK, (i + 1) * HG_BLK)
            q_t, k_t, v_b, bl = prep[d]
            o_ref[r, :] = pairs[d][i] + _dot_nt(q_t[r], _bf(st[d]))
            last = (i + 1) * HG_BLK - 1 if d == 0 else i * HG_BLK
            st[d] = st[d] * jnp.exp(bl[last:last + 1, :]) + _dot_tn(v_b[r], k_t[r]) * bdm
    st_ref[0], st_ref[1] = st


def _hgrn2(p_hg, lb_logits, layer, *, tm, tc):
    bsz, t_all, _ = p_hg.shape
    nc, ncc = t_all // tm, tc // tm
    consts = [jnp.asarray(_np_block_diag_ones(BRANCH_W, HEAD_DIM), BF16),
              jnp.asarray(_np_block_diag_ones(BRANCH_W, HEAD_DIM), F32),
              jnp.asarray(_np_tri(tm, HG_BLK, True, False), BF16),
              jnp.asarray(_np_tri(tm, HG_BLK, True, True), BF16),
              jnp.asarray(_np_tri(tm, HG_BLK, False, False), BF16),
              jnp.asarray(_np_tri(tm, HG_BLK, False, True), BF16)]
    blk = (None, tm, BRANCH_W)
    c2 = lambda b, s: (0, 0)
    fmap = lambda col: (lambda b, s: (b, s, col))
    bmap = lambda col: (lambda b, s: (b, _bwd_order(s, ncc, nc), col))
    return pl.pallas_call(
        functools.partial(_hg_scan_kernel, layer=layer),
        grid=(bsz, nc),
        in_specs=[pl.BlockSpec(blk, fmap(0)), pl.BlockSpec(blk, fmap(1)), pl.BlockSpec(blk, fmap(3)),
                  pl.BlockSpec(blk, bmap(0)), pl.BlockSpec(blk, bmap(2)), pl.BlockSpec(blk, bmap(3)),
                  pl.BlockSpec(lb_logits.shape, c2)] + [pl.BlockSpec(c.shape, c2) for c in consts],
        out_specs=[pl.BlockSpec(blk, fmap(0)), pl.BlockSpec(blk, bmap(0))],
        out_shape=[jax.ShapeDtypeStruct((bsz, t_all, BRANCH_W), F32)] * 2,
        scratch_shapes=[pltpu.VMEM((2, BRANCH_W, BRANCH_W), F32), pltpu.VMEM((2, 4, tm, BRANCH_W), F32)],
        compiler_params=pltpu.CompilerParams(dimension_semantics=("parallel", "arbitrary"),
                                             vmem_limit_bytes=40 * 1024 * 1024),
        name="hgrn2_scan",
    )(p_hg, p_hg, p_hg, p_hg, p_hg, p_hg, lb_logits.astype(F32), *consts)


def _s5_kernel(uf_ref, ub_ref, bb_ref, cc_ref, a_ref, yf_ref, yb_ref, x_ref, h_ref, *, tt, bsz):
    @pl.when(pl.program_id(0) == 0)
    def _():
        h_ref[...] = jnp.zeros_like(h_ref)

    x_ref[0] = _dot(_bf(uf_ref[...]), bb_ref[0])
    x_ref[1] = _dot(_bf(ub_ref[...]), bb_ref[1])

    def step(i, carry):
        hfr, hfi, hbr, hbi = carry
        rf = pl.multiple_of(i * bsz, bsz)
        rb = pl.multiple_of((tt - 1 - i) * bsz, bsz)
        ar, ai = a_ref[0, 0], a_ref[0, 1]
        xr = x_ref[0, pl.ds(rf, bsz), 0:S5_LANES]
        xi = x_ref[0, pl.ds(rf, bsz), S5_LANES:2 * S5_LANES]
        nfr = ar * hfr - ai * hfi + xr
        nfi = ar * hfi + ai * hfr + xi
        x_ref[0, pl.ds(rf, bsz), 0:S5_LANES] = nfr
        x_ref[0, pl.ds(rf, bsz), S5_LANES:2 * S5_LANES] = nfi
        ar, ai = a_ref[1, 0], a_ref[1, 1]
        xr = x_ref[1, pl.ds(rb, bsz), 0:S5_LANES]
        xi = x_ref[1, pl.ds(rb, bsz), S5_LANES:2 * S5_LANES]
        nbr = ar * hbr - ai * hbi + xr
        nbi = ar * hbi + ai * hbr + xi
        x_ref[1, pl.ds(rb, bsz), 0:S5_LANES] = nbr
        x_ref[1, pl.ds(rb, bsz), S5_LANES:2 * S5_LANES] = nbi
        return nfr, nfi, nbr, nbi

    hs = lax.fori_loop(0, tt, step, (h_ref[0, 0], h_ref[0, 1], h_ref[1, 0], h_ref[1, 1]))
    h_ref[0, 0], h_ref[0, 1], h_ref[1, 0], h_ref[1, 1] = hs
    yf_ref[...] = _dot(_bf(x_ref[0]), cc_ref[0])
    yb_ref[...] = _dot(_bf(x_ref[1]), cc_ref[1])


def _s5_matrices(lam_re, lam_im, log_step, b_re, b_im, c_re, c_im):
    lam_re, lam_im = lam_re.astype(F32), lam_im.astype(F32)
    step = jnp.exp(log_step.astype(F32))[..., None]
    mag = jnp.exp(lam_re * step)
    a_re, a_im = mag * jnp.cos(lam_im * step), mag * jnp.sin(lam_im * step)
    den = lam_re * lam_re + lam_im * lam_im
    n_re = a_re - 1.0
    coef_re = (n_re * lam_re + a_im * lam_im) / den
    coef_im = (a_im * lam_re - n_re * lam_im) / den
    b_re, b_im = b_re.astype(F32), b_im.astype(F32)
    bb_re = coef_re[..., None] * b_re - coef_im[..., None] * b_im
    bb_im = coef_re[..., None] * b_im + coef_im[..., None] * b_re
    eye_g = jnp.eye(S5_GROUPS, dtype=F32)
    bb = jnp.stack([bb_re, bb_im], axis=1)
    bb = jnp.einsum('drgph,gk->dghrkp', bb, eye_g).reshape(2, BRANCH_W, 2 * S5_LANES)
    cc = jnp.stack([c_re.astype(F32), -c_im.astype(F32)], axis=0)
    cc = jnp.einsum('rgkp,gj->rjpgk', cc, eye_g).reshape(2 * S5_LANES, BRANCH_W)
    return bb.astype(BF16), cc.astype(BF16), a_re.reshape(2, S5_LANES), a_im.reshape(2, S5_LANES)


def _s5(u_tb, mats, *, bsz, tc, tt):
    bb, cc, a_re, a_im = mats
    n_rows = u_tb.shape[0]
    t_all = n_rows // bsz
    nt, nct = t_all // tt, tc // tt
    a = jnp.broadcast_to(jnp.stack([a_re, a_im], axis=1)[:, :, None, :], (2, 2, bsz, S5_LANES))
    cc2 = jnp.broadcast_to(cc[None], (2,) + cc.shape)
    blk = (tt * bsz, BRANCH_W)
    fmap = lambda s: (s, 0)
    bmap = lambda s: (_bwd_order(s, nct, nt), 0)
    return pl.pallas_call(
        functools.partial(_s5_kernel, tt=tt, bsz=bsz),
        grid=(nt,),
        in_specs=[pl.BlockSpec(blk, fmap), pl.BlockSpec(blk, bmap),
                  pl.BlockSpec(bb.shape, lambda s: (0, 0, 0)), pl.BlockSpec(cc2.shape, lambda s: (0, 0, 0)),
                  pl.BlockSpec(a.shape, lambda s: (0, 0, 0, 0))],
        out_specs=[pl.BlockSpec(blk, fmap), pl.BlockSpec(blk, bmap)],
        out_shape=[jax.ShapeDtypeStruct((n_rows, BRANCH_W), F32)] * 2,
        scratch_shapes=[pltpu.VMEM((2, tt * bsz, 2 * S5_LANES), F32), pltpu.VMEM((2, 2, bsz, S5_LANES), F32)],
        compiler_params=pltpu.CompilerParams(dimension_semantics=("arbitrary",),
                                             vmem_limit_bytes=V7X_VMEM_LIMIT),
        name="s5_scan",
    )(u_tb, u_tb, bb, cc2, a)


def _rope(x, cos, sin_signed):
    w = x.shape[-1]
    lane = lax.broadcasted_iota(jnp.int32, x.shape, 1)
    up = pltpu.roll(x, w - 16, 1)
    dn = pltpu.roll(x, 16, 1)
    return x * cos + jnp.where((lane & 31) < 16, up, dn) * sin_signed


def _attn_kernel(q_ref, kv_ref, cq_ref, sq_ref, ck_ref, sk_ref, qg_ref, kg_ref, bdq_ref, bdk_ref,
                 o_ref, kp_ref, vp_ref, *, tm, tc, nct):
    j = pl.program_id(1)
    t_all = kv_ref.shape[0]
    half = BRANCH_W // 2

    @pl.when(j == 0)
    def _():
        def pairs(x):
            lane = lax.broadcasted_iota(jnp.int32, x.shape, 1)
            sw = pltpu.roll(x, HEAD_DIM, 1)
            return _bf(jnp.where(lane < HEAD_DIM, x, sw)), _bf(jnp.where(lane < HEAD_DIM, sw, x))

        for lo, hi, rope in ((0, tc, False), (tc, t_all, True)):
            k = _head_rms(kv_ref[lo:hi, 0:half], bdk_ref[...], kg_ref[...])
            if rope:
                k = _rope(k, ck_ref[...], sk_ref[...])
            kp_ref[0, lo:hi, :], kp_ref[1, lo:hi, :] = pairs(k)
            v = kv_ref[lo:hi, half:2 * half]
            lane = lax.broadcasted_iota(jnp.int32, v.shape, 1)
            vp_ref[0, lo:hi, :] = _bf(jnp.where(lane < HEAD_DIM, v, 1.0))
            vp_ref[1, lo:hi, :] = _bf(jnp.where(lane < HEAD_DIM, pltpu.roll(v, HEAD_DIM, 1), 1.0))

    def attend(q, nk):
        lane = lax.broadcasted_iota(jnp.int32, (tm, half), 1)
        outs = []
        for g in range(2):
            qp = q[:, g * half:(g + 1) * half]
            q2 = jnp.concatenate([jnp.where(lane < HEAD_DIM, qp, 0.0), jnp.where(lane < HEAD_DIM, 0.0, qp)], axis=0)
            s = _dot_nt(_bf(q2), kp_ref[g, 0:nk, :])
            p = jnp.exp(_bf(s - jnp.max(s, axis=-1, keepdims=True)))
            ol = _dot(p, vp_ref[g, 0:nk, :])
            res = ol / pltpu.roll(ol, HEAD_DIM, 1)
            outs.append(jnp.where(lane < HEAD_DIM, res[0:tm], pltpu.roll(res[tm:2 * tm], HEAD_DIM, 1)))
        o_ref[...] = jnp.concatenate(outs, axis=1)

    qn = _head_rms(q_ref[...], bdq_ref[...], qg_ref[...]) * (HEAD_DIM ** -0.5)

    @pl.when(j < nct)
    def _():
        attend(qn, tc)

    @pl.when(j >= nct)
    def _():
        attend(_rope(qn, cq_ref[...], sq_ref[...]), t_all)


def _rope_tables(n_tokens):
    rows = n_tokens // GRID_W
    r, col = np.meshgrid(np.arange(rows), np.arange(GRID_W), indexing='ij')
    axis_dim = HEAD_DIM // 2
    inv = jnp.asarray(ROPE_THETA, F32) ** (-jnp.arange(0, axis_dim, 2, dtype=F32) / axis_dim)
    ang_r = jnp.asarray(r.reshape(-1, 1), F32) * inv
    ang_c = jnp.asarray(col.reshape(-1, 1), F32) * inv
    ang = jnp.concatenate([ang_r, ang_r, ang_c, ang_c], axis=-1)
    sign = np.where((np.arange(HEAD_DIM) % 32) < 16, -1.0, 1.0).astype(np.float32)
    cos, sin = jnp.cos(ang), jnp.sin(ang) * sign
    return jnp.tile(cos, (1, N_HEADS)), jnp.tile(sin, (1, N_HEADS))


def _attention(p_q, p_kv, qn_g, kn_g, cos4, sin4, *, tm, tc):
    bsz, t_all, _ = p_q.shape
    nt, nct = t_all // tm, tc // tm
    half = BRANCH_W // 2
    tok = lambda b, j: (b, j, 0)
    c2 = lambda b, j: (0, 0)
    lat = lambda b, j: (jnp.maximum(j - nct, 0), 0)
    qg = jnp.tile(qn_g.astype(F32), N_HEADS).reshape(1, BRANCH_W)
    kg = jnp.tile(kn_g.astype(F32), 2).reshape(1, half)
    bdq = jnp.asarray(_np_block_diag_ones(BRANCH_W, HEAD_DIM), BF16)
    bdk = jnp.asarray(_np_block_diag_ones(half, HEAD_DIM), BF16)
    cosk, sink = cos4[:, 0:half], sin4[:, 0:half]
    return pl.pallas_call(
        functools.partial(_attn_kernel, tm=tm, tc=tc, nct=nct),
        grid=(bsz, nt),
        in_specs=[pl.BlockSpec((None, tm, BRANCH_W), tok),
                  pl.BlockSpec((None, t_all, BRANCH_W), lambda b, j: (b, 0, 0)),
                  pl.BlockSpec((tm, BRANCH_W), lat), pl.BlockSpec((tm, BRANCH_W), lat),
                  pl.BlockSpec(cosk.shape, c2), pl.BlockSpec(sink.shape, c2),
                  pl.BlockSpec((1, BRANCH_W), c2), pl.BlockSpec((1, half), c2),
                  pl.BlockSpec(bdq.shape, c2), pl.BlockSpec(bdk.shape, c2)],
        out_specs=pl.BlockSpec((None, tm, BRANCH_W), tok),
        out_shape=jax.ShapeDtypeStruct((bsz, t_all, BRANCH_W), F32),
        scratch_shapes=[pltpu.VMEM((2, t_all, half), BF16), pltpu.VMEM((2, t_all, half), BF16)],
        compiler_params=pltpu.CompilerParams(dimension_semantics=("parallel", "arbitrary"),
                                             vmem_limit_bytes=V7X_VMEM_LIMIT),
        name="attention",
    )(p_q, p_kv, cos4, sin4, cosk, sink, qg, kg, bdq, bdk)


def _merge_kernel(x_ref, m_ref, g_ref, dnf_ref, dnb_ref, z_ref, sf_ref, sb_ref, u_ref, hf_ref, hb_ref, hgate_ref,
                  at_ref, dng_ref, hgg_ref, dsk_ref, glu_ref, bd_ref, wg_ref, wb_ref, wo_ref, o_ref):
    x = x_ref[...]
    bd = bd_ref[...]
    h = _bf(_rms(x) * g_ref[...] * (1.0 + m_ref[4:5, :]) + m_ref[3:4, :])
    y_dn = _head_rms(dnf_ref[...] + dnb_ref[...], bd, dng_ref[...]) * _silu(z_ref[...])
    ys = sf_ref[...] + sb_ref[...] + dsk_ref[...] * u_ref[...]
    ab = _dot(_bf(jax.nn.gelu(ys, approximate=True)), glu_ref[...])
    y_s5 = ab[:, 0:BRANCH_W] * _sigmoid(ab[:, BRANCH_W:2 * BRANCH_W])
    y_hg = _head_rms(hf_ref[...] + hb_ref[...], bd, hgg_ref[...]) * _sigmoid(hgate_ref[...])
    ys_all = (y_dn, y_s5, y_hg, at_ref[...])
    acc = jnp.zeros(x.shape, F32)
    for i in range(4):
        acc = acc + _sigmoid(_dot(h, wg_ref[i])) * _dot(_bf(ys_all[i]), wb_ref[i])
    o_ref[...] = x + m_ref[5:6, :] * _dot(_bf(acc), wo_ref[...])


def _merge(xj, mods, g, br, params, *, tm, nct, j0):
    bsz, t_all, _ = xj.shape
    n_tiles = t_all // tm - j0
    tok = lambda b, j: (b, j + j0, 0)
    col = lambda c: (lambda b, j: (b, j + j0, c))
    mod_map = lambda b, j: (jnp.where(j + j0 < nct, bsz, b), 0, 0)
    c2 = lambda b, j: (0, 0)
    c3 = lambda b, j: (0, 0, 0)
    tb = (None, tm, BRANCH_W)
    tmaj, tmaj_map = (tm, BRANCH_W), (lambda b, j: (j + j0, b))
    dn_g, hg_g, d_skip, w_glu, wg, wb, wo = params
    bd = jnp.asarray(_np_block_diag_ones(BRANCH_W, HEAD_DIM), BF16)
    return pl.pallas_call(
        _merge_kernel,
        grid=(bsz, n_tiles),
        in_specs=[pl.BlockSpec((None, tm, D_MODEL), tok),
                  pl.BlockSpec((None, N_MOD, D_MODEL), mod_map),
                  pl.BlockSpec((1, D_MODEL), c2),
                  pl.BlockSpec(tb, tok), pl.BlockSpec(tb, tok), pl.BlockSpec(tb, col(3)),
                  pl.BlockSpec(tmaj, tmaj_map), pl.BlockSpec(tmaj, tmaj_map), pl.BlockSpec(tmaj, tmaj_map),
                  pl.BlockSpec(tb, tok), pl.BlockSpec(tb, tok), pl.BlockSpec(tb, col(4)),
                  pl.BlockSpec(tb, tok),
                  pl.BlockSpec((1, BRANCH_W), c2), pl.BlockSpec((1, BRANCH_W), c2), pl.BlockSpec((1, BRANCH_W), c2),
                  pl.BlockSpec(w_glu.shape, c2), pl.BlockSpec(bd.shape, c2),
                  pl.BlockSpec(wg.shape, c3), pl.BlockSpec(wb.shape, c3), pl.BlockSpec(wo.shape, c2)],
        out_specs=pl.BlockSpec((None, tm, D_MODEL), lambda b, j: (b, j, 0)),
        out_shape=jax.ShapeDtypeStruct((bsz, n_tiles * tm, D_MODEL), F32),
        compiler_params=pltpu.CompilerParams(dimension_semantics=("parallel", "parallel"),
                                             vmem_limit_bytes=V7X_VMEM_LIMIT),
        name="branch_merge",
    )(xj, mods, g.reshape(1, D_MODEL), br['dn_f'], br['dn_b'], br['p_dn'], br['s5_f'], br['s5_b'], br['p_s5'],
      br['hg_f'], br['hg_b'], br['p_hg'], br['at'], dn_g, hg_g, d_skip, w_glu, bd, wg, wb, wo)


def kernel(x, c, ctx, c_ctx, ada_w, ada_b, norm_g, ffn_w1, ffn_w3, ffn_w2, w_in, dn_conv, dn_a_log, dn_dt_bias,
           dn_norm_g, s5_lam_re, s5_lam_im, s5_log_step, s5_b_re, s5_b_im, s5_c_re, s5_c_im, s5_d, s5_glu,
           hg_lb_logits, hg_norm_g, at_qn_g, at_kn_g, w_branch, w_out, final_g):
    bsz, tl, _ = x.shape
    tc = ctx.shape[1]
    t_all = tc + tl
    depth = ada_w.shape[0]
    tm = 256 if (tc % 256 == 0 and tl % 256 == 0) else 128
    assert tc % tm == 0 and tl % tm == 0 and tm % CHUNK == 0 and bsz % 8 == 0
    nct = tc // tm
    d_ff = ffn_w1.shape[-1]
    n_ff = d_ff // FF_CHUNK

    n_rows = -(-(bsz + 1) // 8) * 8
    cond = jnp.zeros((n_rows, D_MODEL), F32).at[0:bsz].set(c.astype(F32)).at[bsz].set(c_ctx.astype(F32))
    mods_all = _ada_mods(cond, ada_w.astype(F32), ada_b.astype(F32))
    cos4, sin4 = _rope_tables(tl)
    xj = x.astype(F32)

    def ffn_weights(l, i):
        w1 = ffn_w1[l, i].astype(BF16).reshape(D_MODEL, n_ff, FF_CHUNK).transpose(1, 0, 2)
        w3 = ffn_w3[l, i].astype(BF16).reshape(D_MODEL, n_ff, FF_CHUNK).transpose(1, 0, 2)
        w2 = ffn_w2[l, i].astype(BF16)
        return w1, w3, w2

    for l in range(depth):
        last = l == depth - 1
        mods = mods_all[l]
        xj = _ffn_half_step(xj, mods, norm_g[l, 0], *ffn_weights(l, 0), mod_base=0, tm=tm, nct=nct,
                            ctx=ctx.astype(F32) if l == 0 else None)

        wl = w_in[l]
        w_dn = jnp.concatenate([wl[:, 0:1040], jnp.zeros((D_MODEL, 112), wl.dtype)], axis=1)
        ws = [w.astype(BF16) for w in (w_dn, wl[:, 1040:1296], wl[:, 1296:2576], wl[:, 2576:2832], wl[:, 2832:3088])]
        p_dn, p_s5, p_hg, p_aq, p_kv = _input_projection(xj, mods, norm_g[l, 1], ws, tm=tm, nct=nct)

        dn_f, dn_b = _deltanet(p_dn, dn_conv[l], dn_a_log[l], dn_dt_bias[l], tm=tm, tc=tc)
        hg_f, hg_b = _hgrn2(p_hg, hg_lb_logits, l, tm=tm, tc=tc)
        mats = _s5_matrices(s5_lam_re[l], s5_lam_im[l], s5_log_step[l], s5_b_re[l], s5_b_im[l], s5_c_re[l], s5_c_im[l])
        s5_f, s5_b = _s5(p_s5.reshape(t_all * bsz, BRANCH_W), mats, bsz=bsz, tc=tc, tt=CHUNK)
        s5_f = s5_f.reshape(t_all, bsz * BRANCH_W)
        s5_b = s5_b.reshape(t_all, bsz * BRANCH_W)
        y_at = _attention(p_aq, p_kv, at_qn_g[l], at_kn_g[l], cos4, sin4, tm=tm, tc=tc)

        br = dict(dn_f=dn_f, dn_b=dn_b, p_dn=p_dn, s5_f=s5_f, s5_b=s5_b, p_s5=p_s5, hg_f=hg_f, hg_b=hg_b,
                  p_hg=p_hg, at=y_at)
        params = (jnp.tile(dn_norm_g[l].astype(F32), N_HEADS).reshape(1, BRANCH_W),
                  jnp.tile(hg_norm_g[l].astype(F32), N_HEADS).reshape(1, BRANCH_W),
                  s5_d[l].astype(F32).reshape(1, BRANCH_W),
                  s5_glu[l].astype(BF16),
                  wl[:, 3088:].astype(BF16).reshape(D_MODEL, 4, D_MODEL).transpose(1, 0, 2),
                  w_branch[l].astype(BF16),
                  w_out[l].astype(BF16))
        xj = _merge(xj, mods, norm_g[l, 1], br, params, tm=tm, nct=nct, j0=nct if last else 0)
        xj = _ffn_half_step(xj, mods, norm_g[l, 2], *ffn_weights(l, 1), mod_base=6, tm=tm,
                            nct=0 if last else nct, final_g=final_g.astype(F32) if last else None)
    return xj.astype(x.dtype)
```

```python
import functools
import math

import numpy as np
import jax
import jax.numpy as jnp
from jax import lax
from jax.experimental import pallas as pl
from jax.experimental.pallas import tpu as pltpu

F32 = jnp.float32
BF16 = jnp.bfloat16

D_MODEL = 1024
N_MOD = 9
BRANCH_W = 256
HEAD_DIM = 64
N_HEADS = 4
CHUNK = 64
HG_BLK = 16
EPS = 1e-6
LOG_FLOOR = 1e-30
GRID_W = 64
ROPE_THETA = 10000.0
FF_CHUNK = 256
S5_GROUPS = 16
S5_GROUP = 16
S5_STATE = 64
S5_LANES = S5_GROUPS * S5_STATE
NEG_BIG = -1e30
V7X_VMEM_LIMIT = 56 * 1024 * 1024


def _bf(x):
    return x.astype(BF16)


def _dot(a, b):
    return jnp.dot(a, b, preferred_element_type=F32)


def _dot_nt(a, b):
    return lax.dot_general(a, b, (((1,), (1,)), ((), ())), preferred_element_type=F32)


def _dot_tn(a, b):
    return lax.dot_general(a, b, (((0,), (0,)), ((), ())), preferred_element_type=F32)


def _split2(x):
    hi = _bf(x)
    lo = _bf(x - hi.astype(F32))
    return hi, lo


def _split3(x):
    x1 = _bf(x)
    r1 = x - x1.astype(F32)
    x2 = _bf(r1)
    x3 = _bf(r1 - x2.astype(F32))
    return x1, x2, x3


def _dot_x2(x, m):
    hi, lo = _split2(x)
    return _dot(hi, m) + _dot(lo, m)


def _dot_x3(x, m):
    x1, x2, x3 = _split3(x)
    return _dot(x1, m) + _dot(x2, m) + _dot(x3, m)


def _dot3_left(m, x):
    x1, x2, x3 = _split3(x)
    return _dot(m, x1) + _dot(m, x2) + _dot(m, x3)


def _dot_hp(a, b):
    a_hi, a_lo = _split2(a)
    b_hi, b_lo = _split2(b)
    return _dot(a_hi, b_hi) + _dot(a_hi, b_lo) + _dot(a_lo, b_hi)


def _sigmoid(x):
    return jax.nn.sigmoid(x)


def _silu(x):
    return x * jax.nn.sigmoid(x)


def _rms(x):
    return x * lax.rsqrt(jnp.mean(x * x, axis=-1, keepdims=True) + EPS)


def _head_rms(x, bd_ones, gain):
    ss = _dot_x2(x * x, bd_ones) * (1.0 / HEAD_DIM)
    return x * lax.rsqrt(ss + EPS) * gain


def _stack4(x):
    return jnp.concatenate([x, x, x, x], axis=0)


def _unstack4(x):
    n = x.shape[0] // 4
    return x[0:n] + x[n:2 * n] + x[2 * n:3 * n] + x[3 * n:4 * n]


def _np_block_diag_ones(n, blk):
    i = np.arange(n)
    return (i[:, None] // blk == i[None, :] // blk).astype(np.float32)


def _np_tri(n, blk, lower, strict):
    i = np.arange(n)
    same = i[:, None] // blk == i[None, :] // blk
    a, b = i[:, None] % blk, i[None, :] % blk
    if lower:
        t = (a > b) if strict else (a >= b)
    else:
        t = (a < b) if strict else (a <= b)
    return (same & t).astype(np.float32)


def _np_doubling_masks(n, blk):
    i = np.arange(n)
    r, c = i[:, None], i[None, :]
    out = [(r // 2 == c // 2)]
    s = 2
    while s < blk:
        out.append((r // (2 * s) == c // (2 * s)) & (r // s != c // s))
        s *= 2
    return np.stack(out).astype(np.float32)


def _bwd_order(s, n_ctx, n_all):
    return jnp.where(s < n_ctx, n_ctx - 1 - s, n_all + n_ctx - 1 - s)


def _ada_kernel(c_ref, w_ref, b_ref, o_ref):
    s = _silu(c_ref[...])
    w = w_ref[...]
    s_hi, s_lo = _split2(s)
    w_hi, w_lo = _split2(w)
    o_ref[...] = _dot(s_hi, w_hi) + _dot(s_hi, w_lo) + _dot(s_lo, w_hi) + b_ref[...]


def _ada_mods(cond, ada_w, ada_b):
    n_layers = ada_w.shape[0]
    r = cond.shape[0]
    n_out = ada_w.shape[2]
    tn = 1152
    out = pl.pallas_call(
        _ada_kernel,
        grid=(n_layers, n_out // tn),
        in_specs=[
            pl.BlockSpec((r, D_MODEL), lambda l, n: (0, 0)),
            pl.BlockSpec((None, D_MODEL, tn), lambda l, n: (l, 0, n)),
            pl.BlockSpec((None, 1, tn), lambda l, n: (l, 0, n)),
        ],
        out_specs=pl.BlockSpec((None, r, tn), lambda l, n: (l, 0, n)),
        out_shape=jax.ShapeDtypeStruct((n_layers, r, n_out), F32),
        compiler_params=pltpu.CompilerParams(dimension_semantics=("parallel", "parallel"),
                                             vmem_limit_bytes=40 * 1024 * 1024),
        name="ada_mods",
    )(cond, ada_w, ada_b.reshape(n_layers, 1, n_out))
    return out.reshape(n_layers, r, N_MOD, D_MODEL)


def _ffn_kernel(*refs, mod_base, final, n_ctx_tiles, head_rows):
    refs = list(refs)
    o_ref = refs.pop()
    x_ref = refs.pop(0)
    c_ref = refs.pop(0) if n_ctx_tiles else None
    m_ref = refs.pop(0)
    mc_ref = refs.pop(0) if head_rows else None
    g_ref, w1_ref, w3_ref, w2_ref = refs[0:4]
    fg_ref = refs[4] if final else None
    x = x_ref[...]
    if c_ref is not None:
        x = jnp.where(pl.program_id(1) < n_ctx_tiles, c_ref[...], x)

    def mod(i, head):
        row = m_ref[mod_base + i:mod_base + i + 1, :]
        if head:
            row = jnp.where(pl.program_id(1) == 0, mc_ref[mod_base + i:mod_base + i + 1, :], row)
        return row

    def rows(fn):
        if not head_rows:
            return fn(slice(None), False)
        return jnp.concatenate([fn(slice(0, head_rows), True), fn(slice(head_rows, None), False)], axis=0)

    y = _rms(x) * g_ref[...]
    h = _bf(rows(lambda r, hd: y[r] * (1.0 + mod(1, hd)) + mod(0, hd)))
    us = []
    for c in range(w1_ref.shape[0]):
        a = _dot(h, w1_ref[c])
        us.append(_bf(_silu(a) * _dot(h, w3_ref[c])))
    acc = _dot(jnp.concatenate(us, axis=1), w2_ref[...])
    out = x + 0.5 * rows(lambda r, hd: mod(2, hd) * acc[r])
    if final:
        out = _rms(out) * fg_ref[...]
    o_ref[...] = out


def _ffn_half_step(xj, mods, g, w1c, w3c, w2c, *, mod_base, tm, tc, final_g=None, ctx=None):
    bsz, t_all, _ = xj.shape
    final = final_g is not None
    if ctx is not None:
        t_all += ctx.shape[1]
    assert t_all % tm == 0 and (tc % tm == 0 or (tc < tm and tc % 8 == 0 and ctx is None))
    nct = tc // tm
    head_rows = tc if tc % tm else 0

    def tok_map(b, j):
        return (b, j, 0)

    def mod_map(b, j):
        return (jnp.where(j < nct, bsz, b), 0, 0)

    const3 = lambda b, j: (0, 0, 0)
    const2 = lambda b, j: (0, 0)
    once = dict(pipeline_mode=pl.Buffered(1))
    if ctx is None:
        tok_specs, tok_args = [pl.BlockSpec((None, tm, D_MODEL), tok_map)], [xj]
    else:
        tok_specs = [pl.BlockSpec((None, tm, D_MODEL), lambda b, j: (b, jnp.maximum(j - nct, 0), 0)),
                     pl.BlockSpec((None, tm, D_MODEL), lambda b, j: (b, jnp.minimum(j, nct - 1), 0))]
        tok_args = [xj, ctx]
    in_specs = tok_specs + [pl.BlockSpec((None, N_MOD, D_MODEL), mod_map)]
    args = tok_args + [mods]
    if head_rows:
        in_specs.append(pl.BlockSpec((None, N_MOD, D_MODEL), lambda b, j: (bsz, 0, 0)))
        args.append(mods)
    in_specs += [
        pl.BlockSpec((1, D_MODEL), const2),
        pl.BlockSpec(w1c.shape, const3, **once),
        pl.BlockSpec(w3c.shape, const3, **once),
        pl.BlockSpec(w2c.shape, const2, **once),
    ]
    args += [g.reshape(1, D_MODEL), w1c, w3c, w2c]
    if final:
        in_specs.append(pl.BlockSpec((1, D_MODEL), const2))
        args.append(final_g.reshape(1, D_MODEL))
    return pl.pallas_call(
        functools.partial(_ffn_kernel, mod_base=mod_base, final=final, n_ctx_tiles=nct if ctx is not None else 0,
                          head_rows=head_rows),
        grid=(bsz, t_all // tm),
        in_specs=in_specs,
        out_specs=pl.BlockSpec((None, tm, D_MODEL), tok_map),
        out_shape=jax.ShapeDtypeStruct((bsz, t_all, D_MODEL), F32),
        compiler_params=pltpu.CompilerParams(dimension_semantics=("parallel", "parallel"),
                                             vmem_limit_bytes=V7X_VMEM_LIMIT),
        name="ffn_half_step",
    )(*args)


def _inproj_kernel(x_ref, m_ref, g_ref, wdn_ref, ws5_ref, whg_ref, waq_ref, wkv_ref,
                   dn_ref, s5_ref, hg_ref, aq_ref, kv_ref):
    x = x_ref[...]
    h = _bf(_rms(x) * g_ref[...] * (1.0 + m_ref[4:5, :]) + m_ref[3:4, :])
    dn_ref[...] = _dot(h, wdn_ref[...])
    s5_ref[...] = _dot(h, ws5_ref[...])
    hg_ref[...] = _dot(h, whg_ref[...])
    aq_ref[...] = _dot(h, waq_ref[...])
    kv_ref[...] = _dot(h, wkv_ref[...])


def _input_projection(xj, mods, g, ws, *, tm, nct):
    bsz, t_all, _ = xj.shape
    tok_map = lambda b, j: (b, j, 0)
    mod_map = lambda b, j: (jnp.where(j < nct, bsz, b), 0, 0)
    const2 = lambda b, j: (0, 0)
    widths = [w.shape[1] for w in ws]
    return pl.pallas_call(
        _inproj_kernel,
        grid=(bsz, t_all // tm),
        in_specs=[pl.BlockSpec((None, tm, D_MODEL), tok_map),
                  pl.BlockSpec((None, N_MOD, D_MODEL), mod_map),
                  pl.BlockSpec((1, D_MODEL), const2)]
                 + [pl.BlockSpec(w.shape, const2) for w in ws],
        out_specs=[pl.BlockSpec((tm, n), lambda b, j: (j, b)) if i == 1 else pl.BlockSpec((None, tm, n), tok_map)
                   for i, n in enumerate(widths)],
        out_shape=[jax.ShapeDtypeStruct((t_all, bsz * n) if i == 1 else (bsz, t_all, n), F32)
                   for i, n in enumerate(widths)],
        compiler_params=pltpu.CompilerParams(dimension_semantics=("parallel", "parallel"),
                                             vmem_limit_bytes=V7X_VMEM_LIMIT),
        name="input_projection",
    )(xj, mods, g.reshape(1, D_MODEL), *ws)


def _dn_prep_kernel(p_ref, prev_ref, next_ref, cw_ref, al_ref, dtb_ref, e_ref, tril_ref, triu_ref, bd_ref,
                    q_ref, k_ref, v_ref, gb_ref, *, tm, nct, nt):
    j = pl.program_id(1)
    first = jnp.logical_or(j == 0, j == nct)
    last = jnp.logical_or(j == nct - 1, j == nt - 1)
    x = p_ref[:, 0:3 * BRANCH_W]
    pv = prev_ref[...] * jnp.where(first, 0.0, 1.0)
    nx = next_ref[...] * jnp.where(last, 0.0, 1.0)
    ext = jnp.concatenate([pv, x, nx], axis=0)
    n = tm + 16
    acc = jnp.zeros((tm, 3 * BRANCH_W), F32)
    for t in range(5):
        sh = (2 - t) % n
        r = ext if sh == 0 else pltpu.roll(ext, sh, 0)
        acc = acc + r[8:8 + tm] * cw_ref[t:t + 1, :]
    h = _silu(acc)
    bd = bd_ref[...]

    def l2n(t):
        return t * lax.rsqrt(_dot_x2(t * t, bd) + EPS)

    q_ref[...] = l2n(h[:, 0:BRANCH_W]) * (HEAD_DIM ** -0.5)
    k_ref[...] = l2n(h[:, BRANCH_W:2 * BRANCH_W])
    v_ref[...] = h[:, 2 * BRANCH_W:3 * BRANCH_W]

    ab = p_ref[:, 4 * BRANCH_W:4 * BRANCH_W + 128]
    z = ab + dtb_ref[...]
    softplus = jnp.maximum(z, 0.0) + jnp.log1p(jnp.exp(-jnp.abs(z)))
    g = -jnp.exp(al_ref[...]) * softplus
    lane = lax.broadcasted_iota(jnp.int32, ab.shape, 1)
    g = jnp.where(lane < 8, g, 0.0)
    cs_f = _dot3_left(tril_ref[...], g)
    cs_b = _dot3_left(triu_ref[...], g)
    comp = jnp.where(lane < 4, cs_f, jnp.where(lane < 8, cs_b, jnp.where(lane < 16, _sigmoid(ab), 0.0)))
    gb_ref[...] = _dot_x3(comp, e_ref[...])


def _dn_chunks_local(items, bd, it, lvl_ref):
    def stk(x):
        return _stack4(_bf(x)) * bd

    ks = [stk(c[1]) for c in items]
    gq = [_dot_nt(_bf(jnp.concatenate([c[1] * c[4], c[0]], axis=0)), x) for c, x in zip(items, ks)]
    decays, a_s = [], []
    for (q, k, v, gc, be, neg_incl, tri_strict, fwd), g in zip(items, gq):
        col_gc = jnp.sum(gc * it, axis=0, keepdims=True)
        decay = jnp.exp(gc - col_gc + neg_incl)
        decays.append(decay)
        a_s.append(g[0:CHUNK] * decay * tri_strict)
    ts = [it - a * lvl_ref[0] for a in a_s]
    for lv in range(1, lvl_ref.shape[0]):
        xs = [_dot(_bf(a * lvl_ref[lv]), stk(t)) for a, t in zip(a_s, ts)]
        ts = [t - _dot(_bf(t), stk(x)) for t, x in zip(ts, xs)]
    egcs = [jnp.exp(c[3]) for c in items]
    uws = [_dot(_bf(t), jnp.concatenate([stk(c[2] * c[4]), stk(c[1] * c[4] * e)], axis=1))
           for t, c, e in zip(ts, items, egcs)]
    out = []
    for (q, k, v, gc, be, _, _, fwd), uw, g, dc, egc in zip(items, uws, gq, decays, egcs):
        total = gc[CHUNK - 1:CHUNK, :] if fwd else gc[0:1, :]
        out.append((uw[:, 0:BRANCH_W], _bf(uw[:, BRANCH_W:2 * BRANCH_W]), _bf(g[CHUNK:2 * CHUNK] * dc),
                    _bf(q * egc), _bf(k * jnp.exp(total - gc)), jnp.exp(total)))
    return out


def _dn_chunk_state(local, s_prev, bd, bd_f32):
    u, w, attn, q_dec, k_dec, g_last = local
    ws = _dot(jnp.concatenate([w, q_dec], axis=0), _bf(s_prev))
    v_new = _bf(u - ws[0:CHUNK])
    o = ws[CHUNK:2 * CHUNK] + _dot(attn, _stack4(v_new) * bd)
    s_new = s_prev * g_last + _dot_tn(k_dec, v_new) * bd_f32
    return o, s_new


def _dn_scan_kernel(qf_ref, kf_ref, vf_ref, gcf_ref, bef_ref, qb_ref, kb_ref, vb_ref, gcb_ref, beb_ref,
                    bdf_ref, li_ref, ls_ref, ui_ref, us_ref, it_ref, lvl_ref, bd_ref,
                    of_ref, ob_ref, s_ref):
    @pl.when(pl.program_id(1) == 0)
    def _():
        s_ref[...] = jnp.zeros_like(s_ref)

    bd, bd_f32, it = bd_ref[...], bdf_ref[...], it_ref[...]
    n_sub = qf_ref.shape[0] // CHUNK
    dirs = ((qf_ref, kf_ref, vf_ref, gcf_ref, bef_ref, li_ref, ls_ref, of_ref, True),
            (qb_ref, kb_ref, vb_ref, gcb_ref, beb_ref, ui_ref, us_ref, ob_ref, False))
    items = []
    for q_ref, k_ref, v_ref, gc_ref, be_ref, ti_ref, ts_ref, _, fwd in dirs:
        for g in range(n_sub):
            r = slice(g * CHUNK, (g + 1) * CHUNK)
            items.append((q_ref[r, :], k_ref[r, :], v_ref[r, :], gc_ref[r, :], be_ref[r, :],
                          ti_ref[...], ts_ref[...], fwd))
    local = _dn_chunks_local(items, bd, it, lvl_ref)
    s_f, s_b = s_ref[0], s_ref[1]
    for i in range(n_sub):
        gb = n_sub - 1 - i
        o, s_f = _dn_chunk_state(local[i], s_f, bd, bd_f32)
        of_ref[i * CHUNK:(i + 1) * CHUNK, :] = o
        o, s_b = _dn_chunk_state(local[n_sub + gb], s_b, bd, bd_f32)
        ob_ref[gb * CHUNK:(gb + 1) * CHUNK, :] = o
    s_ref[0] = s_f
    s_ref[1] = s_b


def _deltanet(p_dn, conv_w, a_log, dt_bias, *, tm, tc):
    bsz, t_all, _ = p_dn.shape
    nt, nct = t_all // tm, tc // tm
    al = jnp.zeros((1, 128), F32).at[0, 0:8].set(a_log.reshape(8).astype(F32))
    dtb = jnp.zeros((1, 128), F32).at[0, 0:8].set(dt_bias.reshape(8).astype(F32))
    e = np.zeros((128, 4 * BRANCH_W), np.float32)
    for jj in range(4):
        for h in range(N_HEADS):
            e[4 * jj + h, jj * BRANCH_W + h * HEAD_DIM: jj * BRANCH_W + (h + 1) * HEAD_DIM] = 1.0
    consts = [jnp.asarray(e, BF16),
              jnp.asarray(_np_tri(tm, CHUNK, True, False), BF16),
              jnp.asarray(_np_tri(tm, CHUNK, False, False), BF16),
              jnp.asarray(_np_block_diag_ones(BRANCH_W, HEAD_DIM), BF16)]
    tok = lambda b, j: (b, j, 0)
    c2 = lambda b, j: (0, 0)
    hb = tm // 8
    q, k, v, gb = pl.pallas_call(
        functools.partial(_dn_prep_kernel, tm=tm, nct=nct, nt=nt),
        grid=(bsz, nt),
        in_specs=[pl.BlockSpec((None, tm, p_dn.shape[2]), tok),
                  pl.BlockSpec((None, 8, 3 * BRANCH_W), lambda b, j: (b, jnp.maximum(j * hb - 1, 0), 0)),
                  pl.BlockSpec((None, 8, 3 * BRANCH_W), lambda b, j: (b, jnp.minimum((j + 1) * hb, nt * hb - 1), 0)),
                  pl.BlockSpec(conv_w.shape, c2), pl.BlockSpec((1, 128), c2), pl.BlockSpec((1, 128), c2)]
                 + [pl.BlockSpec(c.shape, c2) for c in consts],
        out_specs=[pl.BlockSpec((None, tm, BRANCH_W), tok)] * 3 + [pl.BlockSpec((None, tm, 4 * BRANCH_W), tok)],
        out_shape=[jax.ShapeDtypeStruct((bsz, t_all, BRANCH_W), F32)] * 3
                  + [jax.ShapeDtypeStruct((bsz, t_all, 4 * BRANCH_W), F32)],
        compiler_params=pltpu.CompilerParams(dimension_semantics=("parallel", "parallel"),
                                             vmem_limit_bytes=40 * 1024 * 1024),
        name="deltanet_prep",
    )(p_dn, p_dn, p_dn, conv_w.astype(F32), al, dtb, *consts)

    nc, ncc = nt, nct
    n4 = N_HEADS * CHUNK
    neg = lambda tri: (1.0 - tri) * NEG_BIG
    per_head = lambda m: np.tile(m, (1,) * (m.ndim - 1) + (N_HEADS,))
    masks = [jnp.asarray(_np_block_diag_ones(n4, CHUNK), F32),
             jnp.asarray(per_head(neg(_np_tri(CHUNK, CHUNK, True, False))), F32),
             jnp.asarray(per_head(_np_tri(CHUNK, CHUNK, True, True)), F32),
             jnp.asarray(per_head(neg(_np_tri(CHUNK, CHUNK, False, False))), F32),
             jnp.asarray(per_head(_np_tri(CHUNK, CHUNK, False, True)), F32),
             jnp.asarray(per_head(np.eye(CHUNK, dtype=np.float32)), F32),
             jnp.asarray(per_head(_np_doubling_masks(CHUNK, CHUNK)), F32),
             jnp.asarray(_np_block_diag_ones(n4, CHUNK), BF16)]
    cmap = lambda m: (lambda b, s: (0,) * m.ndim)
    blk = (None, tm, BRANCH_W)
    fmap = lambda col: (lambda b, s: (b, s, col))
    bmap = lambda col: (lambda b, s: (b, _bwd_order(s, ncc, nc), col))
    o_f, o_b = pl.pallas_call(
        _dn_scan_kernel,
        grid=(bsz, nc),
        in_specs=[pl.BlockSpec(blk, fmap(0))] * 3 + [pl.BlockSpec(blk, fmap(0)), pl.BlockSpec(blk, fmap(2))]
                 + [pl.BlockSpec(blk, bmap(0))] * 3 + [pl.BlockSpec(blk, bmap(1)), pl.BlockSpec(blk, bmap(3))]
                 + [pl.BlockSpec(m.shape, cmap(m)) for m in masks],
        out_specs=[pl.BlockSpec(blk, fmap(0)), pl.BlockSpec(blk, bmap(0))],
        out_shape=[jax.ShapeDtypeStruct((bsz, t_all, BRANCH_W), F32)] * 2,
        scratch_shapes=[pltpu.VMEM((2, n4, n4), F32)],
        compiler_params=pltpu.CompilerParams(dimension_semantics=("parallel", "arbitrary"),
                                             vmem_limit_bytes=40 * 1024 * 1024),
        name="deltanet_scan",
    )(q, k, v, gb, gb, q, k, v, gb, gb, *masks)
    return o_f, o_b


def _hg_tile_prepare(q, f, v, lb, tri_incl, tri_excl, tmp_ref):
    qs = _silu(q)
    logf = jnp.log(jnp.maximum(lb + (1.0 - lb) * _sigmoid(f), LOG_FLOOR))
    kk = (1.0 - lb) * _sigmoid(-f)
    bl = _dot3_left(tri_incl, logf)
    suf = _dot3_left(tri_excl, logf)
    tmp_ref[0], tmp_ref[1], tmp_ref[2], tmp_ref[3] = qs, bl - jnp.log(kk), v, bl
    return _bf(qs * jnp.exp(bl)), _bf(kk * jnp.exp(suf)), _bf(v), bl


def _hg_block_pairs(tmp_ref, r0, bd_ones, fwd):
    io = lax.broadcasted_iota(jnp.int32, (HG_BLK, BRANCH_W), 0)
    q_b = _bf(tmp_ref[0, r0:r0 + HG_BLK, :])
    b_b = tmp_ref[3, r0:r0 + HG_BLK, :]
    rows = []
    for i in range(HG_BLK):
        c_s = jnp.broadcast_to(tmp_ref[1, r0 + i:r0 + i + 1, :], (HG_BLK, BRANCH_W))
        vis = (io >= i) if fwd else (io <= i)
        rows.append(q_b * jnp.exp(_bf(jnp.where(vis, b_b - c_s, NEG_BIG))))
    se = _dot(jnp.concatenate(rows, axis=0), bd_ones)
    out = None
    for i in range(HG_BLK):
        term = se[i * HG_BLK:(i + 1) * HG_BLK] * jnp.broadcast_to(tmp_ref[2, r0 + i:r0 + i + 1, :], (HG_BLK, BRANCH_W))
        out = term if out is None else out + term
    return out


def _hg_scan_kernel(qf_ref, ff_ref, vf_ref, qb_ref, fb_ref, vb_ref, lg_ref, bdo_ref, bdm_ref,
                    tli_ref, tls_ref, tui_ref, tus_ref, of_ref, ob_ref, st_ref, tmp_ref, *, layer):
    @pl.when(pl.program_id(1) == 0)
    def _():
        st_ref[...] = jnp.zeros_like(st_ref)

    n_layers = lg_ref.shape[0]
    mx = lg_ref[0:1, :]
    for i in range(1, n_layers):
        mx = jnp.maximum(mx, lg_ref[i:i + 1, :])
    den = jnp.zeros_like(mx)
    num = jnp.zeros_like(mx)
    for i in range(n_layers):
        e = jnp.exp(lg_ref[i:i + 1, :] - mx)
        den = den + e
        if 1 <= i <= layer:
            num = num + e
    lb = num / den
    bdo, bdm = bdo_ref[...], bdm_ref[...]
    n_blk = qf_ref.shape[0] // HG_BLK
    prep = (_hg_tile_prepare(qf_ref[...], ff_ref[...], vf_ref[...], lb, tli_ref[...], tus_ref[...], tmp_ref.at[0]),
            _hg_tile_prepare(qb_ref[...], fb_ref[...], vb_ref[...], lb, tui_ref[...], tls_ref[...], tmp_ref.at[1]))
    pairs = [[_hg_block_pairs(tmp_ref.at[d], i * HG_BLK, bdo, d == 0) for i in range(n_blk)] for d in range(2)]
    st = [st_ref[0], st_ref[1]]
    for step in range(n_blk):
        for d, o_ref in ((0, of_ref), (1, ob_ref)):
            i = step if d == 0 else n_blk - 1 - step
            r = slice(i * HG_BLK, (i + 1) * HG_BLK)
            q_t, k_t, v_b, bl = prep[d]
            o_ref[r, :] = pairs[d][i] + _dot_nt(q_t[r], _bf(st[d]))
            last = (i + 1) * HG_BLK - 1 if d == 0 else i * HG_BLK
            st[d] = st[d] * jnp.exp(bl[last:last + 1, :]) + _dot_tn(v_b[r], k_t[r]) * bdm
    st_ref[0], st_ref[1] = st


def _hgrn2(p_hg, lb_logits, layer, *, tm, tc):
    bsz, t_all, _ = p_hg.shape
    nc, ncc = t_all // tm, tc // tm
    consts = [jnp.asarray(_np_block_diag_ones(BRANCH_W, HEAD_DIM), BF16),
              jnp.asarray(_np_block_diag_ones(BRANCH_W, HEAD_DIM), F32),
              jnp.asarray(_np_tri(tm, HG_BLK, True, False), BF16),
              jnp.asarray(_np_tri(tm, HG_BLK, True, True), BF16),
              jnp.asarray(_np_tri(tm, HG_BLK, False, False), BF16),
              jnp.asarray(_np_tri(tm, HG_BLK, False, True), BF16)]
    blk = (None, tm, BRANCH_W)
    c2 = lambda b, s: (0, 0)
    fmap = lambda col: (lambda b, s: (b, s, col))
    bmap = lambda col: (lambda b, s: (b, _bwd_order(s, ncc, nc), col))
    return pl.pallas_call(
        functools.partial(_hg_scan_kernel, layer=layer),
        grid=(bsz, nc),
        in_specs=[pl.BlockSpec(blk, fmap(0)), pl.BlockSpec(blk, fmap(1)), pl.BlockSpec(blk, fmap(3)),
                  pl.BlockSpec(blk, bmap(0)), pl.BlockSpec(blk, bmap(2)), pl.BlockSpec(blk, bmap(3)),
                  pl.BlockSpec(lb_logits.shape, c2)] + [pl.BlockSpec(c.shape, c2) for c in consts],
        out_specs=[pl.BlockSpec(blk, fmap(0)), pl.BlockSpec(blk, bmap(0))],
        out_shape=[jax.ShapeDtypeStruct((bsz, t_all, BRANCH_W), F32)] * 2,
        scratch_shapes=[pltpu.VMEM((2, BRANCH_W, BRANCH_W), F32), pltpu.VMEM((2, 4, tm, BRANCH_W), F32)],
        compiler_params=pltpu.CompilerParams(dimension_semantics=("parallel", "arbitrary"),
                                             vmem_limit_bytes=40 * 1024 * 1024),
        name="hgrn2_scan",
    )(p_hg, p_hg, p_hg, p_hg, p_hg, p_hg, lb_logits.astype(F32), *consts)


def _s5_kernel(uf_ref, ub_ref, bb_ref, cc_ref, a_ref, yf_ref, yb_ref, x_ref, h_ref, *, tt, bsz):
    @pl.when(pl.program_id(0) == 0)
    def _():
        h_ref[...] = jnp.zeros_like(h_ref)

    x_ref[0] = _dot(_bf(uf_ref[...]), bb_ref[0])
    x_ref[1] = _dot(_bf(ub_ref[...]), bb_ref[1])

    def step(i, carry):
        hfr, hfi, hbr, hbi = carry
        rf = pl.multiple_of(i * bsz, bsz)
        rb = pl.multiple_of((tt - 1 - i) * bsz, bsz)
        ar, ai = a_ref[0, 0], a_ref[0, 1]
        xr = x_ref[0, pl.ds(rf, bsz), 0:S5_LANES]
        xi = x_ref[0, pl.ds(rf, bsz), S5_LANES:2 * S5_LANES]
        nfr = ar * hfr - ai * hfi + xr
        nfi = ar * hfi + ai * hfr + xi
        x_ref[0, pl.ds(rf, bsz), 0:S5_LANES] = nfr
        x_ref[0, pl.ds(rf, bsz), S5_LANES:2 * S5_LANES] = nfi
        ar, ai = a_ref[1, 0], a_ref[1, 1]
        xr = x_ref[1, pl.ds(rb, bsz), 0:S5_LANES]
        xi = x_ref[1, pl.ds(rb, bsz), S5_LANES:2 * S5_LANES]
        nbr = ar * hbr - ai * hbi + xr
        nbi = ar * hbi + ai * hbr + xi
        x_ref[1, pl.ds(rb, bsz), 0:S5_LANES] = nbr
        x_ref[1, pl.ds(rb, bsz), S5_LANES:2 * S5_LANES] = nbi
        return nfr, nfi, nbr, nbi

    hs = lax.fori_loop(0, tt, step, (h_ref[0, 0], h_ref[0, 1], h_ref[1, 0], h_ref[1, 1]))
    h_ref[0, 0], h_ref[0, 1], h_ref[1, 0], h_ref[1, 1] = hs
    yf_ref[...] = _dot(_bf(x_ref[0]), cc_ref[0])
    yb_ref[...] = _dot(_bf(x_ref[1]), cc_ref[1])


def _s5_matrices(lam_re, lam_im, log_step, b_re, b_im, c_re, c_im):
    lam_re, lam_im = lam_re.astype(F32), lam_im.astype(F32)
    step = jnp.exp(log_step.astype(F32))[..., None]
    mag = jnp.exp(lam_re * step)
    a_re, a_im = mag * jnp.cos(lam_im * step), mag * jnp.sin(lam_im * step)
    den = lam_re * lam_re + lam_im * lam_im
    n_re = a_re - 1.0
    coef_re = (n_re * lam_re + a_im * lam_im) / den
    coef_im = (a_im * lam_re - n_re * lam_im) / den
    b_re, b_im = b_re.astype(F32), b_im.astype(F32)
    bb_re = coef_re[..., None] * b_re - coef_im[..., None] * b_im
    bb_im = coef_re[..., None] * b_im + coef_im[..., None] * b_re
    eye_g = jnp.eye(S5_GROUPS, dtype=F32)
    bb = jnp.stack([bb_re, bb_im], axis=1)
    bb = jnp.einsum('drgph,gk->dghrkp', bb, eye_g).reshape(2, BRANCH_W, 2 * S5_LANES)
    cc = jnp.stack([c_re.astype(F32), -c_im.astype(F32)], axis=0)
    cc = jnp.einsum('rgkp,gj->rjpgk', cc, eye_g).reshape(2 * S5_LANES, BRANCH_W)
    return bb.astype(BF16), cc.astype(BF16), a_re.reshape(2, S5_LANES), a_im.reshape(2, S5_LANES)


def _s5(u_tb, mats, *, bsz, tc, tt):
    bb, cc, a_re, a_im = mats
    n_rows = u_tb.shape[0]
    t_all = n_rows // bsz
    nt, nct = t_all // tt, tc // tt
    a = jnp.broadcast_to(jnp.stack([a_re, a_im], axis=1)[:, :, None, :], (2, 2, bsz, S5_LANES))
    cc2 = jnp.broadcast_to(cc[None], (2,) + cc.shape)
    blk = (tt * bsz, BRANCH_W)
    fmap = lambda s: (s, 0)
    bmap = lambda s: (_bwd_order(s, nct, nt), 0)
    return pl.pallas_call(
        functools.partial(_s5_kernel, tt=tt, bsz=bsz),
        grid=(nt,),
        in_specs=[pl.BlockSpec(blk, fmap), pl.BlockSpec(blk, bmap),
                  pl.BlockSpec(bb.shape, lambda s: (0, 0, 0)), pl.BlockSpec(cc2.shape, lambda s: (0, 0, 0)),
                  pl.BlockSpec(a.shape, lambda s: (0, 0, 0, 0))],
        out_specs=[pl.BlockSpec(blk, fmap), pl.BlockSpec(blk, bmap)],
        out_shape=[jax.ShapeDtypeStruct((n_rows, BRANCH_W), F32)] * 2,
        scratch_shapes=[pltpu.VMEM((2, tt * bsz, 2 * S5_LANES), F32), pltpu.VMEM((2, 2, bsz, S5_LANES), F32)],
        compiler_params=pltpu.CompilerParams(dimension_semantics=("arbitrary",),
                                             vmem_limit_bytes=V7X_VMEM_LIMIT),
        name="s5_scan",
    )(u_tb, u_tb, bb, cc2, a)


def _rope(x, cos, sin_signed):
    w = x.shape[-1]
    lane = lax.broadcasted_iota(jnp.int32, x.shape, 1)
    up = pltpu.roll(x, w - 16, 1)
    dn = pltpu.roll(x, 16, 1)
    return x * cos + jnp.where((lane & 31) < 16, up, dn) * sin_signed


def _attn_kernel(q_ref, kv_ref, cq_ref, sq_ref, ck_ref, sk_ref, qg_ref, kg_ref, bdq_ref, bdk_ref,
                 o_ref, kp_ref, vp_ref, *, tm, tc, nct):
    j = pl.program_id(1)
    t_all = kv_ref.shape[0]
    half = BRANCH_W // 2

    @pl.when(j == 0)
    def _():
        def pairs(x):
            lane = lax.broadcasted_iota(jnp.int32, x.shape, 1)
            sw = pltpu.roll(x, HEAD_DIM, 1)
            return _bf(jnp.where(lane < HEAD_DIM, x, sw)), _bf(jnp.where(lane < HEAD_DIM, sw, x))

        for lo, hi, rope in ((0, tc, False), (tc, t_all, True)):
            k = _head_rms(kv_ref[lo:hi, 0:half], bdk_ref[...], kg_ref[...])
            if rope:
                k = _rope(k, ck_ref[...], sk_ref[...])
            kp_ref[0, lo:hi, :], kp_ref[1, lo:hi, :] = pairs(k)
            v = kv_ref[lo:hi, half:2 * half]
            lane = lax.broadcasted_iota(jnp.int32, v.shape, 1)
            sw = pltpu.roll(v, HEAD_DIM, 1)
            for g in range(2):
                v_lo, v_hi = (v, sw) if g == 0 else (sw, v)
                vp_ref[g, 0, lo:hi, :] = _bf(jnp.where(lane < HEAD_DIM, v_lo, 1.0))
                vp_ref[g, 1, lo:hi, :] = _bf(jnp.where(lane < HEAD_DIM, 1.0, v_hi))

    def attend(q, nk):
        lane = lax.broadcasted_iota(jnp.int32, (tm, half), 1)
        outs = []
        for g in range(2):
            qp = q[:, g * half:(g + 1) * half]
            acc = None
            for hh in range(2):
                sel = (lane < HEAD_DIM) if hh == 0 else (lane >= HEAD_DIM)
                s = _dot_nt(_bf(jnp.where(sel, qp, 0.0)), kp_ref[g, 0:nk, :])
                p = jnp.exp(_bf(s - jnp.max(s, axis=-1, keepdims=True)))
                ol = _dot(p, vp_ref[g, hh, 0:nk, :])
                res = ol / pltpu.roll(ol, HEAD_DIM, 1)
                acc = res if acc is None else jnp.where(sel, res, acc)
            outs.append(acc)
        o_ref[...] = jnp.concatenate(outs, axis=1)

    qn = _head_rms(q_ref[...], bdq_ref[...], qg_ref[...]) * (HEAD_DIM ** -0.5)

    @pl.when(j < nct)
    def _():
        attend(qn, tc)

    @pl.when(j >= nct)
    def _():
        attend(_rope(qn, cq_ref[...], sq_ref[...]), t_all)


def _rope_tables(n_tokens):
    rows = n_tokens // GRID_W
    r, col = np.meshgrid(np.arange(rows), np.arange(GRID_W), indexing='ij')
    axis_dim = HEAD_DIM // 2
    inv = jnp.asarray(ROPE_THETA, F32) ** (-jnp.arange(0, axis_dim, 2, dtype=F32) / axis_dim)
    ang_r = jnp.asarray(r.reshape(-1, 1), F32) * inv
    ang_c = jnp.asarray(col.reshape(-1, 1), F32) * inv
    ang = jnp.concatenate([ang_r, ang_r, ang_c, ang_c], axis=-1)
    sign = np.where((np.arange(HEAD_DIM) % 32) < 16, -1.0, 1.0).astype(np.float32)
    cos, sin = jnp.cos(ang), jnp.sin(ang) * sign
    return jnp.tile(cos, (1, N_HEADS)), jnp.tile(sin, (1, N_HEADS))


def _attention(p_q, p_kv, qn_g, kn_g, cos4, sin4, *, tm, tc):
    bsz, t_all, _ = p_q.shape
    nt, nct = t_all // tm, tc // tm
    half = BRANCH_W // 2
    tok = lambda b, j: (b, j, 0)
    c2 = lambda b, j: (0, 0)
    lat = lambda b, j: (jnp.maximum(j - nct, 0), 0)
    qg = jnp.tile(qn_g.astype(F32), N_HEADS).reshape(1, BRANCH_W)
    kg = jnp.tile(kn_g.astype(F32), 2).reshape(1, half)
    bdq = jnp.asarray(_np_block_diag_ones(BRANCH_W, HEAD_DIM), BF16)
    bdk = jnp.asarray(_np_block_diag_ones(half, HEAD_DIM), BF16)
    cosk, sink = cos4[:, 0:half], sin4[:, 0:half]
    return pl.pallas_call(
        functools.partial(_attn_kernel, tm=tm, tc=tc, nct=nct),
        grid=(bsz, nt),
        in_specs=[pl.BlockSpec((None, tm, BRANCH_W), tok),
                  pl.BlockSpec((None, t_all, BRANCH_W), lambda b, j: (b, 0, 0)),
                  pl.BlockSpec((tm, BRANCH_W), lat), pl.BlockSpec((tm, BRANCH_W), lat),
                  pl.BlockSpec(cosk.shape, c2), pl.BlockSpec(sink.shape, c2),
                  pl.BlockSpec((1, BRANCH_W), c2), pl.BlockSpec((1, half), c2),
                  pl.BlockSpec(bdq.shape, c2), pl.BlockSpec(bdk.shape, c2)],
        out_specs=pl.BlockSpec((None, tm, BRANCH_W), tok),
        out_shape=jax.ShapeDtypeStruct((bsz, t_all, BRANCH_W), F32),
        scratch_shapes=[pltpu.VMEM((2, t_all, half), BF16), pltpu.VMEM((2, 2, t_all, half), BF16)],
        compiler_params=pltpu.CompilerParams(dimension_semantics=("parallel", "arbitrary"),
                                             vmem_limit_bytes=V7X_VMEM_LIMIT),
        name="attention",
    )(p_q, p_kv, cos4, sin4, cosk, sink, qg, kg, bdq, bdk)


def _merge_kernel(x_ref, m_ref, g_ref, dnf_ref, dnb_ref, z_ref, sf_ref, sb_ref, u_ref, hf_ref, hb_ref, hgate_ref,
                  at_ref, dng_ref, hgg_ref, dsk_ref, glu_ref, bd_ref, wg_ref, wb_ref, wo_ref, o_ref):
    x = x_ref[...]
    bd = bd_ref[...]
    h = _bf(_rms(x) * g_ref[...] * (1.0 + m_ref[4:5, :]) + m_ref[3:4, :])
    y_dn = _head_rms(dnf_ref[...] + dnb_ref[...], bd, dng_ref[...]) * _silu(z_ref[...])
    ys = sf_ref[...] + sb_ref[...] + dsk_ref[...] * u_ref[...]
    ab = _dot(_bf(jax.nn.gelu(ys, approximate=True)), glu_ref[...])
    y_s5 = ab[:, 0:BRANCH_W] * _sigmoid(ab[:, BRANCH_W:2 * BRANCH_W])
    y_hg = _head_rms(hf_ref[...] + hb_ref[...], bd, hgg_ref[...]) * _sigmoid(hgate_ref[...])
    ys_all = (y_dn, y_s5, y_hg, at_ref[...])
    acc = jnp.zeros(x.shape, F32)
    for i in range(4):
        acc = acc + _sigmoid(_dot(h, wg_ref[i])) * _dot(_bf(ys_all[i]), wb_ref[i])
    o_ref[...] = x + m_ref[5:6, :] * _dot(_bf(acc), wo_ref[...])


def _merge(xj, mods, g, br, params, *, tm, nct, j0):
    bsz, t_all, _ = xj.shape
    n_tiles = t_all // tm - j0
    tok = lambda b, j: (b, j + j0, 0)
    col = lambda c: (lambda b, j: (b, j + j0, c))
    mod_map = lambda b, j: (jnp.where(j + j0 < nct, bsz, b), 0, 0)
    c2 = lambda b, j: (0, 0)
    c3 = lambda b, j: (0, 0, 0)
    tb = (None, tm, BRANCH_W)
    tmaj, tmaj_map = (tm, BRANCH_W), (lambda b, j: (j + j0, b))
    dn_g, hg_g, d_skip, w_glu, wg, wb, wo = params
    bd = jnp.asarray(_np_block_diag_ones(BRANCH_W, HEAD_DIM), BF16)
    return pl.pallas_call(
        _merge_kernel,
        grid=(bsz, n_tiles),
        in_specs=[pl.BlockSpec((None, tm, D_MODEL), tok),
                  pl.BlockSpec((None, N_MOD, D_MODEL), mod_map),
                  pl.BlockSpec((1, D_MODEL), c2),
                  pl.BlockSpec(tb, tok), pl.BlockSpec(tb, tok), pl.BlockSpec(tb, col(3)),
                  pl.BlockSpec(tmaj, tmaj_map), pl.BlockSpec(tmaj, tmaj_map), pl.BlockSpec(tmaj, tmaj_map),
                  pl.BlockSpec(tb, tok), pl.BlockSpec(tb, tok), pl.BlockSpec(tb, col(4)),
                  pl.BlockSpec(tb, tok),
                  pl.BlockSpec((1, BRANCH_W), c2), pl.BlockSpec((1, BRANCH_W), c2), pl.BlockSpec((1, BRANCH_W), c2),
                  pl.BlockSpec(w_glu.shape, c2), pl.BlockSpec(bd.shape, c2),
                  pl.BlockSpec(wg.shape, c3), pl.BlockSpec(wb.shape, c3), pl.BlockSpec(wo.shape, c2)],
        out_specs=pl.BlockSpec((None, tm, D_MODEL), lambda b, j: (b, j, 0)),
        out_shape=jax.ShapeDtypeStruct((bsz, n_tiles * tm, D_MODEL), F32),
        compiler_params=pltpu.CompilerParams(dimension_semantics=("parallel", "parallel"),
                                             vmem_limit_bytes=V7X_VMEM_LIMIT),
        name="branch_merge",
    )(xj, mods, g.reshape(1, D_MODEL), br['dn_f'], br['dn_b'], br['p_dn'], br['s5_f'], br['s5_b'], br['p_s5'],
      br['hg_f'], br['hg_b'], br['p_hg'], br['at'], dn_g, hg_g, d_skip, w_glu, bd, wg, wb, wo)


def kernel(x, c, ctx, c_ctx, ada_w, ada_b, norm_g, ffn_w1, ffn_w3, ffn_w2, w_in, dn_conv, dn_a_log, dn_dt_bias,
           dn_norm_g, s5_lam_re, s5_lam_im, s5_log_step, s5_b_re, s5_b_im, s5_c_re, s5_c_im, s5_d, s5_glu,
           hg_lb_logits, hg_norm_g, at_qn_g, at_kn_g, w_branch, w_out, final_g):
    bsz, tl, _ = x.shape
    tc = ctx.shape[1]
    t_all = tc + tl
    depth = ada_w.shape[0]
    tm = 256 if (tc % 256 == 0 and tl % 256 == 0) else 128
    assert tc % tm == 0 and tl % tm == 0 and tm % CHUNK == 0 and bsz % 8 == 0
    nct = tc // tm
    tm_joint = 768 if (t_all % 768 == 0 and tc < 768) else tm
    tm_latent = next((t for t in (1024, 512) if tl % t == 0), tm)
    d_ff = ffn_w1.shape[-1]
    n_ff = d_ff // FF_CHUNK

    n_rows = -(-(bsz + 1) // 8) * 8
    cond = jnp.zeros((n_rows, D_MODEL), F32).at[0:bsz].set(c.astype(F32)).at[bsz].set(c_ctx.astype(F32))
    mods_all = _ada_mods(cond, ada_w.astype(F32), ada_b.astype(F32))
    cos4, sin4 = _rope_tables(tl)
    xj = x.astype(F32)

    def ffn_weights(l, i):
        w1 = ffn_w1[l, i].astype(BF16).reshape(D_MODEL, n_ff, FF_CHUNK).transpose(1, 0, 2)
        w3 = ffn_w3[l, i].astype(BF16).reshape(D_MODEL, n_ff, FF_CHUNK).transpose(1, 0, 2)
        w2 = ffn_w2[l, i].astype(BF16)
        return w1, w3, w2

    for l in range(depth):
        last = l == depth - 1
        mods = mods_all[l]
        if l == 0:
            xj = _ffn_half_step(xj, mods, norm_g[l, 0], *ffn_weights(l, 0), mod_base=0, tm=tm, tc=tc,
                                ctx=ctx.astype(F32))
        else:
            xj = _ffn_half_step(xj, mods, norm_g[l, 0], *ffn_weights(l, 0), mod_base=0, tm=tm_joint, tc=tc)

        wl = w_in[l]
        w_dn = jnp.concatenate([wl[:, 0:1040], jnp.zeros((D_MODEL, 112), wl.dtype)], axis=1)
        ws = [w.astype(BF16) for w in (w_dn, wl[:, 1040:1296], wl[:, 1296:2576], wl[:, 2576:2832], wl[:, 2832:3088])]
        p_dn, p_s5, p_hg, p_aq, p_kv = _input_projection(xj, mods, norm_g[l, 1], ws, tm=tm, nct=nct)

        dn_f, dn_b = _deltanet(p_dn, dn_conv[l], dn_a_log[l], dn_dt_bias[l], tm=tm, tc=tc)
        hg_f, hg_b = _hgrn2(p_hg, hg_lb_logits, l, tm=tm, tc=tc)
        mats = _s5_matrices(s5_lam_re[l], s5_lam_im[l], s5_log_step[l], s5_b_re[l], s5_b_im[l], s5_c_re[l], s5_c_im[l])
        s5_f, s5_b = _s5(p_s5.reshape(t_all * bsz, BRANCH_W), mats, bsz=bsz, tc=tc, tt=CHUNK)
        s5_f = s5_f.reshape(t_all, bsz * BRANCH_W)
        s5_b = s5_b.reshape(t_all, bsz * BRANCH_W)
        y_at = _attention(p_aq, p_kv, at_qn_g[l], at_kn_g[l], cos4, sin4, tm=tm, tc=tc)

        br = dict(dn_f=dn_f, dn_b=dn_b, p_dn=p_dn, s5_f=s5_f, s5_b=s5_b, p_s5=p_s5, hg_f=hg_f, hg_b=hg_b,
                  p_hg=p_hg, at=y_at)
        params = (jnp.tile(dn_norm_g[l].astype(F32), N_HEADS).reshape(1, BRANCH_W),
                  jnp.tile(hg_norm_g[l].astype(F32), N_HEADS).reshape(1, BRANCH_W),
                  s5_d[l].astype(F32).reshape(1, BRANCH_W),
                  s5_glu[l].astype(BF16),
                  wl[:, 3088:].astype(BF16).reshape(D_MODEL, 4, D_MODEL).transpose(1, 0, 2),
                  w_branch[l].astype(BF16),
                  w_out[l].astype(BF16))
        xj = _merge(xj, mods, norm_g[l, 1], br, params, tm=tm, nct=nct, j0=nct if last else 0)
        if last:
            xj = _ffn_half_step(xj, mods, norm_g[l, 2], *ffn_weights(l, 1), mod_base=6, tm=tm_latent, tc=0,
                                final_g=final_g.astype(F32))
        else:
            xj = _ffn_half_step(xj, mods, norm_g[l, 2], *ffn_weights(l, 1), mod_base=6, tm=tm_joint, tc=tc)
    return xj.astype(x.dtype)
```

```python
import functools
import math

import numpy as np
import jax
import jax.numpy as jnp
from jax import lax
from jax.experimental import pallas as pl
from jax.experimental.pallas import tpu as pltpu

F32 = jnp.float32
BF16 = jnp.bfloat16

D_MODEL = 1024
N_MOD = 9
BRANCH_W = 256
HEAD_DIM = 64
N_HEADS = 4
CHUNK = 64
HG_BLK = 16
EPS = 1e-6
LOG_FLOOR = 1e-30
GRID_W = 64
ROPE_THETA = 10000.0
FF_CHUNK = 256
S5_GROUPS = 16
S5_GROUP = 16
S5_STATE = 64
S5_LANES = S5_GROUPS * S5_STATE
NEG_BIG = -1e30
V7X_VMEM_LIMIT = 56 * 1024 * 1024


def _bf(x):
    return x.astype(BF16)


def _dot(a, b):
    return jnp.dot(a, b, preferred_element_type=F32)


def _dot_nt(a, b):
    return lax.dot_general(a, b, (((1,), (1,)), ((), ())), preferred_element_type=F32)


def _dot_tn(a, b):
    return lax.dot_general(a, b, (((0,), (0,)), ((), ())), preferred_element_type=F32)


def _split2(x):
    hi = _bf(x)
    lo = _bf(x - hi.astype(F32))
    return hi, lo


def _split3(x):
    x1 = _bf(x)
    r1 = x - x1.astype(F32)
    x2 = _bf(r1)
    x3 = _bf(r1 - x2.astype(F32))
    return x1, x2, x3


def _dot_x2(x, m):
    hi, lo = _split2(x)
    return _dot(hi, m) + _dot(lo, m)


def _dot_x3(x, m):
    x1, x2, x3 = _split3(x)
    return _dot(x1, m) + _dot(x2, m) + _dot(x3, m)


def _dot3_left(m, x):
    x1, x2, x3 = _split3(x)
    return _dot(m, x1) + _dot(m, x2) + _dot(m, x3)


def _dot_hp(a, b):
    a_hi, a_lo = _split2(a)
    b_hi, b_lo = _split2(b)
    return _dot(a_hi, b_hi) + _dot(a_hi, b_lo) + _dot(a_lo, b_hi)


def _sigmoid(x):
    return jax.nn.sigmoid(x)


def _silu(x):
    return x * jax.nn.sigmoid(x)


def _rms(x):
    return x * lax.rsqrt(jnp.mean(x * x, axis=-1, keepdims=True) + EPS)


def _head_rms(x, bd_ones, gain):
    ss = _dot_x2(x * x, bd_ones) * (1.0 / HEAD_DIM)
    return x * lax.rsqrt(ss + EPS) * gain


def _stack4(x):
    return jnp.concatenate([x, x, x, x], axis=0)


def _unstack4(x):
    n = x.shape[0] // 4
    return x[0:n] + x[n:2 * n] + x[2 * n:3 * n] + x[3 * n:4 * n]


def _np_block_diag_ones(n, blk):
    i = np.arange(n)
    return (i[:, None] // blk == i[None, :] // blk).astype(np.float32)


def _np_tri(n, blk, lower, strict):
    i = np.arange(n)
    same = i[:, None] // blk == i[None, :] // blk
    a, b = i[:, None] % blk, i[None, :] % blk
    if lower:
        t = (a > b) if strict else (a >= b)
    else:
        t = (a < b) if strict else (a <= b)
    return (same & t).astype(np.float32)


def _np_doubling_masks(n, blk):
    i = np.arange(n)
    r, c = i[:, None], i[None, :]
    out = [(r // 2 == c // 2)]
    s = 2
    while s < blk:
        out.append((r // (2 * s) == c // (2 * s)) & (r // s != c // s))
        s *= 2
    return np.stack(out).astype(np.float32)


def _bwd_order(s, n_ctx, n_all):
    return jnp.where(s < n_ctx, n_ctx - 1 - s, n_all + n_ctx - 1 - s)


def _ada_kernel(c_ref, w_ref, b_ref, o_ref):
    s = _silu(c_ref[...])
    w = w_ref[...]
    s_hi, s_lo = _split2(s)
    w_hi, w_lo = _split2(w)
    o_ref[...] = _dot(s_hi, w_hi) + _dot(s_hi, w_lo) + _dot(s_lo, w_hi) + b_ref[...]


def _ada_mods(cond, ada_w, ada_b):
    n_layers = ada_w.shape[0]
    r = cond.shape[0]
    n_out = ada_w.shape[2]
    tn = 1152
    out = pl.pallas_call(
        _ada_kernel,
        grid=(n_layers, n_out // tn),
        in_specs=[
            pl.BlockSpec((r, D_MODEL), lambda l, n: (0, 0)),
            pl.BlockSpec((None, D_MODEL, tn), lambda l, n: (l, 0, n)),
            pl.BlockSpec((None, 1, tn), lambda l, n: (l, 0, n)),
        ],
        out_specs=pl.BlockSpec((None, r, tn), lambda l, n: (l, 0, n)),
        out_shape=jax.ShapeDtypeStruct((n_layers, r, n_out), F32),
        compiler_params=pltpu.CompilerParams(dimension_semantics=("parallel", "parallel"),
                                             vmem_limit_bytes=40 * 1024 * 1024),
        name="ada_mods",
    )(cond, ada_w, ada_b.reshape(n_layers, 1, n_out))
    return out.reshape(n_layers, r, N_MOD, D_MODEL)


def _ffn_kernel(*refs, mod_base, final, n_ctx_tiles, head_rows):
    refs = list(refs)
    o_ref = refs.pop()
    x_ref = refs.pop(0)
    c_ref = refs.pop(0) if n_ctx_tiles else None
    m_ref = refs.pop(0)
    mc_ref = refs.pop(0) if head_rows else None
    g_ref, w1_ref, w3_ref, w2_ref = refs[0:4]
    fg_ref = refs[4] if final else None
    x = x_ref[...]
    if c_ref is not None:
        x = jnp.where(pl.program_id(1) < n_ctx_tiles, c_ref[...], x)

    def mod(i, head):
        row = m_ref[mod_base + i:mod_base + i + 1, :]
        if head:
            row = jnp.where(pl.program_id(1) == 0, mc_ref[mod_base + i:mod_base + i + 1, :], row)
        return row

    def rows(fn):
        if not head_rows:
            return fn(slice(None), False)
        return jnp.concatenate([fn(slice(0, head_rows), True), fn(slice(head_rows, None), False)], axis=0)

    y = _rms(x) * g_ref[...]
    h = _bf(rows(lambda r, hd: y[r] * (1.0 + mod(1, hd)) + mod(0, hd)))
    us = []
    for c in range(w1_ref.shape[0]):
        a = _dot(h, w1_ref[c])
        us.append(_bf(_silu(a) * _dot(h, w3_ref[c])))
    acc = _dot(jnp.concatenate(us, axis=1), w2_ref[...])
    out = x + 0.5 * rows(lambda r, hd: mod(2, hd) * acc[r])
    if final:
        out = _rms(out) * fg_ref[...]
    o_ref[...] = out


def _ffn_half_step(xj, mods, g, w1c, w3c, w2c, *, mod_base, tm, tc, final_g=None, ctx=None):
    bsz, t_all, _ = xj.shape
    final = final_g is not None
    if ctx is not None:
        t_all += ctx.shape[1]
    assert t_all % tm == 0 and (tc % tm == 0 or (tc < tm and tc % 8 == 0 and ctx is None))
    nct = tc // tm
    head_rows = tc if tc % tm else 0

    def tok_map(b, j):
        return (b, j, 0)

    def mod_map(b, j):
        return (jnp.where(j < nct, bsz, b), 0, 0)

    const3 = lambda b, j: (0, 0, 0)
    const2 = lambda b, j: (0, 0)
    once = dict(pipeline_mode=pl.Buffered(1))
    if ctx is None:
        tok_specs, tok_args = [pl.BlockSpec((None, tm, D_MODEL), tok_map)], [xj]
    else:
        tok_specs = [pl.BlockSpec((None, tm, D_MODEL), lambda b, j: (b, jnp.maximum(j - nct, 0), 0)),
                     pl.BlockSpec((None, tm, D_MODEL), lambda b, j: (b, jnp.minimum(j, nct - 1), 0))]
        tok_args = [xj, ctx]
    in_specs = tok_specs + [pl.BlockSpec((None, N_MOD, D_MODEL), mod_map)]
    args = tok_args + [mods]
    if head_rows:
        in_specs.append(pl.BlockSpec((None, N_MOD, D_MODEL), lambda b, j: (bsz, 0, 0)))
        args.append(mods)
    in_specs += [
        pl.BlockSpec((1, D_MODEL), const2),
        pl.BlockSpec(w1c.shape, const3, **once),
        pl.BlockSpec(w3c.shape, const3, **once),
        pl.BlockSpec(w2c.shape, const2, **once),
    ]
    args += [g.reshape(1, D_MODEL), w1c, w3c, w2c]
    if final:
        in_specs.append(pl.BlockSpec((1, D_MODEL), const2))
        args.append(final_g.reshape(1, D_MODEL))
    return pl.pallas_call(
        functools.partial(_ffn_kernel, mod_base=mod_base, final=final, n_ctx_tiles=nct if ctx is not None else 0,
                          head_rows=head_rows),
        grid=(bsz, t_all // tm),
        in_specs=in_specs,
        out_specs=pl.BlockSpec((None, tm, D_MODEL), tok_map),
        out_shape=jax.ShapeDtypeStruct((bsz, t_all, D_MODEL), F32),
        compiler_params=pltpu.CompilerParams(dimension_semantics=("parallel", "parallel"),
                                             vmem_limit_bytes=V7X_VMEM_LIMIT),
        name="ffn_half_step",
    )(*args)


def _inproj_kernel(x_ref, m_ref, g_ref, wdn_ref, ws5_ref, whg_ref, waq_ref, wkv_ref,
                   dn_ref, s5_ref, hg_ref, aq_ref, kv_ref):
    x = x_ref[...]
    h = _bf(_rms(x) * g_ref[...] * (1.0 + m_ref[4:5, :]) + m_ref[3:4, :])
    dn_ref[...] = _dot(h, wdn_ref[...])
    s5_ref[...] = _dot(h, ws5_ref[...])
    hg_ref[...] = _dot(h, whg_ref[...])
    aq_ref[...] = _dot(h, waq_ref[...])
    kv_ref[...] = _dot(h, wkv_ref[...])


def _input_projection(xj, mods, g, ws, *, tm, nct):
    bsz, t_all, _ = xj.shape
    tok_map = lambda b, j: (b, j, 0)
    mod_map = lambda b, j: (jnp.where(j < nct, bsz, b), 0, 0)
    const2 = lambda b, j: (0, 0)
    widths = [w.shape[1] for w in ws]
    return pl.pallas_call(
        _inproj_kernel,
        grid=(bsz, t_all // tm),
        in_specs=[pl.BlockSpec((None, tm, D_MODEL), tok_map),
                  pl.BlockSpec((None, N_MOD, D_MODEL), mod_map),
                  pl.BlockSpec((1, D_MODEL), const2)]
                 + [pl.BlockSpec(w.shape, const2) for w in ws],
        out_specs=[pl.BlockSpec((tm, n), lambda b, j: (j, b)) if i == 1 else pl.BlockSpec((None, tm, n), tok_map)
                   for i, n in enumerate(widths)],
        out_shape=[jax.ShapeDtypeStruct((t_all, bsz * n) if i == 1 else (bsz, t_all, n), F32)
                   for i, n in enumerate(widths)],
        compiler_params=pltpu.CompilerParams(dimension_semantics=("parallel", "parallel"),
                                             vmem_limit_bytes=V7X_VMEM_LIMIT),
        name="input_projection",
    )(xj, mods, g.reshape(1, D_MODEL), *ws)


def _dn_prep_kernel(p_ref, prev_ref, next_ref, cw_ref, al_ref, dtb_ref, e_ref, tril_ref, triu_ref, bd_ref,
                    q_ref, k_ref, v_ref, gb_ref, *, tm, nct, nt):
    j = pl.program_id(1)
    first = jnp.logical_or(j == 0, j == nct)
    last = jnp.logical_or(j == nct - 1, j == nt - 1)
    x = p_ref[:, 0:3 * BRANCH_W]
    pv = prev_ref[...] * jnp.where(first, 0.0, 1.0)
    nx = next_ref[...] * jnp.where(last, 0.0, 1.0)
    ext = jnp.concatenate([pv, x, nx], axis=0)
    n = tm + 16
    acc = jnp.zeros((tm, 3 * BRANCH_W), F32)
    for t in range(5):
        sh = (2 - t) % n
        r = ext if sh == 0 else pltpu.roll(ext, sh, 0)
        acc = acc + r[8:8 + tm] * cw_ref[t:t + 1, :]
    h = _silu(acc)
    bd = bd_ref[...]

    def l2n(t):
        return t * lax.rsqrt(_dot_x2(t * t, bd) + EPS)

    q_ref[...] = l2n(h[:, 0:BRANCH_W]) * (HEAD_DIM ** -0.5)
    k_ref[...] = l2n(h[:, BRANCH_W:2 * BRANCH_W])
    v_ref[...] = h[:, 2 * BRANCH_W:3 * BRANCH_W]

    ab = p_ref[:, 4 * BRANCH_W:4 * BRANCH_W + 128]
    z = ab + dtb_ref[...]
    softplus = jnp.maximum(z, 0.0) + jnp.log1p(jnp.exp(-jnp.abs(z)))
    g = -jnp.exp(al_ref[...]) * softplus
    lane = lax.broadcasted_iota(jnp.int32, ab.shape, 1)
    g = jnp.where(lane < 8, g, 0.0)
    cs_f = _dot3_left(tril_ref[...], g)
    cs_b = _dot3_left(triu_ref[...], g)
    comp = jnp.where(lane < 4, cs_f, jnp.where(lane < 8, cs_b, jnp.where(lane < 16, _sigmoid(ab), 0.0)))
    gb_ref[...] = _dot_x3(comp, e_ref[...])


def _dn_chunks_local(items, bd, it, lvl_ref):
    def stk(x):
        return _stack4(_bf(x)) * bd

    ks = [stk(c[1]) for c in items]
    gq = [_dot_nt(_bf(jnp.concatenate([c[1] * c[4], c[0]], axis=0)), x) for c, x in zip(items, ks)]
    decays, a_s = [], []
    for (q, k, v, gc, be, neg_incl, tri_strict, fwd), g in zip(items, gq):
        col_gc = jnp.sum(gc * it, axis=0, keepdims=True)
        decay = jnp.exp(gc - col_gc + neg_incl)
        decays.append(decay)
        a_s.append(g[0:CHUNK] * decay * tri_strict)
    ts = [it - a * lvl_ref[0] for a in a_s]
    for lv in range(1, lvl_ref.shape[0]):
        xs = [_dot(_bf(a * lvl_ref[lv]), stk(t)) for a, t in zip(a_s, ts)]
        ts = [t - _dot(_bf(t), stk(x)) for t, x in zip(ts, xs)]
    egcs = [jnp.exp(c[3]) for c in items]
    uws = [_dot(_bf(t), jnp.concatenate([stk(c[2] * c[4]), stk(c[1] * c[4] * e)], axis=1))
           for t, c, e in zip(ts, items, egcs)]
    out = []
    for (q, k, v, gc, be, _, _, fwd), uw, g, dc, egc in zip(items, uws, gq, decays, egcs):
        total = gc[CHUNK - 1:CHUNK, :] if fwd else gc[0:1, :]
        out.append((uw[:, 0:BRANCH_W], _bf(uw[:, BRANCH_W:2 * BRANCH_W]), _bf(g[CHUNK:2 * CHUNK] * dc),
                    _bf(q * egc), _bf(k * jnp.exp(total - gc)), jnp.exp(total)))
    return out


def _dn_chunk_state(local, s_prev, bd, bd_f32):
    u, w, attn, q_dec, k_dec, g_last = local
    ws = _dot(jnp.concatenate([w, q_dec], axis=0), _bf(s_prev))
    v_new = _bf(u - ws[0:CHUNK])
    o = ws[CHUNK:2 * CHUNK] + _dot(attn, _stack4(v_new) * bd)
    s_new = s_prev * g_last + _dot_tn(k_dec, v_new) * bd_f32
    return o, s_new


def _dn_scan_kernel(qf_ref, kf_ref, vf_ref, gcf_ref, bef_ref, qb_ref, kb_ref, vb_ref, gcb_ref, beb_ref,
                    bdf_ref, li_ref, ls_ref, ui_ref, us_ref, it_ref, lvl_ref, bd_ref,
                    of_ref, ob_ref, s_ref):
    @pl.when(pl.program_id(1) == 0)
    def _():
        s_ref[...] = jnp.zeros_like(s_ref)

    bd, bd_f32, it = bd_ref[...], bdf_ref[...], it_ref[...]
    n_sub = qf_ref.shape[0] // CHUNK
    dirs = ((qf_ref, kf_ref, vf_ref, gcf_ref, bef_ref, li_ref, ls_ref, of_ref, True),
            (qb_ref, kb_ref, vb_ref, gcb_ref, beb_ref, ui_ref, us_ref, ob_ref, False))
    items = []
    for q_ref, k_ref, v_ref, gc_ref, be_ref, ti_ref, ts_ref, _, fwd in dirs:
        for g in range(n_sub):
            r = slice(g * CHUNK, (g + 1) * CHUNK)
            items.append((q_ref[r, :], k_ref[r, :], v_ref[r, :], gc_ref[r, :], be_ref[r, :],
                          ti_ref[...], ts_ref[...], fwd))
    local = _dn_chunks_local(items, bd, it, lvl_ref)
    s_f, s_b = s_ref[0], s_ref[1]
    for i in range(n_sub):
        gb = n_sub - 1 - i
        o, s_f = _dn_chunk_state(local[i], s_f, bd, bd_f32)
        of_ref[i * CHUNK:(i + 1) * CHUNK, :] = o
        o, s_b = _dn_chunk_state(local[n_sub + gb], s_b, bd, bd_f32)
        ob_ref[gb * CHUNK:(gb + 1) * CHUNK, :] = o
    s_ref[0] = s_f
    s_ref[1] = s_b


def _deltanet(p_dn, conv_w, a_log, dt_bias, *, tm, tc):
    bsz, t_all, _ = p_dn.shape
    nt, nct = t_all // tm, tc // tm
    al = jnp.zeros((1, 128), F32).at[0, 0:8].set(a_log.reshape(8).astype(F32))
    dtb = jnp.zeros((1, 128), F32).at[0, 0:8].set(dt_bias.reshape(8).astype(F32))
    e = np.zeros((128, 4 * BRANCH_W), np.float32)
    for jj in range(4):
        for h in range(N_HEADS):
            e[4 * jj + h, jj * BRANCH_W + h * HEAD_DIM: jj * BRANCH_W + (h + 1) * HEAD_DIM] = 1.0
    consts = [jnp.asarray(e, BF16),
              jnp.asarray(_np_tri(tm, CHUNK, True, False), BF16),
              jnp.asarray(_np_tri(tm, CHUNK, False, False), BF16),
              jnp.asarray(_np_block_diag_ones(BRANCH_W, HEAD_DIM), BF16)]
    tok = lambda b, j: (b, j, 0)
    c2 = lambda b, j: (0, 0)
    hb = tm // 8
    q, k, v, gb = pl.pallas_call(
        functools.partial(_dn_prep_kernel, tm=tm, nct=nct, nt=nt),
        grid=(bsz, nt),
        in_specs=[pl.BlockSpec((None, tm, p_dn.shape[2]), tok),
                  pl.BlockSpec((None, 8, 3 * BRANCH_W), lambda b, j: (b, jnp.maximum(j * hb - 1, 0), 0)),
                  pl.BlockSpec((None, 8, 3 * BRANCH_W), lambda b, j: (b, jnp.minimum((j + 1) * hb, nt * hb - 1), 0)),
                  pl.BlockSpec(conv_w.shape, c2), pl.BlockSpec((1, 128), c2), pl.BlockSpec((1, 128), c2)]
                 + [pl.BlockSpec(c.shape, c2) for c in consts],
        out_specs=[pl.BlockSpec((None, tm, BRANCH_W), tok)] * 3 + [pl.BlockSpec((None, tm, 4 * BRANCH_W), tok)],
        out_shape=[jax.ShapeDtypeStruct((bsz, t_all, BRANCH_W), F32)] * 3
                  + [jax.ShapeDtypeStruct((bsz, t_all, 4 * BRANCH_W), F32)],
        compiler_params=pltpu.CompilerParams(dimension_semantics=("parallel", "parallel"),
                                             vmem_limit_bytes=40 * 1024 * 1024),
        name="deltanet_prep",
    )(p_dn, p_dn, p_dn, conv_w.astype(F32), al, dtb, *consts)

    nc, ncc = nt, nct
    n4 = N_HEADS * CHUNK
    neg = lambda tri: (1.0 - tri) * NEG_BIG
    per_head = lambda m: np.tile(m, (1,) * (m.ndim - 1) + (N_HEADS,))
    masks = [jnp.asarray(_np_block_diag_ones(n4, CHUNK), F32),
             jnp.asarray(per_head(neg(_np_tri(CHUNK, CHUNK, True, False))), F32),
             jnp.asarray(per_head(_np_tri(CHUNK, CHUNK, True, True)), F32),
             jnp.asarray(per_head(neg(_np_tri(CHUNK, CHUNK, False, False))), F32),
             jnp.asarray(per_head(_np_tri(CHUNK, CHUNK, False, True)), F32),
             jnp.asarray(per_head(np.eye(CHUNK, dtype=np.float32)), F32),
             jnp.asarray(per_head(_np_doubling_masks(CHUNK, CHUNK)), F32),
             jnp.asarray(_np_block_diag_ones(n4, CHUNK), BF16)]
    cmap = lambda m: (lambda b, s: (0,) * m.ndim)
    blk = (None, tm, BRANCH_W)
    fmap = lambda col: (lambda b, s: (b, s, col))
    bmap = lambda col: (lambda b, s: (b, _bwd_order(s, ncc, nc), col))
    o_f, o_b = pl.pallas_call(
        _dn_scan_kernel,
        grid=(bsz, nc),
        in_specs=[pl.BlockSpec(blk, fmap(0))] * 3 + [pl.BlockSpec(blk, fmap(0)), pl.BlockSpec(blk, fmap(2))]
                 + [pl.BlockSpec(blk, bmap(0))] * 3 + [pl.BlockSpec(blk, bmap(1)), pl.BlockSpec(blk, bmap(3))]
                 + [pl.BlockSpec(m.shape, cmap(m)) for m in masks],
        out_specs=[pl.BlockSpec(blk, fmap(0)), pl.BlockSpec(blk, bmap(0))],
        out_shape=[jax.ShapeDtypeStruct((bsz, t_all, BRANCH_W), F32)] * 2,
        scratch_shapes=[pltpu.VMEM((2, n4, n4), F32)],
        compiler_params=pltpu.CompilerParams(dimension_semantics=("parallel", "arbitrary"),
                                             vmem_limit_bytes=40 * 1024 * 1024),
        name="deltanet_scan",
    )(q, k, v, gb, gb, q, k, v, gb, gb, *masks)
    return o_f, o_b


def _hg_tile_prepare(q, f, v, lb, tri_incl, tri_excl, tmp_ref):
    qs = _silu(q)
    logf = jnp.log(jnp.maximum(lb + (1.0 - lb) * _sigmoid(f), LOG_FLOOR))
    kk = (1.0 - lb) * _sigmoid(-f)
    bl = _dot3_left(tri_incl, logf)
    suf = _dot3_left(tri_excl, logf)
    tmp_ref[0], tmp_ref[1], tmp_ref[2], tmp_ref[3] = qs, bl - jnp.log(kk), v, bl
    return _bf(qs * jnp.exp(bl)), _bf(kk * jnp.exp(suf)), _bf(v), bl


def _hg_block_pairs(tmp_ref, r0, bd_ones, fwd):
    io = lax.broadcasted_iota(jnp.int32, (HG_BLK, BRANCH_W), 0)
    q_b = _bf(tmp_ref[0, r0:r0 + HG_BLK, :])
    b_b = tmp_ref[3, r0:r0 + HG_BLK, :]
    rows = []
    for i in range(HG_BLK):
        c_s = jnp.broadcast_to(tmp_ref[1, r0 + i:r0 + i + 1, :], (HG_BLK, BRANCH_W))
        vis = (io >= i) if fwd else (io <= i)
        rows.append(q_b * jnp.exp(_bf(jnp.where(vis, b_b - c_s, NEG_BIG))))
    se = _dot(jnp.concatenate(rows, axis=0), bd_ones)
    out = None
    for i in range(HG_BLK):
        term = se[i * HG_BLK:(i + 1) * HG_BLK] * jnp.broadcast_to(tmp_ref[2, r0 + i:r0 + i + 1, :], (HG_BLK, BRANCH_W))
        out = term if out is None else out + term
    return out


def _hg_scan_kernel(qf_ref, ff_ref, vf_ref, qb_ref, fb_ref, vb_ref, lg_ref, bdo_ref, bdm_ref,
                    tli_ref, tls_ref, tui_ref, tus_ref, of_ref, ob_ref, st_ref, tmp_ref, *, layer):
    @pl.when(pl.program_id(1) == 0)
    def _():
        st_ref[...] = jnp.zeros_like(st_ref)

    n_layers = lg_ref.shape[0]
    mx = lg_ref[0:1, :]
    for i in range(1, n_layers):
        mx = jnp.maximum(mx, lg_ref[i:i + 1, :])
    den = jnp.zeros_like(mx)
    num = jnp.zeros_like(mx)
    for i in range(n_layers):
        e = jnp.exp(lg_ref[i:i + 1, :] - mx)
        den = den + e
        if 1 <= i <= layer:
            num = num + e
    lb = num / den
    bdo, bdm = bdo_ref[...], bdm_ref[...]
    n_blk = qf_ref.shape[0] // HG_BLK
    prep = (_hg_tile_prepare(qf_ref[...], ff_ref[...], vf_ref[...], lb, tli_ref[...], tus_ref[...], tmp_ref.at[0]),
            _hg_tile_prepare(qb_ref[...], fb_ref[...], vb_ref[...], lb, tui_ref[...], tls_ref[...], tmp_ref.at[1]))
    pairs = [[_hg_block_pairs(tmp_ref.at[d], i * HG_BLK, bdo, d == 0) for i in range(n_blk)] for d in range(2)]
    st = [st_ref[0], st_ref[1]]
    for step in range(n_blk):
        for d, o_ref in ((0, of_ref), (1, ob_ref)):
            i = step if d == 0 else n_blk - 1 - step
            r = slice(i * HG_BLK, (i + 1) * HG_BLK)
            q_t, k_t, v_b, bl = prep[d]
            o_ref[r, :] = pairs[d][i] + _dot_nt(q_t[r], _bf(st[d]))
            last = (i + 1) * HG_BLK - 1 if d == 0 else i * HG_BLK
            st[d] = st[d] * jnp.exp(bl[last:last + 1, :]) + _dot_tn(v_b[r], k_t[r]) * bdm
    st_ref[0], st_ref[1] = st


def _hgrn2(p_hg, lb_logits, layer, *, tm, tc):
    bsz, t_all, _ = p_hg.shape
    nc, ncc = t_all // tm, tc // tm
    consts = [jnp.asarray(_np_block_diag_ones(BRANCH_W, HEAD_DIM), BF16),
              jnp.asarray(_np_block_diag_ones(BRANCH_W, HEAD_DIM), F32),
              jnp.asarray(_np_tri(tm, HG_BLK, True, False), BF16),
              jnp.asarray(_np_tri(tm, HG_BLK, True, True), BF16),
              jnp.asarray(_np_tri(tm, HG_BLK, False, False), BF16),
              jnp.asarray(_np_tri(tm, HG_BLK, False, True), BF16)]
    blk = (None, tm, BRANCH_W)
    c2 = lambda b, s: (0, 0)
    fmap = lambda col: (lambda b, s: (b, s, col))
    bmap = lambda col: (lambda b, s: (b, _bwd_order(s, ncc, nc), col))
    return pl.pallas_call(
        functools.partial(_hg_scan_kernel, layer=layer),
        grid=(bsz, nc),
        in_specs=[pl.BlockSpec(blk, fmap(0)), pl.BlockSpec(blk, fmap(1)), pl.BlockSpec(blk, fmap(3)),
                  pl.BlockSpec(blk, bmap(0)), pl.BlockSpec(blk, bmap(2)), pl.BlockSpec(blk, bmap(3)),
                  pl.BlockSpec(lb_logits.shape, c2)] + [pl.BlockSpec(c.shape, c2) for c in consts],
        out_specs=[pl.BlockSpec(blk, fmap(0)), pl.BlockSpec(blk, bmap(0))],
        out_shape=[jax.ShapeDtypeStruct((bsz, t_all, BRANCH_W), F32)] * 2,
        scratch_shapes=[pltpu.VMEM((2, BRANCH_W, BRANCH_W), F32), pltpu.VMEM((2, 4, tm, BRANCH_W), F32)],
        compiler_params=pltpu.CompilerParams(dimension_semantics=("parallel", "arbitrary"),
                                             vmem_limit_bytes=40 * 1024 * 1024),
        name="hgrn2_scan",
    )(p_hg, p_hg, p_hg, p_hg, p_hg, p_hg, lb_logits.astype(F32), *consts)


def _s5_kernel(uf_ref, ub_ref, bb_ref, cc_ref, a_ref, yf_ref, yb_ref, x_ref, h_ref, *, tt, bsz):
    @pl.when(pl.program_id(0) == 0)
    def _():
        h_ref[...] = jnp.zeros_like(h_ref)

    x_ref[0] = _dot(_bf(uf_ref[...]), bb_ref[0])
    x_ref[1] = _dot(_bf(ub_ref[...]), bb_ref[1])

    def step(i, carry):
        hfr, hfi, hbr, hbi = carry
        rf = pl.multiple_of(i * bsz, bsz)
        rb = pl.multiple_of((tt - 1 - i) * bsz, bsz)
        ar, ai = a_ref[0, 0], a_ref[0, 1]
        xr = x_ref[0, pl.ds(rf, bsz), 0:S5_LANES]
        xi = x_ref[0, pl.ds(rf, bsz), S5_LANES:2 * S5_LANES]
        nfr = ar * hfr - ai * hfi + xr
        nfi = ar * hfi + ai * hfr + xi
        x_ref[0, pl.ds(rf, bsz), 0:S5_LANES] = nfr
        x_ref[0, pl.ds(rf, bsz), S5_LANES:2 * S5_LANES] = nfi
        ar, ai = a_ref[1, 0], a_ref[1, 1]
        xr = x_ref[1, pl.ds(rb, bsz), 0:S5_LANES]
        xi = x_ref[1, pl.ds(rb, bsz), S5_LANES:2 * S5_LANES]
        nbr = ar * hbr - ai * hbi + xr
        nbi = ar * hbi + ai * hbr + xi
        x_ref[1, pl.ds(rb, bsz), 0:S5_LANES] = nbr
        x_ref[1, pl.ds(rb, bsz), S5_LANES:2 * S5_LANES] = nbi
        return nfr, nfi, nbr, nbi

    hs = lax.fori_loop(0, tt, step, (h_ref[0, 0], h_ref[0, 1], h_ref[1, 0], h_ref[1, 1]))
    h_ref[0, 0], h_ref[0, 1], h_ref[1, 0], h_ref[1, 1] = hs
    yf_ref[...] = _dot(_bf(x_ref[0]), cc_ref[0])
    yb_ref[...] = _dot(_bf(x_ref[1]), cc_ref[1])


def _s5_matrices(lam_re, lam_im, log_step, b_re, b_im, c_re, c_im):
    lam_re, lam_im = lam_re.astype(F32), lam_im.astype(F32)
    step = jnp.exp(log_step.astype(F32))[..., None]
    mag = jnp.exp(lam_re * step)
    a_re, a_im = mag * jnp.cos(lam_im * step), mag * jnp.sin(lam_im * step)
    den = lam_re * lam_re + lam_im * lam_im
    n_re = a_re - 1.0
    coef_re = (n_re * lam_re + a_im * lam_im) / den
    coef_im = (a_im * lam_re - n_re * lam_im) / den
    b_re, b_im = b_re.astype(F32), b_im.astype(F32)
    bb_re = coef_re[..., None] * b_re - coef_im[..., None] * b_im
    bb_im = coef_re[..., None] * b_im + coef_im[..., None] * b_re
    eye_g = jnp.eye(S5_GROUPS, dtype=F32)
    bb = jnp.stack([bb_re, bb_im], axis=1)
    bb = jnp.einsum('drgph,gk->dghrkp', bb, eye_g).reshape(2, BRANCH_W, 2 * S5_LANES)
    cc = jnp.stack([c_re.astype(F32), -c_im.astype(F32)], axis=0)
    cc = jnp.einsum('rgkp,gj->rjpgk', cc, eye_g).reshape(2 * S5_LANES, BRANCH_W)
    return bb.astype(BF16), cc.astype(BF16), a_re.reshape(2, S5_LANES), a_im.reshape(2, S5_LANES)


def _s5(u_tb, mats, *, bsz, tc, tt):
    bb, cc, a_re, a_im = mats
    n_rows = u_tb.shape[0]
    t_all = n_rows // bsz
    nt, nct = t_all // tt, tc // tt
    a = jnp.broadcast_to(jnp.stack([a_re, a_im], axis=1)[:, :, None, :], (2, 2, bsz, S5_LANES))
    cc2 = jnp.broadcast_to(cc[None], (2,) + cc.shape)
    blk = (tt * bsz, BRANCH_W)
    fmap = lambda s: (s, 0)
    bmap = lambda s: (_bwd_order(s, nct, nt), 0)
    return pl.pallas_call(
        functools.partial(_s5_kernel, tt=tt, bsz=bsz),
        grid=(nt,),
        in_specs=[pl.BlockSpec(blk, fmap), pl.BlockSpec(blk, bmap),
                  pl.BlockSpec(bb.shape, lambda s: (0, 0, 0)), pl.BlockSpec(cc2.shape, lambda s: (0, 0, 0)),
                  pl.BlockSpec(a.shape, lambda s: (0, 0, 0, 0))],
        out_specs=[pl.BlockSpec(blk, fmap), pl.BlockSpec(blk, bmap)],
        out_shape=[jax.ShapeDtypeStruct((n_rows, BRANCH_W), F32)] * 2,
        scratch_shapes=[pltpu.VMEM((2, tt * bsz, 2 * S5_LANES), F32), pltpu.VMEM((2, 2, bsz, S5_LANES), F32)],
        compiler_params=pltpu.CompilerParams(dimension_semantics=("arbitrary",),
                                             vmem_limit_bytes=V7X_VMEM_LIMIT),
        name="s5_scan",
    )(u_tb, u_tb, bb, cc2, a)


def _rope(x, cos, sin_signed):
    w = x.shape[-1]
    lane = lax.broadcasted_iota(jnp.int32, x.shape, 1)
    up = pltpu.roll(x, w - 16, 1)
    dn = pltpu.roll(x, 16, 1)
    return x * cos + jnp.where((lane & 31) < 16, up, dn) * sin_signed


def _attn_kernel(q_ref, kv_ref, cq_ref, sq_ref, ck_ref, sk_ref, qg_ref, kg_ref, bdq_ref, bdk_ref,
                 o_ref, kp_ref, vp_ref, *, tm, tc, nct, head_rows):
    j = pl.program_id(1)
    t_all = kv_ref.shape[0]
    half = BRANCH_W // 2

    @pl.when(j == 0)
    def _():
        def pairs(x):
            lane = lax.broadcasted_iota(jnp.int32, x.shape, 1)
            sw = pltpu.roll(x, HEAD_DIM, 1)
            return _bf(jnp.where(lane < HEAD_DIM, x, sw)), _bf(jnp.where(lane < HEAD_DIM, sw, x))

        for lo, hi, rope in ((0, tc, False), (tc, t_all, True)):
            k = _head_rms(kv_ref[lo:hi, 0:half], bdk_ref[...], kg_ref[...])
            if rope:
                k = _rope(k, ck_ref[...], sk_ref[...])
            kp_ref[0, lo:hi, :], kp_ref[1, lo:hi, :] = pairs(k)
            v = kv_ref[lo:hi, half:2 * half]
            lane = lax.broadcasted_iota(jnp.int32, v.shape, 1)
            sw = pltpu.roll(v, HEAD_DIM, 1)
            for g in range(2):
                v_lo, v_hi = (v, sw) if g == 0 else (sw, v)
                vp_ref[g, 0, lo:hi, :] = _bf(jnp.where(lane < HEAD_DIM, v_lo, 1.0))
                vp_ref[g, 1, lo:hi, :] = _bf(jnp.where(lane < HEAD_DIM, 1.0, v_hi))

    def attend(q, nk):
        lane = lax.broadcasted_iota(jnp.int32, (q.shape[0], half), 1)
        outs = []
        for g in range(2):
            qp = q[:, g * half:(g + 1) * half]
            acc = None
            for hh in range(2):
                sel = (lane < HEAD_DIM) if hh == 0 else (lane >= HEAD_DIM)
                s = _dot_nt(_bf(jnp.where(sel, qp, 0.0)), kp_ref[g, 0:nk, :])
                p = jnp.exp(_bf(s - jnp.max(s, axis=-1, keepdims=True)))
                ol = _dot(p, vp_ref[g, hh, 0:nk, :])
                res = ol / pltpu.roll(ol, HEAD_DIM, 1)
                acc = res if acc is None else jnp.where(sel, res, acc)
            outs.append(acc)
        return jnp.concatenate(outs, axis=1)

    qn = _head_rms(q_ref[...], bdq_ref[...], qg_ref[...]) * (HEAD_DIM ** -0.5)
    if head_rows:
        @pl.when(j == 0)
        def _():
            o_ref[0:head_rows, :] = attend(qn[0:head_rows], tc)
            o_ref[head_rows:tm, :] = attend(
                _rope(qn[head_rows:tm], cq_ref[head_rows:tm, :], sq_ref[head_rows:tm, :]), t_all)

        @pl.when(j > 0)
        def _():
            o_ref[...] = attend(_rope(qn, cq_ref[...], sq_ref[...]), t_all)
    else:
        @pl.when(j < nct)
        def _():
            o_ref[...] = attend(qn, tc)

        @pl.when(j >= nct)
        def _():
            o_ref[...] = attend(_rope(qn, cq_ref[...], sq_ref[...]), t_all)


def _rope_tables(n_tokens):
    rows = n_tokens // GRID_W
    r, col = np.meshgrid(np.arange(rows), np.arange(GRID_W), indexing='ij')
    axis_dim = HEAD_DIM // 2
    inv = jnp.asarray(ROPE_THETA, F32) ** (-jnp.arange(0, axis_dim, 2, dtype=F32) / axis_dim)
    ang_r = jnp.asarray(r.reshape(-1, 1), F32) * inv
    ang_c = jnp.asarray(col.reshape(-1, 1), F32) * inv
    ang = jnp.concatenate([ang_r, ang_r, ang_c, ang_c], axis=-1)
    sign = np.where((np.arange(HEAD_DIM) % 32) < 16, -1.0, 1.0).astype(np.float32)
    cos, sin = jnp.cos(ang), jnp.sin(ang) * sign
    return jnp.tile(cos, (1, N_HEADS)), jnp.tile(sin, (1, N_HEADS))


def _attention(p_q, p_kv, qn_g, kn_g, cos4, sin4, *, tm, tc):
    bsz, t_all, _ = p_q.shape
    assert t_all % tm == 0 and (tc % tm == 0 or (tc < tm and tc % 8 == 0))
    nt, nct = t_all // tm, tc // tm
    head_rows = tc if tc % tm else 0
    half = BRANCH_W // 2
    tok = lambda b, j: (b, j, 0)
    c2 = lambda b, j: (0, 0)
    lat = lambda b, j: (j, 0)
    pad = jnp.zeros((tc, BRANCH_W), F32)
    cosq, sinq = jnp.concatenate([pad, cos4], axis=0), jnp.concatenate([pad, sin4], axis=0)
    qg = jnp.tile(qn_g.astype(F32), N_HEADS).reshape(1, BRANCH_W)
    kg = jnp.tile(kn_g.astype(F32), 2).reshape(1, half)
    bdq = jnp.asarray(_np_block_diag_ones(BRANCH_W, HEAD_DIM), BF16)
    bdk = jnp.asarray(_np_block_diag_ones(half, HEAD_DIM), BF16)
    cosk, sink = cos4[:, 0:half], sin4[:, 0:half]
    return pl.pallas_call(
        functools.partial(_attn_kernel, tm=tm, tc=tc, nct=nct, head_rows=head_rows),
        grid=(bsz, nt),
        in_specs=[pl.BlockSpec((None, tm, BRANCH_W), tok),
                  pl.BlockSpec((None, t_all, BRANCH_W), lambda b, j: (b, 0, 0)),
                  pl.BlockSpec((tm, BRANCH_W), lat), pl.BlockSpec((tm, BRANCH_W), lat),
                  pl.BlockSpec(cosk.shape, c2), pl.BlockSpec(sink.shape, c2),
                  pl.BlockSpec((1, BRANCH_W), c2), pl.BlockSpec((1, half), c2),
                  pl.BlockSpec(bdq.shape, c2), pl.BlockSpec(bdk.shape, c2)],
        out_specs=pl.BlockSpec((None, tm, BRANCH_W), tok),
        out_shape=jax.ShapeDtypeStruct((bsz, t_all, BRANCH_W), F32),
        scratch_shapes=[pltpu.VMEM((2, t_all, half), BF16), pltpu.VMEM((2, 2, t_all, half), BF16)],
        compiler_params=pltpu.CompilerParams(dimension_semantics=("parallel", "arbitrary"),
                                             vmem_limit_bytes=V7X_VMEM_LIMIT),
        name="attention",
    )(p_q, p_kv, cosq, sinq, cosk, sink, qg, kg, bdq, bdk)


def _merge_kernel(x_ref, m_ref, g_ref, dnf_ref, dnb_ref, z_ref, sf_ref, sb_ref, u_ref, hf_ref, hb_ref, hgate_ref,
                  at_ref, dng_ref, hgg_ref, dsk_ref, glu_ref, bd_ref, wg_ref, wb_ref, wo_ref, o_ref):
    x = x_ref[...]
    bd = bd_ref[...]
    h = _bf(_rms(x) * g_ref[...] * (1.0 + m_ref[4:5, :]) + m_ref[3:4, :])
    y_dn = _head_rms(dnf_ref[...] + dnb_ref[...], bd, dng_ref[...]) * _silu(z_ref[...])
    ys = sf_ref[...] + sb_ref[...] + dsk_ref[...] * u_ref[...]
    ab = _dot(_bf(jax.nn.gelu(ys, approximate=True)), glu_ref[...])
    y_s5 = ab[:, 0:BRANCH_W] * _sigmoid(ab[:, BRANCH_W:2 * BRANCH_W])
    y_hg = _head_rms(hf_ref[...] + hb_ref[...], bd, hgg_ref[...]) * _sigmoid(hgate_ref[...])
    ys_all = (y_dn, y_s5, y_hg, at_ref[...])
    acc = jnp.zeros(x.shape, F32)
    for i in range(4):
        acc = acc + _sigmoid(_dot(h, wg_ref[i])) * _dot(_bf(ys_all[i]), wb_ref[i])
    o_ref[...] = x + m_ref[5:6, :] * _dot(_bf(acc), wo_ref[...])


def _merge(xj, mods, g, br, params, *, tm, nct, j0):
    bsz, t_all, _ = xj.shape
    n_tiles = t_all // tm - j0
    tok = lambda b, j: (b, j + j0, 0)
    col = lambda c: (lambda b, j: (b, j + j0, c))
    mod_map = lambda b, j: (jnp.where(j + j0 < nct, bsz, b), 0, 0)
    c2 = lambda b, j: (0, 0)
    c3 = lambda b, j: (0, 0, 0)
    tb = (None, tm, BRANCH_W)
    tmaj, tmaj_map = (tm, BRANCH_W), (lambda b, j: (j + j0, b))
    dn_g, hg_g, d_skip, w_glu, wg, wb, wo = params
    bd = jnp.asarray(_np_block_diag_ones(BRANCH_W, HEAD_DIM), BF16)
    return pl.pallas_call(
        _merge_kernel,
        grid=(bsz, n_tiles),
        in_specs=[pl.BlockSpec((None, tm, D_MODEL), tok),
                  pl.BlockSpec((None, N_MOD, D_MODEL), mod_map),
                  pl.BlockSpec((1, D_MODEL), c2),
                  pl.BlockSpec(tb, tok), pl.BlockSpec(tb, tok), pl.BlockSpec(tb, col(3)),
                  pl.BlockSpec(tmaj, tmaj_map), pl.BlockSpec(tmaj, tmaj_map), pl.BlockSpec(tmaj, tmaj_map),
                  pl.BlockSpec(tb, tok), pl.BlockSpec(tb, tok), pl.BlockSpec(tb, col(4)),
                  pl.BlockSpec(tb, tok),
                  pl.BlockSpec((1, BRANCH_W), c2), pl.BlockSpec((1, BRANCH_W), c2), pl.BlockSpec((1, BRANCH_W), c2),
                  pl.BlockSpec(w_glu.shape, c2), pl.BlockSpec(bd.shape, c2),
                  pl.BlockSpec(wg.shape, c3), pl.BlockSpec(wb.shape, c3), pl.BlockSpec(wo.shape, c2)],
        out_specs=pl.BlockSpec((None, tm, D_MODEL), lambda b, j: (b, j, 0)),
        out_shape=jax.ShapeDtypeStruct((bsz, n_tiles * tm, D_MODEL), F32),
        compiler_params=pltpu.CompilerParams(dimension_semantics=("parallel", "parallel"),
                                             vmem_limit_bytes=V7X_VMEM_LIMIT),
        name="branch_merge",
    )(xj, mods, g.reshape(1, D_MODEL), br['dn_f'], br['dn_b'], br['p_dn'], br['s5_f'], br['s5_b'], br['p_s5'],
      br['hg_f'], br['hg_b'], br['p_hg'], br['at'], dn_g, hg_g, d_skip, w_glu, bd, wg, wb, wo)


def kernel(x, c, ctx, c_ctx, ada_w, ada_b, norm_g, ffn_w1, ffn_w3, ffn_w2, w_in, dn_conv, dn_a_log, dn_dt_bias,
           dn_norm_g, s5_lam_re, s5_lam_im, s5_log_step, s5_b_re, s5_b_im, s5_c_re, s5_c_im, s5_d, s5_glu,
           hg_lb_logits, hg_norm_g, at_qn_g, at_kn_g, w_branch, w_out, final_g):
    bsz, tl, _ = x.shape
    tc = ctx.shape[1]
    t_all = tc + tl
    depth = ada_w.shape[0]
    tm = 256 if (tc % 256 == 0 and tl % 256 == 0) else 128
    assert tc % tm == 0 and tl % tm == 0 and tm % CHUNK == 0 and bsz % 8 == 0
    nct = tc // tm
    tm_joint = 768 if (t_all % 768 == 0 and tc < 768) else tm
    tm_latent = next((t for t in (1024, 512) if tl % t == 0), tm)
    d_ff = ffn_w1.shape[-1]
    n_ff = d_ff // FF_CHUNK

    n_rows = -(-(bsz + 1) // 8) * 8
    cond = jnp.zeros((n_rows, D_MODEL), F32).at[0:bsz].set(c.astype(F32)).at[bsz].set(c_ctx.astype(F32))
    mods_all = _ada_mods(cond, ada_w.astype(F32), ada_b.astype(F32))
    cos4, sin4 = _rope_tables(tl)
    xj = x.astype(F32)

    def ffn_weights(l, i):
        w1 = ffn_w1[l, i].astype(BF16).reshape(D_MODEL, n_ff, FF_CHUNK).transpose(1, 0, 2)
        w3 = ffn_w3[l, i].astype(BF16).reshape(D_MODEL, n_ff, FF_CHUNK).transpose(1, 0, 2)
        w2 = ffn_w2[l, i].astype(BF16)
        return w1, w3, w2

    for l in range(depth):
        last = l == depth - 1
        mods = mods_all[l]
        if l == 0:
            xj = _ffn_half_step(xj, mods, norm_g[l, 0], *ffn_weights(l, 0), mod_base=0, tm=tm, tc=tc,
                                ctx=ctx.astype(F32))
        else:
            xj = _ffn_half_step(xj, mods, norm_g[l, 0], *ffn_weights(l, 0), mod_base=0, tm=tm_joint, tc=tc)

        wl = w_in[l]
        w_dn = jnp.concatenate([wl[:, 0:1040], jnp.zeros((D_MODEL, 112), wl.dtype)], axis=1)
        ws = [w.astype(BF16) for w in (w_dn, wl[:, 1040:1296], wl[:, 1296:2576], wl[:, 2576:2832], wl[:, 2832:3088])]
        p_dn, p_s5, p_hg, p_aq, p_kv = _input_projection(xj, mods, norm_g[l, 1], ws, tm=tm, nct=nct)

        dn_f, dn_b = _deltanet(p_dn, dn_conv[l], dn_a_log[l], dn_dt_bias[l], tm=tm, tc=tc)
        hg_f, hg_b = _hgrn2(p_hg, hg_lb_logits, l, tm=tm, tc=tc)
        mats = _s5_matrices(s5_lam_re[l], s5_lam_im[l], s5_log_step[l], s5_b_re[l], s5_b_im[l], s5_c_re[l], s5_c_im[l])
        s5_f, s5_b = _s5(p_s5.reshape(t_all * bsz, BRANCH_W), mats, bsz=bsz, tc=tc, tt=CHUNK)
        s5_f = s5_f.reshape(t_all, bsz * BRANCH_W)
        s5_b = s5_b.reshape(t_all, bsz * BRANCH_W)
        y_at = _attention(p_aq, p_kv, at_qn_g[l], at_kn_g[l], cos4, sin4, tm=tm_joint, tc=tc)

        br = dict(dn_f=dn_f, dn_b=dn_b, p_dn=p_dn, s5_f=s5_f, s5_b=s5_b, p_s5=p_s5, hg_f=hg_f, hg_b=hg_b,
                  p_hg=p_hg, at=y_at)
        params = (jnp.tile(dn_norm_g[l].astype(F32), N_HEADS).reshape(1, BRANCH_W),
                  jnp.tile(hg_norm_g[l].astype(F32), N_HEADS).reshape(1, BRANCH_W),
                  s5_d[l].astype(F32).reshape(1, BRANCH_W),
                  s5_glu[l].astype(BF16),
                  wl[:, 3088:].astype(BF16).reshape(D_MODEL, 4, D_MODEL).transpose(1, 0, 2),
                  w_branch[l].astype(BF16),
                  w_out[l].astype(BF16))
        xj = _merge(xj, mods, norm_g[l, 1], br, params, tm=tm, nct=nct, j0=nct if last else 0)
        if last:
            xj = _ffn_half_step(xj, mods, norm_g[l, 2], *ffn_weights(l, 1), mod_base=6, tm=tm_latent, tc=0,
                                final_g=final_g.astype(F32))
        else:
            xj = _ffn_half_step(xj, mods, norm_g[l, 2], *ffn_weights(l, 1), mod_base=6, tm=tm_joint, tc=tc)
    return xj.astype(x.dtype)
```

```python
import functools
import math

import numpy as np
import jax
import jax.numpy as jnp
from jax import lax
from jax.experimental import pallas as pl
from jax.experimental.pallas import tpu as pltpu

F32 = jnp.float32
BF16 = jnp.bfloat16

D_MODEL = 1024
N_MOD = 9
BRANCH_W = 256
HEAD_DIM = 64
N_HEADS = 4
CHUNK = 64
HG_BLK = 16
EPS = 1e-6
LOG_FLOOR = 1e-30
GRID_W = 64
ROPE_THETA = 10000.0
FF_CHUNK = 256
S5_GROUPS = 16
S5_GROUP = 16
S5_STATE = 64
S5_LANES = S5_GROUPS * S5_STATE
NEG_BIG = -1e30
V7X_VMEM_LIMIT = 56 * 1024 * 1024


def _bf(x):
    return x.astype(BF16)


def _dot(a, b):
    return jnp.dot(a, b, preferred_element_type=F32)


def _dot_nt(a, b):
    return lax.dot_general(a, b, (((1,), (1,)), ((), ())), preferred_element_type=F32)


def _dot_tn(a, b):
    return lax.dot_general(a, b, (((0,), (0,)), ((), ())), preferred_element_type=F32)


def _split2(x):
    hi = _bf(x)
    lo = _bf(x - hi.astype(F32))
    return hi, lo


def _split3(x):
    x1 = _bf(x)
    r1 = x - x1.astype(F32)
    x2 = _bf(r1)
    x3 = _bf(r1 - x2.astype(F32))
    return x1, x2, x3


def _dot_x2(x, m):
    hi, lo = _split2(x)
    return _dot(hi, m) + _dot(lo, m)


def _dot_x3(x, m):
    x1, x2, x3 = _split3(x)
    return _dot(x1, m) + _dot(x2, m) + _dot(x3, m)


def _dot3_left(m, x):
    x1, x2, x3 = _split3(x)
    return _dot(m, x1) + _dot(m, x2) + _dot(m, x3)


def _dot_hp(a, b):
    a_hi, a_lo = _split2(a)
    b_hi, b_lo = _split2(b)
    return _dot(a_hi, b_hi) + _dot(a_hi, b_lo) + _dot(a_lo, b_hi)


def _sigmoid(x):
    return jax.nn.sigmoid(x)


def _silu(x):
    return x * jax.nn.sigmoid(x)


def _rms(x):
    return x * lax.rsqrt(jnp.mean(x * x, axis=-1, keepdims=True) + EPS)


def _head_rms(x, bd_ones, gain):
    ss = _dot_x2(x * x, bd_ones) * (1.0 / HEAD_DIM)
    return x * lax.rsqrt(ss + EPS) * gain


def _stack4(x):
    return jnp.concatenate([x, x, x, x], axis=0)


def _unstack4(x):
    n = x.shape[0] // 4
    return x[0:n] + x[n:2 * n] + x[2 * n:3 * n] + x[3 * n:4 * n]


def _np_block_diag_ones(n, blk):
    i = np.arange(n)
    return (i[:, None] // blk == i[None, :] // blk).astype(np.float32)


def _np_tri(n, blk, lower, strict):
    i = np.arange(n)
    same = i[:, None] // blk == i[None, :] // blk
    a, b = i[:, None] % blk, i[None, :] % blk
    if lower:
        t = (a > b) if strict else (a >= b)
    else:
        t = (a < b) if strict else (a <= b)
    return (same & t).astype(np.float32)


def _np_doubling_masks(n, blk):
    i = np.arange(n)
    r, c = i[:, None], i[None, :]
    out = [(r // 2 == c // 2)]
    s = 2
    while s < blk:
        out.append((r // (2 * s) == c // (2 * s)) & (r // s != c // s))
        s *= 2
    return np.stack(out).astype(np.float32)


def _bwd_order(s, n_ctx, n_all):
    return jnp.where(s < n_ctx, n_ctx - 1 - s, n_all + n_ctx - 1 - s)


def _ada_kernel(c_ref, w_ref, b_ref, o_ref):
    s = _silu(c_ref[...])
    w = w_ref[...]
    s_hi, s_lo = _split2(s)
    w_hi, w_lo = _split2(w)
    o_ref[...] = _dot(s_hi, w_hi) + _dot(s_hi, w_lo) + _dot(s_lo, w_hi) + b_ref[...]


def _ada_mods(cond, ada_w, ada_b):
    n_layers = ada_w.shape[0]
    r = cond.shape[0]
    n_out = ada_w.shape[2]
    tn = 1152
    out = pl.pallas_call(
        _ada_kernel,
        grid=(n_layers, n_out // tn),
        in_specs=[
            pl.BlockSpec((r, D_MODEL), lambda l, n: (0, 0)),
            pl.BlockSpec((None, D_MODEL, tn), lambda l, n: (l, 0, n)),
            pl.BlockSpec((None, 1, tn), lambda l, n: (l, 0, n)),
        ],
        out_specs=pl.BlockSpec((None, r, tn), lambda l, n: (l, 0, n)),
        out_shape=jax.ShapeDtypeStruct((n_layers, r, n_out), F32),
        compiler_params=pltpu.CompilerParams(dimension_semantics=("parallel", "parallel"),
                                             vmem_limit_bytes=40 * 1024 * 1024),
        name="ada_mods",
    )(cond, ada_w, ada_b.reshape(n_layers, 1, n_out))
    return out.reshape(n_layers, r, N_MOD, D_MODEL)


def _ffn_kernel(*refs, mod_base, final, n_ctx_tiles, head_rows):
    refs = list(refs)
    o_ref = refs.pop()
    x_ref = refs.pop(0)
    c_ref = refs.pop(0) if n_ctx_tiles else None
    m_ref = refs.pop(0)
    mc_ref = refs.pop(0) if head_rows else None
    g_ref, w1_ref, w3_ref, w2_ref = refs[0:4]
    fg_ref = refs[4] if final else None
    x = x_ref[...]
    if c_ref is not None:
        x = jnp.where(pl.program_id(1) < n_ctx_tiles, c_ref[...], x)

    def mod(i, head):
        row = m_ref[mod_base + i:mod_base + i + 1, :]
        if head:
            row = jnp.where(pl.program_id(1) == 0, mc_ref[mod_base + i:mod_base + i + 1, :], row)
        return row

    def rows(fn):
        if not head_rows:
            return fn(slice(None), False)
        return jnp.concatenate([fn(slice(0, head_rows), True), fn(slice(head_rows, None), False)], axis=0)

    y = _rms(x) * g_ref[...]
    h = _bf(rows(lambda r, hd: y[r] * (1.0 + mod(1, hd)) + mod(0, hd)))
    us = []
    for c in range(w1_ref.shape[0]):
        a = _dot(h, w1_ref[c])
        us.append(_bf(_silu(a) * _dot(h, w3_ref[c])))
    acc = _dot(jnp.concatenate(us, axis=1), w2_ref[...])
    out = x + 0.5 * rows(lambda r, hd: mod(2, hd) * acc[r])
    if final:
        out = _rms(out) * fg_ref[...]
    o_ref[...] = out


def _ffn_half_step(xj, mods, g, w1c, w3c, w2c, *, mod_base, tm, tc, final_g=None, ctx=None):
    bsz, t_all, _ = xj.shape
    final = final_g is not None
    if ctx is not None:
        t_all += ctx.shape[1]
    assert t_all % tm == 0 and (tc % tm == 0 or (tc < tm and tc % 8 == 0 and ctx is None))
    nct = tc // tm
    head_rows = tc if tc % tm else 0

    def tok_map(b, j):
        return (b, j, 0)

    def mod_map(b, j):
        return (jnp.where(j < nct, bsz, b), 0, 0)

    const3 = lambda b, j: (0, 0, 0)
    const2 = lambda b, j: (0, 0)
    once = dict(pipeline_mode=pl.Buffered(1))
    if ctx is None:
        tok_specs, tok_args = [pl.BlockSpec((None, tm, D_MODEL), tok_map)], [xj]
    else:
        tok_specs = [pl.BlockSpec((None, tm, D_MODEL), lambda b, j: (b, jnp.maximum(j - nct, 0), 0)),
                     pl.BlockSpec((None, tm, D_MODEL), lambda b, j: (b, jnp.minimum(j, nct - 1), 0))]
        tok_args = [xj, ctx]
    in_specs = tok_specs + [pl.BlockSpec((None, N_MOD, D_MODEL), mod_map)]
    args = tok_args + [mods]
    if head_rows:
        in_specs.append(pl.BlockSpec((None, N_MOD, D_MODEL), lambda b, j: (bsz, 0, 0)))
        args.append(mods)
    in_specs += [
        pl.BlockSpec((1, D_MODEL), const2),
        pl.BlockSpec(w1c.shape, const3, **once),
        pl.BlockSpec(w3c.shape, const3, **once),
        pl.BlockSpec(w2c.shape, const2, **once),
    ]
    args += [g.reshape(1, D_MODEL), w1c, w3c, w2c]
    if final:
        in_specs.append(pl.BlockSpec((1, D_MODEL), const2))
        args.append(final_g.reshape(1, D_MODEL))
    return pl.pallas_call(
        functools.partial(_ffn_kernel, mod_base=mod_base, final=final, n_ctx_tiles=nct if ctx is not None else 0,
                          head_rows=head_rows),
        grid=(bsz, t_all // tm),
        in_specs=in_specs,
        out_specs=pl.BlockSpec((None, tm, D_MODEL), tok_map),
        out_shape=jax.ShapeDtypeStruct((bsz, t_all, D_MODEL), F32),
        compiler_params=pltpu.CompilerParams(dimension_semantics=("parallel", "parallel"),
                                             vmem_limit_bytes=V7X_VMEM_LIMIT),
        name="ffn_half_step",
    )(*args)


def _inproj_kernel(x_ref, m_ref, mc_ref, g_ref, wdn_ref, ws5_ref, whg_ref, waq_ref, wkv_ref,
                   dn_ref, s5_ref, hg_ref, aq_ref, kv_ref, *, head_rows):
    y = _rms(x_ref[...]) * g_ref[...]
    if head_rows:
        first = pl.program_id(1) == 0
        scale_h = jnp.where(first, mc_ref[4:5, :], m_ref[4:5, :])
        shift_h = jnp.where(first, mc_ref[3:4, :], m_ref[3:4, :])
        h = _bf(jnp.concatenate([y[0:head_rows] * (1.0 + scale_h) + shift_h,
                                 y[head_rows:] * (1.0 + m_ref[4:5, :]) + m_ref[3:4, :]], axis=0))
    else:
        h = _bf(y * (1.0 + m_ref[4:5, :]) + m_ref[3:4, :])
    dn_ref[...] = _dot(h, wdn_ref[...])
    s5_ref[...] = _dot(h, ws5_ref[...])
    hg_ref[...] = _dot(h, whg_ref[...])
    aq_ref[...] = _dot(h, waq_ref[...])
    kv_ref[...] = _dot(h, wkv_ref[...])


def _input_projection(xj, mods, g, ws, *, tm, tc):
    bsz, t_all, _ = xj.shape
    assert t_all % tm == 0 and (tc % tm == 0 or (tc < tm and tc % 8 == 0))
    nct = tc // tm
    tok_map = lambda b, j: (b, j, 0)
    mod_map = lambda b, j: (jnp.where(j < nct, bsz, b), 0, 0)
    const2 = lambda b, j: (0, 0)
    widths = [w.shape[1] for w in ws]
    return pl.pallas_call(
        functools.partial(_inproj_kernel, head_rows=tc if tc % tm else 0),
        grid=(bsz, t_all // tm),
        in_specs=[pl.BlockSpec((None, tm, D_MODEL), tok_map),
                  pl.BlockSpec((None, N_MOD, D_MODEL), mod_map),
                  pl.BlockSpec((None, N_MOD, D_MODEL), lambda b, j: (bsz, 0, 0)),
                  pl.BlockSpec((1, D_MODEL), const2)]
                 + [pl.BlockSpec(w.shape, const2, pipeline_mode=pl.Buffered(1)) for w in ws],
        out_specs=[pl.BlockSpec((tm, n), lambda b, j: (j, b)) if i == 1 else pl.BlockSpec((None, tm, n), tok_map)
                   for i, n in enumerate(widths)],
        out_shape=[jax.ShapeDtypeStruct((t_all, bsz * n) if i == 1 else (bsz, t_all, n), F32)
                   for i, n in enumerate(widths)],
        compiler_params=pltpu.CompilerParams(dimension_semantics=("parallel", "parallel"),
                                             vmem_limit_bytes=V7X_VMEM_LIMIT),
        name="input_projection",
    )(xj, mods, mods, g.reshape(1, D_MODEL), *ws)


def _dn_prep_kernel(p_ref, prev_ref, next_ref, cw_ref, al_ref, dtb_ref, e_ref, tril_ref, triu_ref, bd_ref,
                    q_ref, k_ref, v_ref, gb_ref, *, tm, nct, nt):
    j = pl.program_id(1)
    first = jnp.logical_or(j == 0, j == nct)
    last = jnp.logical_or(j == nct - 1, j == nt - 1)
    x = p_ref[:, 0:3 * BRANCH_W]
    pv = prev_ref[...] * jnp.where(first, 0.0, 1.0)
    nx = next_ref[...] * jnp.where(last, 0.0, 1.0)
    ext = jnp.concatenate([pv, x, nx], axis=0)
    n = tm + 16
    acc = jnp.zeros((tm, 3 * BRANCH_W), F32)
    for t in range(5):
        sh = (2 - t) % n
        r = ext if sh == 0 else pltpu.roll(ext, sh, 0)
        acc = acc + r[8:8 + tm] * cw_ref[t:t + 1, :]
    h = _silu(acc)
    bd = bd_ref[...]

    def l2n(t):
        return t * lax.rsqrt(_dot_x2(t * t, bd) + EPS)

    q_ref[...] = l2n(h[:, 0:BRANCH_W]) * (HEAD_DIM ** -0.5)
    k_ref[...] = l2n(h[:, BRANCH_W:2 * BRANCH_W])
    v_ref[...] = h[:, 2 * BRANCH_W:3 * BRANCH_W]

    ab = p_ref[:, 4 * BRANCH_W:4 * BRANCH_W + 128]
    z = ab + dtb_ref[...]
    softplus = jnp.maximum(z, 0.0) + jnp.log1p(jnp.exp(-jnp.abs(z)))
    g = -jnp.exp(al_ref[...]) * softplus
    lane = lax.broadcasted_iota(jnp.int32, ab.shape, 1)
    g = jnp.where(lane < 8, g, 0.0)
    cs_f = _dot3_left(tril_ref[...], g)
    cs_b = _dot3_left(triu_ref[...], g)
    comp = jnp.where(lane < 4, cs_f, jnp.where(lane < 8, cs_b, jnp.where(lane < 16, _sigmoid(ab), 0.0)))
    gb_ref[...] = _dot_x3(comp, e_ref[...])


def _dn_chunks_local(items, bd, it, lvl_ref):
    def stk(x):
        return _stack4(_bf(x)) * bd

    ks = [stk(c[1]) for c in items]
    gq = [_dot_nt(_bf(jnp.concatenate([c[1] * c[4], c[0]], axis=0)), x) for c, x in zip(items, ks)]
    decays, a_s = [], []
    for (q, k, v, gc, be, neg_incl, tri_strict, fwd), g in zip(items, gq):
        col_gc = jnp.sum(gc * it, axis=0, keepdims=True)
        decay = jnp.exp(gc - col_gc + neg_incl)
        decays.append(decay)
        a_s.append(g[0:CHUNK] * decay * tri_strict)
    ts = [it - a * lvl_ref[0] for a in a_s]
    for lv in range(1, lvl_ref.shape[0]):
        xs = [_dot(_bf(a * lvl_ref[lv]), stk(t)) for a, t in zip(a_s, ts)]
        ts = [t - _dot(_bf(t), stk(x)) for t, x in zip(ts, xs)]
    egcs = [jnp.exp(c[3]) for c in items]
    uws = [_dot(_bf(t), jnp.concatenate([stk(c[2] * c[4]), stk(c[1] * c[4] * e)], axis=1))
           for t, c, e in zip(ts, items, egcs)]
    out = []
    for (q, k, v, gc, be, _, _, fwd), uw, g, dc, egc in zip(items, uws, gq, decays, egcs):
        total = gc[CHUNK - 1:CHUNK, :] if fwd else gc[0:1, :]
        out.append((uw[:, 0:BRANCH_W], _bf(uw[:, BRANCH_W:2 * BRANCH_W]), _bf(g[CHUNK:2 * CHUNK] * dc),
                    _bf(q * egc), _bf(k * jnp.exp(total - gc)), jnp.exp(total)))
    return out


def _dn_chunk_state(local, s_prev, bd, bd_f32):
    u, w, attn, q_dec, k_dec, g_last = local
    ws = _dot(jnp.concatenate([w, q_dec], axis=0), _bf(s_prev))
    v_new = _bf(u - ws[0:CHUNK])
    o = ws[CHUNK:2 * CHUNK] + _dot(attn, _stack4(v_new) * bd)
    s_new = s_prev * g_last + _dot_tn(k_dec, v_new) * bd_f32
    return o, s_new


def _dn_scan_kernel(qf_ref, kf_ref, vf_ref, gcf_ref, bef_ref, qb_ref, kb_ref, vb_ref, gcb_ref, beb_ref,
                    bdf_ref, li_ref, ls_ref, ui_ref, us_ref, it_ref, lvl_ref, bd_ref,
                    of_ref, ob_ref, s_ref):
    @pl.when(pl.program_id(1) == 0)
    def _():
        s_ref[...] = jnp.zeros_like(s_ref)

    bd, bd_f32, it = bd_ref[...], bdf_ref[...], it_ref[...]
    n_sub = qf_ref.shape[0] // CHUNK
    dirs = ((qf_ref, kf_ref, vf_ref, gcf_ref, bef_ref, li_ref, ls_ref, of_ref, True),
            (qb_ref, kb_ref, vb_ref, gcb_ref, beb_ref, ui_ref, us_ref, ob_ref, False))
    items = []
    for q_ref, k_ref, v_ref, gc_ref, be_ref, ti_ref, ts_ref, _, fwd in dirs:
        for g in range(n_sub):
            r = slice(g * CHUNK, (g + 1) * CHUNK)
            items.append((q_ref[r, :], k_ref[r, :], v_ref[r, :], gc_ref[r, :], be_ref[r, :],
                          ti_ref[...], ts_ref[...], fwd))
    local = _dn_chunks_local(items, bd, it, lvl_ref)
    s_f, s_b = s_ref[0], s_ref[1]
    for i in range(n_sub):
        gb = n_sub - 1 - i
        o, s_f = _dn_chunk_state(local[i], s_f, bd, bd_f32)
        of_ref[i * CHUNK:(i + 1) * CHUNK, :] = o
        o, s_b = _dn_chunk_state(local[n_sub + gb], s_b, bd, bd_f32)
        ob_ref[gb * CHUNK:(gb + 1) * CHUNK, :] = o
    s_ref[0] = s_f
    s_ref[1] = s_b


def _deltanet(p_dn, conv_w, a_log, dt_bias, *, tm, tc):
    bsz, t_all, _ = p_dn.shape
    nt, nct = t_all // tm, tc // tm
    al = jnp.zeros((1, 128), F32).at[0, 0:8].set(a_log.reshape(8).astype(F32))
    dtb = jnp.zeros((1, 128), F32).at[0, 0:8].set(dt_bias.reshape(8).astype(F32))
    e = np.zeros((128, 4 * BRANCH_W), np.float32)
    for jj in range(4):
        for h in range(N_HEADS):
            e[4 * jj + h, jj * BRANCH_W + h * HEAD_DIM: jj * BRANCH_W + (h + 1) * HEAD_DIM] = 1.0
    consts = [jnp.asarray(e, BF16),
              jnp.asarray(_np_tri(tm, CHUNK, True, False), BF16),
              jnp.asarray(_np_tri(tm, CHUNK, False, False), BF16),
              jnp.asarray(_np_block_diag_ones(BRANCH_W, HEAD_DIM), BF16)]
    tok = lambda b, j: (b, j, 0)
    c2 = lambda b, j: (0, 0)
    hb = tm // 8
    q, k, v, gb = pl.pallas_call(
        functools.partial(_dn_prep_kernel, tm=tm, nct=nct, nt=nt),
        grid=(bsz, nt),
        in_specs=[pl.BlockSpec((None, tm, p_dn.shape[2]), tok),
                  pl.BlockSpec((None, 8, 3 * BRANCH_W), lambda b, j: (b, jnp.maximum(j * hb - 1, 0), 0)),
                  pl.BlockSpec((None, 8, 3 * BRANCH_W), lambda b, j: (b, jnp.minimum((j + 1) * hb, nt * hb - 1), 0)),
                  pl.BlockSpec(conv_w.shape, c2), pl.BlockSpec((1, 128), c2), pl.BlockSpec((1, 128), c2)]
                 + [pl.BlockSpec(c.shape, c2) for c in consts],
        out_specs=[pl.BlockSpec((None, tm, BRANCH_W), tok)] * 3 + [pl.BlockSpec((None, tm, 4 * BRANCH_W), tok)],
        out_shape=[jax.ShapeDtypeStruct((bsz, t_all, BRANCH_W), F32)] * 3
                  + [jax.ShapeDtypeStruct((bsz, t_all, 4 * BRANCH_W), F32)],
        compiler_params=pltpu.CompilerParams(dimension_semantics=("parallel", "parallel"),
                                             vmem_limit_bytes=40 * 1024 * 1024),
        name="deltanet_prep",
    )(p_dn, p_dn, p_dn, conv_w.astype(F32), al, dtb, *consts)

    nc, ncc = nt, nct
    n4 = N_HEADS * CHUNK
    neg = lambda tri: (1.0 - tri) * NEG_BIG
    per_head = lambda m: np.tile(m, (1,) * (m.ndim - 1) + (N_HEADS,))
    masks = [jnp.asarray(_np_block_diag_ones(n4, CHUNK), F32),
             jnp.asarray(per_head(neg(_np_tri(CHUNK, CHUNK, True, False))), F32),
             jnp.asarray(per_head(_np_tri(CHUNK, CHUNK, True, True)), F32),
             jnp.asarray(per_head(neg(_np_tri(CHUNK, CHUNK, False, False))), F32),
             jnp.asarray(per_head(_np_tri(CHUNK, CHUNK, False, True)), F32),
             jnp.asarray(per_head(np.eye(CHUNK, dtype=np.float32)), F32),
             jnp.asarray(per_head(_np_doubling_masks(CHUNK, CHUNK)), F32),
             jnp.asarray(_np_block_diag_ones(n4, CHUNK), BF16)]
    cmap = lambda m: (lambda b, s: (0,) * m.ndim)
    blk = (None, tm, BRANCH_W)
    fmap = lambda col: (lambda b, s: (b, s, col))
    bmap = lambda col: (lambda b, s: (b, _bwd_order(s, ncc, nc), col))
    o_f, o_b = pl.pallas_call(
        _dn_scan_kernel,
        grid=(bsz, nc),
        in_specs=[pl.BlockSpec(blk, fmap(0))] * 3 + [pl.BlockSpec(blk, fmap(0)), pl.BlockSpec(blk, fmap(2))]
                 + [pl.BlockSpec(blk, bmap(0))] * 3 + [pl.BlockSpec(blk, bmap(1)), pl.BlockSpec(blk, bmap(3))]
                 + [pl.BlockSpec(m.shape, cmap(m)) for m in masks],
        out_specs=[pl.BlockSpec(blk, fmap(0)), pl.BlockSpec(blk, bmap(0))],
        out_shape=[jax.ShapeDtypeStruct((bsz, t_all, BRANCH_W), F32)] * 2,
        scratch_shapes=[pltpu.VMEM((2, n4, n4), F32)],
        compiler_params=pltpu.CompilerParams(dimension_semantics=("parallel", "arbitrary"),
                                             vmem_limit_bytes=40 * 1024 * 1024),
        name="deltanet_scan",
    )(q, k, v, gb, gb, q, k, v, gb, gb, *masks)
    return o_f, o_b


def _hg_tile_prepare(q, f, v, lb, tri_incl, tri_excl, tmp_ref):
    qs = _silu(q)
    logf = jnp.log(jnp.maximum(lb + (1.0 - lb) * _sigmoid(f), LOG_FLOOR))
    kk = (1.0 - lb) * _sigmoid(-f)
    bl = _dot3_left(tri_incl, logf)
    suf = _dot3_left(tri_excl, logf)
    tmp_ref[0], tmp_ref[1], tmp_ref[2], tmp_ref[3] = qs, bl - jnp.log(kk), v, bl
    return _bf(qs * jnp.exp(bl)), _bf(kk * jnp.exp(suf)), _bf(v), bl


def _row16(row):
    r8 = jnp.broadcast_to(row, (8, row.shape[1]))
    return jnp.concatenate([r8] * (HG_BLK // 8), axis=0)


def _hg_block_pairs(tmp_ref, r0, bd_ones, fwd):
    io = lax.broadcasted_iota(jnp.int32, (HG_BLK, BRANCH_W), 0)
    q_b = _bf(tmp_ref[0, r0:r0 + HG_BLK, :])
    b_b = tmp_ref[3, r0:r0 + HG_BLK, :]
    rows = []
    for i in range(HG_BLK):
        c_s = _row16(tmp_ref[1, r0 + i:r0 + i + 1, :])
        vis = (io >= i) if fwd else (io <= i)
        rows.append(q_b * jnp.exp(_bf(jnp.where(vis, b_b - c_s, NEG_BIG))))
    se = _dot(jnp.concatenate(rows, axis=0), bd_ones)
    out = None
    for i in range(HG_BLK):
        term = se[i * HG_BLK:(i + 1) * HG_BLK] * _row16(tmp_ref[2, r0 + i:r0 + i + 1, :])
        out = term if out is None else out + term
    return out


def _hg_scan_kernel(qf_ref, ff_ref, vf_ref, qb_ref, fb_ref, vb_ref, lg_ref, bdo_ref, bdm_ref,
                    tli_ref, tls_ref, tui_ref, tus_ref, of_ref, ob_ref, st_ref, tmp_ref, *, layer):
    @pl.when(pl.program_id(1) == 0)
    def _():
        st_ref[...] = jnp.zeros_like(st_ref)

    n_layers = lg_ref.shape[0]
    mx = lg_ref[0:1, :]
    for i in range(1, n_layers):
        mx = jnp.maximum(mx, lg_ref[i:i + 1, :])
    den = jnp.zeros_like(mx)
    num = jnp.zeros_like(mx)
    for i in range(n_layers):
        e = jnp.exp(lg_ref[i:i + 1, :] - mx)
        den = den + e
        if 1 <= i <= layer:
            num = num + e
    lb = num / den
    bdo, hmask = bdo_ref[...], bdm_ref[...]

    def stk(x):
        return jnp.concatenate([x] * N_HEADS, axis=0) * hmask

    n_blk = qf_ref.shape[0] // HG_BLK
    prep = (_hg_tile_prepare(qf_ref[...], ff_ref[...], vf_ref[...], lb, tli_ref[...], tus_ref[...], tmp_ref.at[0]),
            _hg_tile_prepare(qb_ref[...], fb_ref[...], vb_ref[...], lb, tui_ref[...], tls_ref[...], tmp_ref.at[1]))
    pairs = [[_hg_block_pairs(tmp_ref.at[d], i * HG_BLK, bdo, d == 0) for i in range(n_blk)] for d in range(2)]
    st = [st_ref[0], st_ref[1]]
    for step in range(n_blk):
        for d, o_ref in ((0, of_ref), (1, ob_ref)):
            i = step if d == 0 else n_blk - 1 - step
            r = slice(i * HG_BLK, (i + 1) * HG_BLK)
            q_t, k_t, v_b, bl = prep[d]
            o_ref[r, :] = pairs[d][i] + _dot_nt(q_t[r], _bf(st[d]))
            last = (i + 1) * HG_BLK - 1 if d == 0 else i * HG_BLK
            st[d] = st[d] * jnp.exp(bl[last:last + 1, :]) + _dot_tn(stk(v_b[r]), stk(k_t[r]))
    st_ref[0], st_ref[1] = st


def _hgrn2(p_hg, lb_logits, layer, *, tm, tc):
    bsz, t_all, _ = p_hg.shape
    nc, ncc = t_all // tm, tc // tm
    head_of_row = np.arange(N_HEADS * HG_BLK)[:, None] // HG_BLK == np.arange(BRANCH_W)[None, :] // HEAD_DIM
    consts = [jnp.asarray(_np_block_diag_ones(BRANCH_W, HEAD_DIM), BF16),
              jnp.asarray(head_of_row.astype(np.float32), BF16),
              jnp.asarray(_np_tri(tm, HG_BLK, True, False), BF16),
              jnp.asarray(_np_tri(tm, HG_BLK, True, True), BF16),
              jnp.asarray(_np_tri(tm, HG_BLK, False, False), BF16),
              jnp.asarray(_np_tri(tm, HG_BLK, False, True), BF16)]
    blk = (None, tm, BRANCH_W)
    c2 = lambda b, s: (0, 0)
    fmap = lambda col: (lambda b, s: (b, s, col))
    bmap = lambda col: (lambda b, s: (b, _bwd_order(s, ncc, nc), col))
    return pl.pallas_call(
        functools.partial(_hg_scan_kernel, layer=layer),
        grid=(bsz, nc),
        in_specs=[pl.BlockSpec(blk, fmap(0)), pl.BlockSpec(blk, fmap(1)), pl.BlockSpec(blk, fmap(3)),
                  pl.BlockSpec(blk, bmap(0)), pl.BlockSpec(blk, bmap(2)), pl.BlockSpec(blk, bmap(3)),
                  pl.BlockSpec(lb_logits.shape, c2)] + [pl.BlockSpec(c.shape, c2) for c in consts],
        out_specs=[pl.BlockSpec(blk, fmap(0)), pl.BlockSpec(blk, bmap(0))],
        out_shape=[jax.ShapeDtypeStruct((bsz, t_all, BRANCH_W), F32)] * 2,
        scratch_shapes=[pltpu.VMEM((2, BRANCH_W, BRANCH_W), F32), pltpu.VMEM((2, 4, tm, BRANCH_W), F32)],
        compiler_params=pltpu.CompilerParams(dimension_semantics=("parallel", "arbitrary"),
                                             vmem_limit_bytes=40 * 1024 * 1024),
        name="hgrn2_scan",
    )(p_hg, p_hg, p_hg, p_hg, p_hg, p_hg, lb_logits.astype(F32), *consts)


def _s5_kernel(uf_ref, ub_ref, bb_ref, cc_ref, a_ref, yf_ref, yb_ref, x_ref, h_ref, *, tt, bsz):
    @pl.when(pl.program_id(0) == 0)
    def _():
        h_ref[...] = jnp.zeros_like(h_ref)

    x_ref[0] = _dot(_bf(uf_ref[...]), bb_ref[0])
    x_ref[1] = _dot(_bf(ub_ref[...]), bb_ref[1])

    def step(i, carry):
        hfr, hfi, hbr, hbi = carry
        rf = pl.multiple_of(i * bsz, bsz)
        rb = pl.multiple_of((tt - 1 - i) * bsz, bsz)
        ar, ai = a_ref[0, 0], a_ref[0, 1]
        xr = x_ref[0, pl.ds(rf, bsz), 0:S5_LANES]
        xi = x_ref[0, pl.ds(rf, bsz), S5_LANES:2 * S5_LANES]
        nfr = ar * hfr - ai * hfi + xr
        nfi = ar * hfi + ai * hfr + xi
        x_ref[0, pl.ds(rf, bsz), 0:S5_LANES] = nfr
        x_ref[0, pl.ds(rf, bsz), S5_LANES:2 * S5_LANES] = nfi
        ar, ai = a_ref[1, 0], a_ref[1, 1]
        xr = x_ref[1, pl.ds(rb, bsz), 0:S5_LANES]
        xi = x_ref[1, pl.ds(rb, bsz), S5_LANES:2 * S5_LANES]
        nbr = ar * hbr - ai * hbi + xr
        nbi = ar * hbi + ai * hbr + xi
        x_ref[1, pl.ds(rb, bsz), 0:S5_LANES] = nbr
        x_ref[1, pl.ds(rb, bsz), S5_LANES:2 * S5_LANES] = nbi
        return nfr, nfi, nbr, nbi

    hs = lax.fori_loop(0, tt, step, (h_ref[0, 0], h_ref[0, 1], h_ref[1, 0], h_ref[1, 1]))
    h_ref[0, 0], h_ref[0, 1], h_ref[1, 0], h_ref[1, 1] = hs
    yf_ref[...] = _dot(_bf(x_ref[0]), cc_ref[0])
    yb_ref[...] = _dot(_bf(x_ref[1]), cc_ref[1])


def _s5_matrices(lam_re, lam_im, log_step, b_re, b_im, c_re, c_im):
    lam_re, lam_im = lam_re.astype(F32), lam_im.astype(F32)
    step = jnp.exp(log_step.astype(F32))[..., None]
    mag = jnp.exp(lam_re * step)
    a_re, a_im = mag * jnp.cos(lam_im * step), mag * jnp.sin(lam_im * step)
    den = lam_re * lam_re + lam_im * lam_im
    n_re = a_re - 1.0
    coef_re = (n_re * lam_re + a_im * lam_im) / den
    coef_im = (a_im * lam_re - n_re * lam_im) / den
    b_re, b_im = b_re.astype(F32), b_im.astype(F32)
    bb_re = coef_re[..., None] * b_re - coef_im[..., None] * b_im
    bb_im = coef_re[..., None] * b_im + coef_im[..., None] * b_re
    eye_g = jnp.eye(S5_GROUPS, dtype=F32)
    bb = jnp.stack([bb_re, bb_im], axis=1)
    bb = jnp.einsum('drgph,gk->dghrkp', bb, eye_g).reshape(2, BRANCH_W, 2 * S5_LANES)
    cc = jnp.stack([c_re.astype(F32), -c_im.astype(F32)], axis=0)
    cc = jnp.einsum('rgkp,gj->rjpgk', cc, eye_g).reshape(2 * S5_LANES, BRANCH_W)
    return bb.astype(BF16), cc.astype(BF16), a_re.reshape(2, S5_LANES), a_im.reshape(2, S5_LANES)


def _s5(u_tb, mats, *, bsz, tc, tt):
    bb, cc, a_re, a_im = mats
    n_rows = u_tb.shape[0]
    t_all = n_rows // bsz
    nt, nct = t_all // tt, tc // tt
    a = jnp.broadcast_to(jnp.stack([a_re, a_im], axis=1)[:, :, None, :], (2, 2, bsz, S5_LANES))
    cc2 = jnp.broadcast_to(cc[None], (2,) + cc.shape)
    blk = (tt * bsz, BRANCH_W)
    fmap = lambda s: (s, 0)
    bmap = lambda s: (_bwd_order(s, nct, nt), 0)
    return pl.pallas_call(
        functools.partial(_s5_kernel, tt=tt, bsz=bsz),
        grid=(nt,),
        in_specs=[pl.BlockSpec(blk, fmap), pl.BlockSpec(blk, bmap),
                  pl.BlockSpec(bb.shape, lambda s: (0, 0, 0)), pl.BlockSpec(cc2.shape, lambda s: (0, 0, 0)),
                  pl.BlockSpec(a.shape, lambda s: (0, 0, 0, 0))],
        out_specs=[pl.BlockSpec(blk, fmap), pl.BlockSpec(blk, bmap)],
        out_shape=[jax.ShapeDtypeStruct((n_rows, BRANCH_W), F32)] * 2,
        scratch_shapes=[pltpu.VMEM((2, tt * bsz, 2 * S5_LANES), F32), pltpu.VMEM((2, 2, bsz, S5_LANES), F32)],
        compiler_params=pltpu.CompilerParams(dimension_semantics=("arbitrary",),
                                             vmem_limit_bytes=V7X_VMEM_LIMIT),
        name="s5_scan",
    )(u_tb, u_tb, bb, cc2, a)


def _rope(x, cos, sin_signed):
    w = x.shape[-1]
    lane = lax.broadcasted_iota(jnp.int32, x.shape, 1)
    up = pltpu.roll(x, w - 16, 1)
    dn = pltpu.roll(x, 16, 1)
    return x * cos + jnp.where((lane & 31) < 16, up, dn) * sin_signed


def _attn_kernel(q_ref, kv_ref, cq_ref, sq_ref, ck_ref, sk_ref, qg_ref, kg_ref, bdq_ref, bdk_ref,
                 o_ref, kp_ref, vp_ref, *, tm, tc, nct, head_rows):
    j = pl.program_id(1)
    t_all = kv_ref.shape[0]
    half = BRANCH_W // 2

    @pl.when(j == 0)
    def _():
        def pairs(x):
            lane = lax.broadcasted_iota(jnp.int32, x.shape, 1)
            sw = pltpu.roll(x, HEAD_DIM, 1)
            return _bf(jnp.where(lane < HEAD_DIM, x, sw)), _bf(jnp.where(lane < HEAD_DIM, sw, x))

        for lo, hi, rope in ((0, tc, False), (tc, t_all, True)):
            k = _head_rms(kv_ref[lo:hi, 0:half], bdk_ref[...], kg_ref[...])
            if rope:
                k = _rope(k, ck_ref[...], sk_ref[...])
            kp_ref[0, lo:hi, :], kp_ref[1, lo:hi, :] = pairs(k)
            v = kv_ref[lo:hi, half:2 * half]
            lane = lax.broadcasted_iota(jnp.int32, v.shape, 1)
            sw = pltpu.roll(v, HEAD_DIM, 1)
            for g in range(2):
                v_lo, v_hi = (v, sw) if g == 0 else (sw, v)
                vp_ref[g, 0, lo:hi, :] = _bf(jnp.where(lane < HEAD_DIM, v_lo, 1.0))
                vp_ref[g, 1, lo:hi, :] = _bf(jnp.where(lane < HEAD_DIM, 1.0, v_hi))

    def attend(q, nk):
        lane = lax.broadcasted_iota(jnp.int32, (q.shape[0], half), 1)
        outs = []
        for g in range(2):
            qp = q[:, g * half:(g + 1) * half]
            acc = None
            for hh in range(2):
                sel = (lane < HEAD_DIM) if hh == 0 else (lane >= HEAD_DIM)
                s = _dot_nt(_bf(jnp.where(sel, qp, 0.0)), kp_ref[g, 0:nk, :])
                p = jnp.exp(_bf(s - jnp.max(s, axis=-1, keepdims=True)))
                ol = _dot(p, vp_ref[g, hh, 0:nk, :])
                res = ol / pltpu.roll(ol, HEAD_DIM, 1)
                acc = res if acc is None else jnp.where(sel, res, acc)
            outs.append(acc)
        return jnp.concatenate(outs, axis=1)

    qn = _head_rms(q_ref[...], bdq_ref[...], qg_ref[...]) * (HEAD_DIM ** -0.5)
    if head_rows:
        @pl.when(j == 0)
        def _():
            o_ref[0:head_rows, :] = attend(qn[0:head_rows], tc)
            o_ref[head_rows:tm, :] = attend(
                _rope(qn[head_rows:tm], cq_ref[head_rows:tm, :], sq_ref[head_rows:tm, :]), t_all)

        @pl.when(j > 0)
        def _():
            o_ref[...] = attend(_rope(qn, cq_ref[...], sq_ref[...]), t_all)
    else:
        @pl.when(j < nct)
        def _():
            o_ref[...] = attend(qn, tc)

        @pl.when(j >= nct)
        def _():
            o_ref[...] = attend(_rope(qn, cq_ref[...], sq_ref[...]), t_all)


def _rope_tables(n_tokens):
    rows = n_tokens // GRID_W
    r, col = np.meshgrid(np.arange(rows), np.arange(GRID_W), indexing='ij')
    axis_dim = HEAD_DIM // 2
    inv = jnp.asarray(ROPE_THETA, F32) ** (-jnp.arange(0, axis_dim, 2, dtype=F32) / axis_dim)
    ang_r = jnp.asarray(r.reshape(-1, 1), F32) * inv
    ang_c = jnp.asarray(col.reshape(-1, 1), F32) * inv
    ang = jnp.concatenate([ang_r, ang_r, ang_c, ang_c], axis=-1)
    sign = np.where((np.arange(HEAD_DIM) % 32) < 16, -1.0, 1.0).astype(np.float32)
    cos, sin = jnp.cos(ang), jnp.sin(ang) * sign
    return jnp.tile(cos, (1, N_HEADS)), jnp.tile(sin, (1, N_HEADS))


def _attention(p_q, p_kv, qn_g, kn_g, cos4, sin4, *, tm, tc):
    bsz, t_all, _ = p_q.shape
    assert t_all % tm == 0 and (tc % tm == 0 or (tc < tm and tc % 8 == 0))
    nt, nct = t_all // tm, tc // tm
    head_rows = tc if tc % tm else 0
    half = BRANCH_W // 2
    tok = lambda b, j: (b, j, 0)
    c2 = lambda b, j: (0, 0)
    lat = lambda b, j: (j, 0)
    pad = jnp.zeros((tc, BRANCH_W), F32)
    cosq, sinq = jnp.concatenate([pad, cos4], axis=0), jnp.concatenate([pad, sin4], axis=0)
    qg = jnp.tile(qn_g.astype(F32), N_HEADS).reshape(1, BRANCH_W)
    kg = jnp.tile(kn_g.astype(F32), 2).reshape(1, half)
    bdq = jnp.asarray(_np_block_diag_ones(BRANCH_W, HEAD_DIM), BF16)
    bdk = jnp.asarray(_np_block_diag_ones(half, HEAD_DIM), BF16)
    cosk, sink = cos4[:, 0:half], sin4[:, 0:half]
    return pl.pallas_call(
        functools.partial(_attn_kernel, tm=tm, tc=tc, nct=nct, head_rows=head_rows),
        grid=(bsz, nt),
        in_specs=[pl.BlockSpec((None, tm, BRANCH_W), tok),
                  pl.BlockSpec((None, t_all, BRANCH_W), lambda b, j: (b, 0, 0)),
                  pl.BlockSpec((tm, BRANCH_W), lat), pl.BlockSpec((tm, BRANCH_W), lat),
                  pl.BlockSpec(cosk.shape, c2), pl.BlockSpec(sink.shape, c2),
                  pl.BlockSpec((1, BRANCH_W), c2), pl.BlockSpec((1, half), c2),
                  pl.BlockSpec(bdq.shape, c2), pl.BlockSpec(bdk.shape, c2)],
        out_specs=pl.BlockSpec((None, tm, BRANCH_W), tok),
        out_shape=jax.ShapeDtypeStruct((bsz, t_all, BRANCH_W), F32),
        scratch_shapes=[pltpu.VMEM((2, t_all, half), BF16), pltpu.VMEM((2, 2, t_all, half), BF16)],
        compiler_params=pltpu.CompilerParams(dimension_semantics=("parallel", "arbitrary"),
                                             vmem_limit_bytes=V7X_VMEM_LIMIT),
        name="attention",
    )(p_q, p_kv, cosq, sinq, cosk, sink, qg, kg, bdq, bdk)


def _merge_kernel(x_ref, m_ref, g_ref, dnf_ref, dnb_ref, z_ref, sf_ref, sb_ref, u_ref, hf_ref, hb_ref, hgate_ref,
                  at_ref, dng_ref, hgg_ref, dsk_ref, glu_ref, bd_ref, wg_ref, wb_ref, wo_ref, o_ref):
    x = x_ref[...]
    bd = bd_ref[...]
    h = _bf(_rms(x) * g_ref[...] * (1.0 + m_ref[4:5, :]) + m_ref[3:4, :])
    y_dn = _head_rms(dnf_ref[...] + dnb_ref[...], bd, dng_ref[...]) * _silu(z_ref[...])
    ys = sf_ref[...] + sb_ref[...] + dsk_ref[...] * u_ref[...]
    ab = _dot(_bf(jax.nn.gelu(ys, approximate=True)), glu_ref[...])
    y_s5 = ab[:, 0:BRANCH_W] * _sigmoid(ab[:, BRANCH_W:2 * BRANCH_W])
    y_hg = _head_rms(hf_ref[...] + hb_ref[...], bd, hgg_ref[...]) * _sigmoid(hgate_ref[...])
    ys_all = (y_dn, y_s5, y_hg, at_ref[...])
    acc = jnp.zeros(x.shape, F32)
    for i in range(4):
        acc = acc + _sigmoid(_dot(h, wg_ref[i])) * _dot(_bf(ys_all[i]), wb_ref[i])
    o_ref[...] = x + m_ref[5:6, :] * _dot(_bf(acc), wo_ref[...])


def _merge(xj, mods, g, br, params, *, tm, nct, j0):
    bsz, t_all, _ = xj.shape
    n_tiles = t_all // tm - j0
    tok = lambda b, j: (b, j + j0, 0)
    col = lambda c: (lambda b, j: (b, j + j0, c))
    mod_map = lambda b, j: (jnp.where(j + j0 < nct, bsz, b), 0, 0)
    c2 = lambda b, j: (0, 0)
    c3 = lambda b, j: (0, 0, 0)
    tb = (None, tm, BRANCH_W)
    tmaj, tmaj_map = (tm, BRANCH_W), (lambda b, j: (j + j0, b))
    dn_g, hg_g, d_skip, w_glu, wg, wb, wo = params
    bd = jnp.asarray(_np_block_diag_ones(BRANCH_W, HEAD_DIM), BF16)
    return pl.pallas_call(
        _merge_kernel,
        grid=(bsz, n_tiles),
        in_specs=[pl.BlockSpec((None, tm, D_MODEL), tok),
                  pl.BlockSpec((None, N_MOD, D_MODEL), mod_map),
                  pl.BlockSpec((1, D_MODEL), c2),
                  pl.BlockSpec(tb, tok), pl.BlockSpec(tb, tok), pl.BlockSpec(tb, col(3)),
                  pl.BlockSpec(tmaj, tmaj_map), pl.BlockSpec(tmaj, tmaj_map), pl.BlockSpec(tmaj, tmaj_map),
                  pl.BlockSpec(tb, tok), pl.BlockSpec(tb, tok), pl.BlockSpec(tb, col(4)),
                  pl.BlockSpec(tb, tok),
                  pl.BlockSpec((1, BRANCH_W), c2), pl.BlockSpec((1, BRANCH_W), c2), pl.BlockSpec((1, BRANCH_W), c2),
                  pl.BlockSpec(w_glu.shape, c2), pl.BlockSpec(bd.shape, c2),
                  pl.BlockSpec(wg.shape, c3), pl.BlockSpec(wb.shape, c3), pl.BlockSpec(wo.shape, c2)],
        out_specs=pl.BlockSpec((None, tm, D_MODEL), lambda b, j: (b, j, 0)),
        out_shape=jax.ShapeDtypeStruct((bsz, n_tiles * tm, D_MODEL), F32),
        compiler_params=pltpu.CompilerParams(dimension_semantics=("parallel", "parallel"),
                                             vmem_limit_bytes=V7X_VMEM_LIMIT),
        name="branch_merge",
    )(xj, mods, g.reshape(1, D_MODEL), br['dn_f'], br['dn_b'], br['p_dn'], br['s5_f'], br['s5_b'], br['p_s5'],
      br['hg_f'], br['hg_b'], br['p_hg'], br['at'], dn_g, hg_g, d_skip, w_glu, bd, wg, wb, wo)


def kernel(x, c, ctx, c_ctx, ada_w, ada_b, norm_g, ffn_w1, ffn_w3, ffn_w2, w_in, dn_conv, dn_a_log, dn_dt_bias,
           dn_norm_g, s5_lam_re, s5_lam_im, s5_log_step, s5_b_re, s5_b_im, s5_c_re, s5_c_im, s5_d, s5_glu,
           hg_lb_logits, hg_norm_g, at_qn_g, at_kn_g, w_branch, w_out, final_g):
    bsz, tl, _ = x.shape
    tc = ctx.shape[1]
    t_all = tc + tl
    depth = ada_w.shape[0]
    tm = 256 if (tc % 256 == 0 and tl % 256 == 0) else 128
    assert tc % tm == 0 and tl % tm == 0 and tm % CHUNK == 0 and bsz % 8 == 0
    nct = tc // tm
    tm_joint = 768 if (t_all % 768 == 0 and tc < 768) else tm
    tm_latent = next((t for t in (1024, 512) if tl % t == 0), tm)
    d_ff = ffn_w1.shape[-1]
    n_ff = d_ff // FF_CHUNK

    n_rows = -(-(bsz + 1) // 8) * 8
    cond = jnp.zeros((n_rows, D_MODEL), F32).at[0:bsz].set(c.astype(F32)).at[bsz].set(c_ctx.astype(F32))
    mods_all = _ada_mods(cond, ada_w.astype(F32), ada_b.astype(F32))
    cos4, sin4 = _rope_tables(tl)
    xj = x.astype(F32)

    def ffn_weights(l, i):
        w1 = ffn_w1[l, i].astype(BF16).reshape(D_MODEL, n_ff, FF_CHUNK).transpose(1, 0, 2)
        w3 = ffn_w3[l, i].astype(BF16).reshape(D_MODEL, n_ff, FF_CHUNK).transpose(1, 0, 2)
        w2 = ffn_w2[l, i].astype(BF16)
        return w1, w3, w2

    for l in range(depth):
        last = l == depth - 1
        mods = mods_all[l]
        if l == 0:
            xj = _ffn_half_step(xj, mods, norm_g[l, 0], *ffn_weights(l, 0), mod_base=0, tm=tm, tc=tc,
                                ctx=ctx.astype(F32))
        else:
            xj = _ffn_half_step(xj, mods, norm_g[l, 0], *ffn_weights(l, 0), mod_base=0, tm=tm_joint, tc=tc)

        wl = w_in[l]
        w_dn = jnp.concatenate([wl[:, 0:1040], jnp.zeros((D_MODEL, 112), wl.dtype)], axis=1)
        ws = [w.astype(BF16) for w in (w_dn, wl[:, 1040:1296], wl[:, 1296:2576], wl[:, 2576:2832], wl[:, 2832:3088])]
        p_dn, p_s5, p_hg, p_aq, p_kv = _input_projection(xj, mods, norm_g[l, 1], ws, tm=tm_joint, tc=tc)

        dn_f, dn_b = _deltanet(p_dn, dn_conv[l], dn_a_log[l], dn_dt_bias[l], tm=tm, tc=tc)
        hg_f, hg_b = _hgrn2(p_hg, hg_lb_logits, l, tm=tm, tc=tc)
        mats = _s5_matrices(s5_lam_re[l], s5_lam_im[l], s5_log_step[l], s5_b_re[l], s5_b_im[l], s5_c_re[l], s5_c_im[l])
        s5_f, s5_b = _s5(p_s5.reshape(t_all * bsz, BRANCH_W), mats, bsz=bsz, tc=tc, tt=CHUNK)
        s5_f = s5_f.reshape(t_all, bsz * BRANCH_W)
        s5_b = s5_b.reshape(t_all, bsz * BRANCH_W)
        y_at = _attention(p_aq, p_kv, at_qn_g[l], at_kn_g[l], cos4, sin4, tm=tm_joint, tc=tc)

        br = dict(dn_f=dn_f, dn_b=dn_b, p_dn=p_dn, s5_f=s5_f, s5_b=s5_b, p_s5=p_s5, hg_f=hg_f, hg_b=hg_b,
                  p_hg=p_hg, at=y_at)
        params = (jnp.tile(dn_norm_g[l].astype(F32), N_HEADS).reshape(1, BRANCH_W),
                  jnp.tile(hg_norm_g[l].astype(F32), N_HEADS).reshape(1, BRANCH_W),
                  s5_d[l].astype(F32).reshape(1, BRANCH_W),
                  s5_glu[l].astype(BF16),
                  wl[:, 3088:].astype(BF16).reshape(D_MODEL, 4, D_MODEL).transpose(1, 0, 2),
                  w_branch[l].astype(BF16),
                  w_out[l].astype(BF16))
        xj = _merge(xj, mods, norm_g[l, 1], br, params, tm=tm, nct=nct, j0=nct if last else 0)
        if last:
            xj = _ffn_half_step(xj, mods, norm_g[l, 2], *ffn_weights(l, 1), mod_base=6, tm=tm_latent, tc=0,
                                final_g=final_g.astype(F32))
        else:
            xj = _ffn_half_step(xj, mods, norm_g[l, 2], *ffn_weights(l, 1), mod_base=6, tm=tm_joint, tc=tc)
    return xj.astype(x.dtype)
```

```python
import functools
import math

import numpy as np
import jax
import jax.numpy as jnp
from jax import lax
from jax.experimental import pallas as pl
from jax.experimental.pallas import tpu as pltpu

F32 = jnp.float32
BF16 = jnp.bfloat16

D_MODEL = 1024
N_MOD = 9
BRANCH_W = 256
HEAD_DIM = 64
N_HEADS = 4
CHUNK = 64
HG_BLK = 16
EPS = 1e-6
LOG_FLOOR = 1e-30
GRID_W = 64
ROPE_THETA = 10000.0
FF_CHUNK = 256
S5_GROUPS = 16
S5_GROUP = 16
S5_STATE = 64
S5_LANES = S5_GROUPS * S5_STATE
NEG_BIG = -1e30
V7X_VMEM_LIMIT = 56 * 1024 * 1024


def _bf(x):
    return x.astype(BF16)


def _dot(a, b):
    return jnp.dot(a, b, preferred_element_type=F32)


def _dot_nt(a, b):
    return lax.dot_general(a, b, (((1,), (1,)), ((), ())), preferred_element_type=F32)


def _dot_tn(a, b):
    return lax.dot_general(a, b, (((0,), (0,)), ((), ())), preferred_element_type=F32)


def _split2(x):
    hi = _bf(x)
    lo = _bf(x - hi.astype(F32))
    return hi, lo


def _split3(x):
    x1 = _bf(x)
    r1 = x - x1.astype(F32)
    x2 = _bf(r1)
    x3 = _bf(r1 - x2.astype(F32))
    return x1, x2, x3


def _dot_x2(x, m):
    hi, lo = _split2(x)
    return _dot(hi, m) + _dot(lo, m)


def _dot_x3(x, m):
    x1, x2, x3 = _split3(x)
    return _dot(x1, m) + _dot(x2, m) + _dot(x3, m)


def _dot3_left(m, x):
    x1, x2, x3 = _split3(x)
    return _dot(m, x1) + _dot(m, x2) + _dot(m, x3)


def _dot_hp(a, b):
    a_hi, a_lo = _split2(a)
    b_hi, b_lo = _split2(b)
    return _dot(a_hi, b_hi) + _dot(a_hi, b_lo) + _dot(a_lo, b_hi)


def _sigmoid(x):
    return jax.nn.sigmoid(x)


def _silu(x):
    return x * jax.nn.sigmoid(x)


def _rms(x):
    return x * lax.rsqrt(jnp.mean(x * x, axis=-1, keepdims=True) + EPS)


def _head_rms(x, bd_ones, gain):
    ss = _dot_x2(x * x, bd_ones) * (1.0 / HEAD_DIM)
    return x * lax.rsqrt(ss + EPS) * gain


def _stack4(x):
    return jnp.concatenate([x, x, x, x], axis=0)


def _unstack4(x):
    n = x.shape[0] // 4
    return x[0:n] + x[n:2 * n] + x[2 * n:3 * n] + x[3 * n:4 * n]


def _np_block_diag_ones(n, blk):
    i = np.arange(n)
    return (i[:, None] // blk == i[None, :] // blk).astype(np.float32)


def _np_tri(n, blk, lower, strict):
    i = np.arange(n)
    same = i[:, None] // blk == i[None, :] // blk
    a, b = i[:, None] % blk, i[None, :] % blk
    if lower:
        t = (a > b) if strict else (a >= b)
    else:
        t = (a < b) if strict else (a <= b)
    return (same & t).astype(np.float32)


def _np_doubling_masks(n, blk):
    i = np.arange(n)
    r, c = i[:, None], i[None, :]
    out = [(r // 2 == c // 2)]
    s = 2
    while s < blk:
        out.append((r // (2 * s) == c // (2 * s)) & (r // s != c // s))
        s *= 2
    return np.stack(out).astype(np.float32)


def _bwd_order(s, n_ctx, n_all):
    return jnp.where(s < n_ctx, n_ctx - 1 - s, n_all + n_ctx - 1 - s)


def _ada_kernel(c_ref, w_ref, b_ref, o_ref):
    s = _silu(c_ref[...])
    w = w_ref[...]
    s_hi, s_lo = _split2(s)
    w_hi, w_lo = _split2(w)
    o_ref[...] = _dot(s_hi, w_hi) + _dot(s_hi, w_lo) + _dot(s_lo, w_hi) + b_ref[...]


def _ada_mods(cond, ada_w, ada_b):
    n_layers = ada_w.shape[0]
    r = cond.shape[0]
    n_out = ada_w.shape[2]
    tn = 1152
    out = pl.pallas_call(
        _ada_kernel,
        grid=(n_layers, n_out // tn),
        in_specs=[
            pl.BlockSpec((r, D_MODEL), lambda l, n: (0, 0)),
            pl.BlockSpec((None, D_MODEL, tn), lambda l, n: (l, 0, n)),
            pl.BlockSpec((None, 1, tn), lambda l, n: (l, 0, n)),
        ],
        out_specs=pl.BlockSpec((None, r, tn), lambda l, n: (l, 0, n)),
        out_shape=jax.ShapeDtypeStruct((n_layers, r, n_out), F32),
        compiler_params=pltpu.CompilerParams(dimension_semantics=("parallel", "parallel"),
                                             vmem_limit_bytes=40 * 1024 * 1024),
        name="ada_mods",
    )(cond, ada_w, ada_b.reshape(n_layers, 1, n_out))
    return out.reshape(n_layers, r, N_MOD, D_MODEL)


def _ffn_kernel(*refs, mod_base, final, n_ctx_tiles, head_rows):
    refs = list(refs)
    o_ref = refs.pop()
    x_ref = refs.pop(0)
    c_ref = refs.pop(0) if n_ctx_tiles else None
    m_ref = refs.pop(0)
    mc_ref = refs.pop(0) if head_rows else None
    g_ref, w1_ref, w3_ref, w2_ref = refs[0:4]
    fg_ref = refs[4] if final else None
    x = x_ref[...]
    if c_ref is not None:
        x = jnp.where(pl.program_id(1) < n_ctx_tiles, c_ref[...], x)

    def mod(i, head):
        row = m_ref[mod_base + i:mod_base + i + 1, :]
        if head:
            row = jnp.where(pl.program_id(1) == 0, mc_ref[mod_base + i:mod_base + i + 1, :], row)
        return row

    def rows(fn):
        if not head_rows:
            return fn(slice(None), False)
        return jnp.concatenate([fn(slice(0, head_rows), True), fn(slice(head_rows, None), False)], axis=0)

    y = _rms(x) * g_ref[...]
    h = _bf(rows(lambda r, hd: y[r] * (1.0 + mod(1, hd)) + mod(0, hd)))
    us = []
    for c in range(w1_ref.shape[0]):
        a = _dot(h, w1_ref[c])
        us.append(_bf(_silu(a) * _dot(h, w3_ref[c])))
    acc = _dot(jnp.concatenate(us, axis=1), w2_ref[...])
    out = x + 0.5 * rows(lambda r, hd: mod(2, hd) * acc[r])
    if final:
        out = _rms(out) * fg_ref[...]
    o_ref[...] = out


def _ffn_half_step(xj, mods, g, w1c, w3c, w2c, *, mod_base, tm, tc, final_g=None, ctx=None):
    bsz, t_all, _ = xj.shape
    final = final_g is not None
    if ctx is not None:
        t_all += ctx.shape[1]
    assert t_all % tm == 0 and (tc % tm == 0 or (tc < tm and tc % 8 == 0 and ctx is None))
    nct = tc // tm
    head_rows = tc if tc % tm else 0

    def tok_map(b, j):
        return (b, j, 0)

    def mod_map(b, j):
        return (jnp.where(j < nct, bsz, b), 0, 0)

    const3 = lambda b, j: (0, 0, 0)
    const2 = lambda b, j: (0, 0)
    once = dict(pipeline_mode=pl.Buffered(1))
    if ctx is None:
        tok_specs, tok_args = [pl.BlockSpec((None, tm, D_MODEL), tok_map)], [xj]
    else:
        tok_specs = [pl.BlockSpec((None, tm, D_MODEL), lambda b, j: (b, jnp.maximum(j - nct, 0), 0)),
                     pl.BlockSpec((None, tm, D_MODEL), lambda b, j: (b, jnp.minimum(j, nct - 1), 0))]
        tok_args = [xj, ctx]
    in_specs = tok_specs + [pl.BlockSpec((None, N_MOD, D_MODEL), mod_map)]
    args = tok_args + [mods]
    if head_rows:
        in_specs.append(pl.BlockSpec((None, N_MOD, D_MODEL), lambda b, j: (bsz, 0, 0)))
        args.append(mods)
    in_specs += [
        pl.BlockSpec((1, D_MODEL), const2),
        pl.BlockSpec(w1c.shape, const3, **once),
        pl.BlockSpec(w3c.shape, const3, **once),
        pl.BlockSpec(w2c.shape, const2, **once),
    ]
    args += [g.reshape(1, D_MODEL), w1c, w3c, w2c]
    if final:
        in_specs.append(pl.BlockSpec((1, D_MODEL), const2))
        args.append(final_g.reshape(1, D_MODEL))
    return pl.pallas_call(
        functools.partial(_ffn_kernel, mod_base=mod_base, final=final, n_ctx_tiles=nct if ctx is not None else 0,
                          head_rows=head_rows),
        grid=(bsz, t_all // tm),
        in_specs=in_specs,
        out_specs=pl.BlockSpec((None, tm, D_MODEL), tok_map),
        out_shape=jax.ShapeDtypeStruct((bsz, t_all, D_MODEL), F32),
        compiler_params=pltpu.CompilerParams(dimension_semantics=("parallel", "parallel"),
                                             vmem_limit_bytes=V7X_VMEM_LIMIT),
        name="ffn_half_step",
    )(*args)


def _inproj_kernel(x_ref, m_ref, mc_ref, g_ref, wdn_ref, ws5_ref, whg_ref, waq_ref, wkv_ref,
                   dn_ref, s5_ref, hg_ref, aq_ref, kv_ref, *, head_rows):
    y = _rms(x_ref[...]) * g_ref[...]
    if head_rows:
        first = pl.program_id(1) == 0
        scale_h = jnp.where(first, mc_ref[4:5, :], m_ref[4:5, :])
        shift_h = jnp.where(first, mc_ref[3:4, :], m_ref[3:4, :])
        h = _bf(jnp.concatenate([y[0:head_rows] * (1.0 + scale_h) + shift_h,
                                 y[head_rows:] * (1.0 + m_ref[4:5, :]) + m_ref[3:4, :]], axis=0))
    else:
        h = _bf(y * (1.0 + m_ref[4:5, :]) + m_ref[3:4, :])
    dn_ref[...] = _dot(h, wdn_ref[...])
    s5_ref[...] = _dot(h, ws5_ref[...])
    hg_ref[...] = _dot(h, whg_ref[...])
    aq_ref[...] = _dot(h, waq_ref[...])
    kv_ref[...] = _dot(h, wkv_ref[...])


def _input_projection(xj, mods, g, ws, *, tm, tc):
    bsz, t_all, _ = xj.shape
    assert t_all % tm == 0 and (tc % tm == 0 or (tc < tm and tc % 8 == 0))
    nct = tc // tm
    tok_map = lambda b, j: (b, j, 0)
    mod_map = lambda b, j: (jnp.where(j < nct, bsz, b), 0, 0)
    const2 = lambda b, j: (0, 0)
    widths = [w.shape[1] for w in ws]
    return pl.pallas_call(
        functools.partial(_inproj_kernel, head_rows=tc if tc % tm else 0),
        grid=(bsz, t_all // tm),
        in_specs=[pl.BlockSpec((None, tm, D_MODEL), tok_map),
                  pl.BlockSpec((None, N_MOD, D_MODEL), mod_map),
                  pl.BlockSpec((None, N_MOD, D_MODEL), lambda b, j: (bsz, 0, 0)),
                  pl.BlockSpec((1, D_MODEL), const2)]
                 + [pl.BlockSpec(w.shape, const2, pipeline_mode=pl.Buffered(1)) for w in ws],
        out_specs=[pl.BlockSpec((tm, n), lambda b, j: (j, b)) if i == 1 else pl.BlockSpec((None, tm, n), tok_map)
                   for i, n in enumerate(widths)],
        out_shape=[jax.ShapeDtypeStruct((t_all, bsz * n) if i == 1 else (bsz, t_all, n), F32)
                   for i, n in enumerate(widths)],
        compiler_params=pltpu.CompilerParams(dimension_semantics=("parallel", "parallel"),
                                             vmem_limit_bytes=V7X_VMEM_LIMIT),
        name="input_projection",
    )(xj, mods, mods, g.reshape(1, D_MODEL), *ws)


def _dn_prep_kernel(p_ref, prev_ref, next_ref, cw_ref, al_ref, dtb_ref, e_ref, tril_ref, triu_ref, bd_ref,
                    q_ref, k_ref, v_ref, gb_ref, *, tm, nct, nt):
    j = pl.program_id(1)
    first = jnp.logical_or(j == 0, j == nct)
    last = jnp.logical_or(j == nct - 1, j == nt - 1)
    x = p_ref[:, 0:3 * BRANCH_W]
    pv = prev_ref[...] * jnp.where(first, 0.0, 1.0)
    nx = next_ref[...] * jnp.where(last, 0.0, 1.0)
    ext = jnp.concatenate([pv, x, nx], axis=0)
    n = tm + 16
    acc = jnp.zeros((tm, 3 * BRANCH_W), F32)
    for t in range(5):
        sh = (2 - t) % n
        r = ext if sh == 0 else pltpu.roll(ext, sh, 0)
        acc = acc + r[8:8 + tm] * cw_ref[t:t + 1, :]
    h = _silu(acc)
    bd = bd_ref[...]

    def l2n(t):
        return t * lax.rsqrt(_dot_x2(t * t, bd) + EPS)

    q_ref[...] = l2n(h[:, 0:BRANCH_W]) * (HEAD_DIM ** -0.5)
    k_ref[...] = l2n(h[:, BRANCH_W:2 * BRANCH_W])
    v_ref[...] = h[:, 2 * BRANCH_W:3 * BRANCH_W]

    ab = p_ref[:, 4 * BRANCH_W:4 * BRANCH_W + 128]
    z = ab + dtb_ref[...]
    softplus = jnp.maximum(z, 0.0) + jnp.log1p(jnp.exp(-jnp.abs(z)))
    g = -jnp.exp(al_ref[...]) * softplus
    lane = lax.broadcasted_iota(jnp.int32, ab.shape, 1)
    g = jnp.where(lane < 8, g, 0.0)
    cs_f = _dot3_left(tril_ref[...], g)
    cs_b = _dot3_left(triu_ref[...], g)
    comp = jnp.where(lane < 4, cs_f, jnp.where(lane < 8, cs_b, jnp.where(lane < 16, _sigmoid(ab), 0.0)))
    gb_ref[...] = _dot_x3(comp, e_ref[...])


def _dn_chunks_local(items, bd, it, lvl_ref):
    def stk(x):
        return _stack4(_bf(x)) * bd

    ks = [stk(c[1]) for c in items]
    gq = [_dot_nt(_bf(jnp.concatenate([c[1] * c[4], c[0]], axis=0)), x) for c, x in zip(items, ks)]
    decays, a_s = [], []
    for (q, k, v, gc, be, neg_incl, tri_strict, fwd), g in zip(items, gq):
        col_gc = jnp.sum(gc * it, axis=0, keepdims=True)
        decay = jnp.exp(gc - col_gc + neg_incl)
        decays.append(decay)
        a_s.append(g[0:CHUNK] * decay * tri_strict)
    ts = [it - a * lvl_ref[0] for a in a_s]
    for lv in range(1, lvl_ref.shape[0]):
        xs = [_dot(_bf(a * lvl_ref[lv]), stk(t)) for a, t in zip(a_s, ts)]
        ts = [t - _dot(_bf(t), stk(x)) for t, x in zip(ts, xs)]
    egcs = [jnp.exp(c[3]) for c in items]
    uws = [_dot(_bf(t), jnp.concatenate([stk(c[2] * c[4]), stk(c[1] * c[4] * e)], axis=1))
           for t, c, e in zip(ts, items, egcs)]
    out = []
    for (q, k, v, gc, be, _, _, fwd), uw, g, dc, egc in zip(items, uws, gq, decays, egcs):
        total = gc[CHUNK - 1:CHUNK, :] if fwd else gc[0:1, :]
        out.append((uw[:, 0:BRANCH_W], _bf(uw[:, BRANCH_W:2 * BRANCH_W]), _bf(g[CHUNK:2 * CHUNK] * dc),
                    _bf(q * egc), _bf(k * jnp.exp(total - gc)), jnp.exp(total)))
    return out


def _dn_chunk_state(local, s_prev, bd, bd_f32):
    u, w, attn, q_dec, k_dec, g_last = local
    ws = _dot(jnp.concatenate([w, q_dec], axis=0), _bf(s_prev))
    v_new = _bf(u - ws[0:CHUNK])
    o = ws[CHUNK:2 * CHUNK] + _dot(attn, _stack4(v_new) * bd)
    s_new = s_prev * g_last + _dot_tn(k_dec, v_new) * bd_f32
    return o, s_new


def _dn_scan_kernel(qf_ref, kf_ref, vf_ref, gcf_ref, bef_ref, qb_ref, kb_ref, vb_ref, gcb_ref, beb_ref,
                    bdf_ref, li_ref, ls_ref, ui_ref, us_ref, it_ref, lvl_ref, bd_ref,
                    of_ref, ob_ref, s_ref):
    @pl.when(pl.program_id(1) == 0)
    def _():
        s_ref[...] = jnp.zeros_like(s_ref)

    bd, bd_f32, it = bd_ref[...], bdf_ref[...], it_ref[...]
    n_sub = qf_ref.shape[0] // CHUNK
    dirs = ((qf_ref, kf_ref, vf_ref, gcf_ref, bef_ref, li_ref, ls_ref, of_ref, True),
            (qb_ref, kb_ref, vb_ref, gcb_ref, beb_ref, ui_ref, us_ref, ob_ref, False))
    items = []
    for q_ref, k_ref, v_ref, gc_ref, be_ref, ti_ref, ts_ref, _, fwd in dirs:
        for g in range(n_sub):
            r = slice(g * CHUNK, (g + 1) * CHUNK)
            items.append((q_ref[r, :], k_ref[r, :], v_ref[r, :], gc_ref[r, :], be_ref[r, :],
                          ti_ref[...], ts_ref[...], fwd))
    local = _dn_chunks_local(items, bd, it, lvl_ref)
    s_f, s_b = s_ref[0], s_ref[1]
    for i in range(n_sub):
        gb = n_sub - 1 - i
        o, s_f = _dn_chunk_state(local[i], s_f, bd, bd_f32)
        of_ref[i * CHUNK:(i + 1) * CHUNK, :] = o
        o, s_b = _dn_chunk_state(local[n_sub + gb], s_b, bd, bd_f32)
        ob_ref[gb * CHUNK:(gb + 1) * CHUNK, :] = o
    s_ref[0] = s_f
    s_ref[1] = s_b


def _deltanet(p_dn, conv_w, a_log, dt_bias, *, tm, tc):
    bsz, t_all, _ = p_dn.shape
    nt, nct = t_all // tm, tc // tm
    al = jnp.zeros((1, 128), F32).at[0, 0:8].set(a_log.reshape(8).astype(F32))
    dtb = jnp.zeros((1, 128), F32).at[0, 0:8].set(dt_bias.reshape(8).astype(F32))
    e = np.zeros((128, 4 * BRANCH_W), np.float32)
    for jj in range(4):
        for h in range(N_HEADS):
            e[4 * jj + h, jj * BRANCH_W + h * HEAD_DIM: jj * BRANCH_W + (h + 1) * HEAD_DIM] = 1.0
    consts = [jnp.asarray(e, BF16),
              jnp.asarray(_np_tri(tm, CHUNK, True, False), BF16),
              jnp.asarray(_np_tri(tm, CHUNK, False, False), BF16),
              jnp.asarray(_np_block_diag_ones(BRANCH_W, HEAD_DIM), BF16)]
    tok = lambda b, j: (b, j, 0)
    c2 = lambda b, j: (0, 0)
    hb = tm // 8
    q, k, v, gb = pl.pallas_call(
        functools.partial(_dn_prep_kernel, tm=tm, nct=nct, nt=nt),
        grid=(bsz, nt),
        in_specs=[pl.BlockSpec((None, tm, p_dn.shape[2]), tok),
                  pl.BlockSpec((None, 8, 3 * BRANCH_W), lambda b, j: (b, jnp.maximum(j * hb - 1, 0), 0)),
                  pl.BlockSpec((None, 8, 3 * BRANCH_W), lambda b, j: (b, jnp.minimum((j + 1) * hb, nt * hb - 1), 0)),
                  pl.BlockSpec(conv_w.shape, c2), pl.BlockSpec((1, 128), c2), pl.BlockSpec((1, 128), c2)]
                 + [pl.BlockSpec(c.shape, c2) for c in consts],
        out_specs=[pl.BlockSpec((None, tm, BRANCH_W), tok)] * 3 + [pl.BlockSpec((None, tm, 4 * BRANCH_W), tok)],
        out_shape=[jax.ShapeDtypeStruct((bsz, t_all, BRANCH_W), F32)] * 3
                  + [jax.ShapeDtypeStruct((bsz, t_all, 4 * BRANCH_W), F32)],
        compiler_params=pltpu.CompilerParams(dimension_semantics=("parallel", "parallel"),
                                             vmem_limit_bytes=40 * 1024 * 1024),
        name="deltanet_prep",
    )(p_dn, p_dn, p_dn, conv_w.astype(F32), al, dtb, *consts)

    nc, ncc = nt, nct
    n4 = N_HEADS * CHUNK
    neg = lambda tri: (1.0 - tri) * NEG_BIG
    per_head = lambda m: np.tile(m, (1,) * (m.ndim - 1) + (N_HEADS,))
    masks = [jnp.asarray(_np_block_diag_ones(n4, CHUNK), F32),
             jnp.asarray(per_head(neg(_np_tri(CHUNK, CHUNK, True, False))), F32),
             jnp.asarray(per_head(_np_tri(CHUNK, CHUNK, True, True)), F32),
             jnp.asarray(per_head(neg(_np_tri(CHUNK, CHUNK, False, False))), F32),
             jnp.asarray(per_head(_np_tri(CHUNK, CHUNK, False, True)), F32),
             jnp.asarray(per_head(np.eye(CHUNK, dtype=np.float32)), F32),
             jnp.asarray(per_head(_np_doubling_masks(CHUNK, CHUNK)), F32),
             jnp.asarray(_np_block_diag_ones(n4, CHUNK), BF16)]
    cmap = lambda m: (lambda b, s: (0,) * m.ndim)
    blk = (None, tm, BRANCH_W)
    fmap = lambda col: (lambda b, s: (b, s, col))
    bmap = lambda col: (lambda b, s: (b, _bwd_order(s, ncc, nc), col))
    o_f, o_b = pl.pallas_call(
        _dn_scan_kernel,
        grid=(bsz, nc),
        in_specs=[pl.BlockSpec(blk, fmap(0))] * 3 + [pl.BlockSpec(blk, fmap(0)), pl.BlockSpec(blk, fmap(2))]
                 + [pl.BlockSpec(blk, bmap(0))] * 3 + [pl.BlockSpec(blk, bmap(1)), pl.BlockSpec(blk, bmap(3))]
                 + [pl.BlockSpec(m.shape, cmap(m)) for m in masks],
        out_specs=[pl.BlockSpec(blk, fmap(0)), pl.BlockSpec(blk, bmap(0))],
        out_shape=[jax.ShapeDtypeStruct((bsz, t_all, BRANCH_W), F32)] * 2,
        scratch_shapes=[pltpu.VMEM((2, n4, n4), F32)],
        compiler_params=pltpu.CompilerParams(dimension_semantics=("parallel", "arbitrary"),
                                             vmem_limit_bytes=40 * 1024 * 1024),
        name="deltanet_scan",
    )(q, k, v, gb, gb, q, k, v, gb, gb, *masks)
    return o_f, o_b


def _hg_tile_prepare(q, f, v, lb, tri_incl, tri_excl, tmp_ref):
    qs = _silu(q)
    logf = jnp.log(jnp.maximum(lb + (1.0 - lb) * _sigmoid(f), LOG_FLOOR))
    kk = (1.0 - lb) * _sigmoid(-f)
    bl = _dot3_left(tri_incl, logf)
    suf = _dot3_left(tri_excl, logf)
    tmp_ref[0], tmp_ref[1], tmp_ref[2], tmp_ref[3] = qs, bl - jnp.log(kk), v, bl
    return _bf(qs * jnp.exp(bl)), _bf(kk * jnp.exp(suf)), _bf(v), bl


def _row16(row):
    r8 = jnp.broadcast_to(row, (8, row.shape[1]))
    return jnp.concatenate([r8] * (HG_BLK // 8), axis=0)


def _hg_block_pairs(tmp_ref, r0, bd_ones, fwd):
    io = lax.broadcasted_iota(jnp.int32, (HG_BLK, BRANCH_W), 0)
    q_b = _bf(tmp_ref[0, r0:r0 + HG_BLK, :])
    b_b = tmp_ref[3, r0:r0 + HG_BLK, :]
    rows = []
    for i in range(HG_BLK):
        c_s = _row16(tmp_ref[1, r0 + i:r0 + i + 1, :])
        vis = (io >= i) if fwd else (io <= i)
        rows.append(q_b * jnp.exp(_bf(jnp.where(vis, b_b - c_s, NEG_BIG))))
    se = _dot(jnp.concatenate(rows, axis=0), bd_ones)
    out = None
    for i in range(HG_BLK):
        term = se[i * HG_BLK:(i + 1) * HG_BLK] * _row16(tmp_ref[2, r0 + i:r0 + i + 1, :])
        out = term if out is None else out + term
    return out


def _hg_scan_kernel(qf_ref, ff_ref, vf_ref, qb_ref, fb_ref, vb_ref, lg_ref, bdo_ref, bdm_ref,
                    tli_ref, tls_ref, tui_ref, tus_ref, of_ref, ob_ref, st_ref, tmp_ref, *, layer):
    @pl.when(pl.program_id(1) == 0)
    def _():
        st_ref[...] = jnp.zeros_like(st_ref)

    n_layers = lg_ref.shape[0]
    mx = lg_ref[0:1, :]
    for i in range(1, n_layers):
        mx = jnp.maximum(mx, lg_ref[i:i + 1, :])
    den = jnp.zeros_like(mx)
    num = jnp.zeros_like(mx)
    for i in range(n_layers):
        e = jnp.exp(lg_ref[i:i + 1, :] - mx)
        den = den + e
        if 1 <= i <= layer:
            num = num + e
    lb = num / den
    bdo, hmask = bdo_ref[...], bdm_ref[...]

    def stk(x):
        return jnp.concatenate([x] * N_HEADS, axis=0) * hmask

    n_blk = qf_ref.shape[0] // HG_BLK
    prep = (_hg_tile_prepare(qf_ref[...], ff_ref[...], vf_ref[...], lb, tli_ref[...], tus_ref[...], tmp_ref.at[0]),
            _hg_tile_prepare(qb_ref[...], fb_ref[...], vb_ref[...], lb, tui_ref[...], tls_ref[...], tmp_ref.at[1]))
    pairs = [[_hg_block_pairs(tmp_ref.at[d], i * HG_BLK, bdo, d == 0) for i in range(n_blk)] for d in range(2)]
    st = [st_ref[0], st_ref[1]]
    for step in range(n_blk):
        for d, o_ref in ((0, of_ref), (1, ob_ref)):
            i = step if d == 0 else n_blk - 1 - step
            r = slice(i * HG_BLK, (i + 1) * HG_BLK)
            q_t, k_t, v_b, bl = prep[d]
            o_ref[r, :] = pairs[d][i] + _dot_nt(q_t[r], _bf(st[d]))
            last = (i + 1) * HG_BLK - 1 if d == 0 else i * HG_BLK
            st[d] = st[d] * jnp.exp(bl[last:last + 1, :]) + _dot_tn(stk(v_b[r]), stk(k_t[r]))
    st_ref[0], st_ref[1] = st


def _hgrn2(p_hg, lb_logits, layer, *, tm, tc):
    bsz, t_all, _ = p_hg.shape
    nc, ncc = t_all // tm, tc // tm
    head_of_row = np.arange(N_HEADS * HG_BLK)[:, None] // HG_BLK == np.arange(BRANCH_W)[None, :] // HEAD_DIM
    consts = [jnp.asarray(_np_block_diag_ones(BRANCH_W, HEAD_DIM), BF16),
              jnp.asarray(head_of_row.astype(np.float32), BF16),
              jnp.asarray(_np_tri(tm, HG_BLK, True, False), BF16),
              jnp.asarray(_np_tri(tm, HG_BLK, True, True), BF16),
              jnp.asarray(_np_tri(tm, HG_BLK, False, False), BF16),
              jnp.asarray(_np_tri(tm, HG_BLK, False, True), BF16)]
    blk = (None, tm, BRANCH_W)
    c2 = lambda b, s: (0, 0)
    fmap = lambda col: (lambda b, s: (b, s, col))
    bmap = lambda col: (lambda b, s: (b, _bwd_order(s, ncc, nc), col))
    return pl.pallas_call(
        functools.partial(_hg_scan_kernel, layer=layer),
        grid=(bsz, nc),
        in_specs=[pl.BlockSpec(blk, fmap(0)), pl.BlockSpec(blk, fmap(1)), pl.BlockSpec(blk, fmap(3)),
                  pl.BlockSpec(blk, bmap(0)), pl.BlockSpec(blk, bmap(2)), pl.BlockSpec(blk, bmap(3)),
                  pl.BlockSpec(lb_logits.shape, c2)] + [pl.BlockSpec(c.shape, c2) for c in consts],
        out_specs=[pl.BlockSpec(blk, fmap(0)), pl.BlockSpec(blk, bmap(0))],
        out_shape=[jax.ShapeDtypeStruct((bsz, t_all, BRANCH_W), F32)] * 2,
        scratch_shapes=[pltpu.VMEM((2, BRANCH_W, BRANCH_W), F32), pltpu.VMEM((2, 4, tm, BRANCH_W), F32)],
        compiler_params=pltpu.CompilerParams(dimension_semantics=("parallel", "arbitrary"),
                                             vmem_limit_bytes=40 * 1024 * 1024),
        name="hgrn2_scan",
    )(p_hg, p_hg, p_hg, p_hg, p_hg, p_hg, lb_logits.astype(F32), *consts)


def _s5_kernel(uf_ref, ub_ref, bb_ref, cc_ref, a_ref, yf_ref, yb_ref, x_ref, h_ref, *, tt, bsz):
    @pl.when(pl.program_id(0) == 0)
    def _():
        h_ref[...] = jnp.zeros_like(h_ref)

    x_ref[0] = _dot(_bf(uf_ref[...]), bb_ref[0])
    x_ref[1] = _dot(_bf(ub_ref[...]), bb_ref[1])

    def step(i, carry):
        hfr, hfi, hbr, hbi = carry
        rf = pl.multiple_of(i * bsz, bsz)
        rb = pl.multiple_of((tt - 1 - i) * bsz, bsz)
        ar, ai = a_ref[0, 0], a_ref[0, 1]
        xr = x_ref[0, pl.ds(rf, bsz), 0:S5_LANES]
        xi = x_ref[0, pl.ds(rf, bsz), S5_LANES:2 * S5_LANES]
        nfr = ar * hfr - ai * hfi + xr
        nfi = ar * hfi + ai * hfr + xi
        x_ref[0, pl.ds(rf, bsz), 0:S5_LANES] = nfr
        x_ref[0, pl.ds(rf, bsz), S5_LANES:2 * S5_LANES] = nfi
        ar, ai = a_ref[1, 0], a_ref[1, 1]
        xr = x_ref[1, pl.ds(rb, bsz), 0:S5_LANES]
        xi = x_ref[1, pl.ds(rb, bsz), S5_LANES:2 * S5_LANES]
        nbr = ar * hbr - ai * hbi + xr
        nbi = ar * hbi + ai * hbr + xi
        x_ref[1, pl.ds(rb, bsz), 0:S5_LANES] = nbr
        x_ref[1, pl.ds(rb, bsz), S5_LANES:2 * S5_LANES] = nbi
        return nfr, nfi, nbr, nbi

    hs = lax.fori_loop(0, tt, step, (h_ref[0, 0], h_ref[0, 1], h_ref[1, 0], h_ref[1, 1]))
    h_ref[0, 0], h_ref[0, 1], h_ref[1, 0], h_ref[1, 1] = hs
    yf_ref[...] = _dot(_bf(x_ref[0]), cc_ref[0])
    yb_ref[...] = _dot(_bf(x_ref[1]), cc_ref[1])


def _s5_matrices(lam_re, lam_im, log_step, b_re, b_im, c_re, c_im):
    lam_re, lam_im = lam_re.astype(F32), lam_im.astype(F32)
    step = jnp.exp(log_step.astype(F32))[..., None]
    mag = jnp.exp(lam_re * step)
    a_re, a_im = mag * jnp.cos(lam_im * step), mag * jnp.sin(lam_im * step)
    den = lam_re * lam_re + lam_im * lam_im
    n_re = a_re - 1.0
    coef_re = (n_re * lam_re + a_im * lam_im) / den
    coef_im = (a_im * lam_re - n_re * lam_im) / den
    b_re, b_im = b_re.astype(F32), b_im.astype(F32)
    bb_re = coef_re[..., None] * b_re - coef_im[..., None] * b_im
    bb_im = coef_re[..., None] * b_im + coef_im[..., None] * b_re
    eye_g = jnp.eye(S5_GROUPS, dtype=F32)
    bb = jnp.stack([bb_re, bb_im], axis=1)
    bb = jnp.einsum('drgph,gk->dghrkp', bb, eye_g).reshape(2, BRANCH_W, 2 * S5_LANES)
    cc = jnp.stack([c_re.astype(F32), -c_im.astype(F32)], axis=0)
    cc = jnp.einsum('rgkp,gj->rjpgk', cc, eye_g).reshape(2 * S5_LANES, BRANCH_W)
    return bb.astype(BF16), cc.astype(BF16), a_re.reshape(2, S5_LANES), a_im.reshape(2, S5_LANES)


def _s5(u_tb, mats, *, bsz, tc, tt):
    bb, cc, a_re, a_im = mats
    n_rows = u_tb.shape[0]
    t_all = n_rows // bsz
    nt, nct = t_all // tt, tc // tt
    a = jnp.broadcast_to(jnp.stack([a_re, a_im], axis=1)[:, :, None, :], (2, 2, bsz, S5_LANES))
    cc2 = jnp.broadcast_to(cc[None], (2,) + cc.shape)
    blk = (tt * bsz, BRANCH_W)
    fmap = lambda s: (s, 0)
    bmap = lambda s: (_bwd_order(s, nct, nt), 0)
    return pl.pallas_call(
        functools.partial(_s5_kernel, tt=tt, bsz=bsz),
        grid=(nt,),
        in_specs=[pl.BlockSpec(blk, fmap), pl.BlockSpec(blk, bmap),
                  pl.BlockSpec(bb.shape, lambda s: (0, 0, 0)), pl.BlockSpec(cc2.shape, lambda s: (0, 0, 0)),
                  pl.BlockSpec(a.shape, lambda s: (0, 0, 0, 0))],
        out_specs=[pl.BlockSpec(blk, fmap), pl.BlockSpec(blk, bmap)],
        out_shape=[jax.ShapeDtypeStruct((n_rows, BRANCH_W), F32)] * 2,
        scratch_shapes=[pltpu.VMEM((2, tt * bsz, 2 * S5_LANES), F32), pltpu.VMEM((2, 2, bsz, S5_LANES), F32)],
        compiler_params=pltpu.CompilerParams(dimension_semantics=("arbitrary",),
                                             vmem_limit_bytes=V7X_VMEM_LIMIT),
        name="s5_scan",
    )(u_tb, u_tb, bb, cc2, a)


def _rope(x, cos, sin_signed):
    w = x.shape[-1]
    lane = lax.broadcasted_iota(jnp.int32, x.shape, 1)
    up = pltpu.roll(x, w - 16, 1)
    dn = pltpu.roll(x, 16, 1)
    return x * cos + jnp.where((lane & 31) < 16, up, dn) * sin_signed


def _attn_kernel(q_ref, kv_ref, cq_ref, sq_ref, ck_ref, sk_ref, qg_ref, kg_ref, bdq_ref, bdk_ref,
                 o_ref, kp_ref, vp_ref, *, tm, tc, nct, head_rows):
    j = pl.program_id(1)
    t_all = kv_ref.shape[0]
    half = BRANCH_W // 2

    @pl.when(j == 0)
    def _():
        def pairs(x):
            lane = lax.broadcasted_iota(jnp.int32, x.shape, 1)
            sw = pltpu.roll(x, HEAD_DIM, 1)
            return _bf(jnp.where(lane < HEAD_DIM, x, sw)), _bf(jnp.where(lane < HEAD_DIM, sw, x))

        for lo, hi, rope in ((0, tc, False), (tc, t_all, True)):
            k = _head_rms(kv_ref[lo:hi, 0:half], bdk_ref[...], kg_ref[...])
            if rope:
                k = _rope(k, ck_ref[...], sk_ref[...])
            kp_ref[0, lo:hi, :], kp_ref[1, lo:hi, :] = pairs(k)
            v = kv_ref[lo:hi, half:2 * half]
            lane = lax.broadcasted_iota(jnp.int32, v.shape, 1)
            sw = pltpu.roll(v, HEAD_DIM, 1)
            for g in range(2):
                v_lo, v_hi = (v, sw) if g == 0 else (sw, v)
                vp_ref[g, 0, lo:hi, :] = _bf(jnp.where(lane < HEAD_DIM, v_lo, 1.0))
                vp_ref[g, 1, lo:hi, :] = _bf(jnp.where(lane < HEAD_DIM, 1.0, v_hi))

    def attend(q, nk):
        lane = lax.broadcasted_iota(jnp.int32, (q.shape[0], half), 1)
        outs = []
        for g in range(2):
            qp = q[:, g * half:(g + 1) * half]
            acc = None
            for hh in range(2):
                sel = (lane < HEAD_DIM) if hh == 0 else (lane >= HEAD_DIM)
                s = _dot_nt(_bf(jnp.where(sel, qp, 0.0)), kp_ref[g, 0:nk, :])
                p = jnp.exp(_bf(s - jnp.max(s, axis=-1, keepdims=True)))
                ol = _dot(p, vp_ref[g, hh, 0:nk, :])
                res = ol / pltpu.roll(ol, HEAD_DIM, 1)
                acc = res if acc is None else jnp.where(sel, res, acc)
            outs.append(acc)
        return jnp.concatenate(outs, axis=1)

    qn = _head_rms(q_ref[...], bdq_ref[...], qg_ref[...]) * (HEAD_DIM ** -0.5)
    if head_rows:
        @pl.when(j == 0)
        def _():
            o_ref[0:head_rows, :] = attend(qn[0:head_rows], tc)
            o_ref[head_rows:tm, :] = attend(
                _rope(qn[head_rows:tm], cq_ref[head_rows:tm, :], sq_ref[head_rows:tm, :]), t_all)

        @pl.when(j > 0)
        def _():
            o_ref[...] = attend(_rope(qn, cq_ref[...], sq_ref[...]), t_all)
    else:
        @pl.when(j < nct)
        def _():
            o_ref[...] = attend(qn, tc)

        @pl.when(j >= nct)
        def _():
            o_ref[...] = attend(_rope(qn, cq_ref[...], sq_ref[...]), t_all)


def _rope_tables(n_tokens):
    rows = n_tokens // GRID_W
    r, col = np.meshgrid(np.arange(rows), np.arange(GRID_W), indexing='ij')
    axis_dim = HEAD_DIM // 2
    inv = jnp.asarray(ROPE_THETA, F32) ** (-jnp.arange(0, axis_dim, 2, dtype=F32) / axis_dim)
    ang_r = jnp.asarray(r.reshape(-1, 1), F32) * inv
    ang_c = jnp.asarray(col.reshape(-1, 1), F32) * inv
    ang = jnp.concatenate([ang_r, ang_r, ang_c, ang_c], axis=-1)
    sign = np.where((np.arange(HEAD_DIM) % 32) < 16, -1.0, 1.0).astype(np.float32)
    cos, sin = jnp.cos(ang), jnp.sin(ang) * sign
    return jnp.tile(cos, (1, N_HEADS)), jnp.tile(sin, (1, N_HEADS))


def _attention(p_q, p_kv, qn_g, kn_g, cos4, sin4, *, tm, tc):
    bsz, t_all, _ = p_q.shape
    assert t_all % tm == 0 and (tc % tm == 0 or (tc < tm and tc % 8 == 0))
    nt, nct = t_all // tm, tc // tm
    head_rows = tc if tc % tm else 0
    half = BRANCH_W // 2
    tok = lambda b, j: (b, j, 0)
    c2 = lambda b, j: (0, 0)
    lat = lambda b, j: (j, 0)
    pad = jnp.zeros((tc, BRANCH_W), F32)
    cosq, sinq = jnp.concatenate([pad, cos4], axis=0), jnp.concatenate([pad, sin4], axis=0)
    qg = jnp.tile(qn_g.astype(F32), N_HEADS).reshape(1, BRANCH_W)
    kg = jnp.tile(kn_g.astype(F32), 2).reshape(1, half)
    bdq = jnp.asarray(_np_block_diag_ones(BRANCH_W, HEAD_DIM), BF16)
    bdk = jnp.asarray(_np_block_diag_ones(half, HEAD_DIM), BF16)
    cosk, sink = cos4[:, 0:half], sin4[:, 0:half]
    return pl.pallas_call(
        functools.partial(_attn_kernel, tm=tm, tc=tc, nct=nct, head_rows=head_rows),
        grid=(bsz, nt),
        in_specs=[pl.BlockSpec((None, tm, BRANCH_W), tok),
                  pl.BlockSpec((None, t_all, BRANCH_W), lambda b, j: (b, 0, 0)),
                  pl.BlockSpec((tm, BRANCH_W), lat), pl.BlockSpec((tm, BRANCH_W), lat),
                  pl.BlockSpec(cosk.shape, c2), pl.BlockSpec(sink.shape, c2),
                  pl.BlockSpec((1, BRANCH_W), c2), pl.BlockSpec((1, half), c2),
                  pl.BlockSpec(bdq.shape, c2), pl.BlockSpec(bdk.shape, c2)],
        out_specs=pl.BlockSpec((None, tm, BRANCH_W), tok),
        out_shape=jax.ShapeDtypeStruct((bsz, t_all, BRANCH_W), F32),
        scratch_shapes=[pltpu.VMEM((2, t_all, half), BF16), pltpu.VMEM((2, 2, t_all, half), BF16)],
        compiler_params=pltpu.CompilerParams(dimension_semantics=("parallel", "arbitrary"),
                                             vmem_limit_bytes=V7X_VMEM_LIMIT),
        name="attention",
    )(p_q, p_kv, cosq, sinq, cosk, sink, qg, kg, bdq, bdk)


def _merge_kernel(x_ref, m_ref, mc_ref, g_ref, dnf_ref, dnb_ref, z_ref, sf_ref, sb_ref, u_ref, hf_ref, hb_ref,
                  hgate_ref, at_ref, dng_ref, hgg_ref, dsk_ref, glu_ref, bd_ref, wg_ref, wb_ref, wo_ref, o_ref,
                  *, head_rows):
    x = x_ref[...]
    bd = bd_ref[...]

    def mod(i, head):
        row = m_ref[i:i + 1, :]
        return jnp.where(pl.program_id(1) == 0, mc_ref[i:i + 1, :], row) if head else row

    def rows(fn):
        if not head_rows:
            return fn(slice(None), False)
        return jnp.concatenate([fn(slice(0, head_rows), True), fn(slice(head_rows, None), False)], axis=0)

    y = _rms(x) * g_ref[...]
    h = _bf(rows(lambda r, hd: y[r] * (1.0 + mod(4, hd)) + mod(3, hd)))
    y_dn = _head_rms(dnf_ref[...] + dnb_ref[...], bd, dng_ref[...]) * _silu(z_ref[...])
    ys = sf_ref[...] + sb_ref[...] + dsk_ref[...] * u_ref[...]
    ab = _dot(_bf(jax.nn.gelu(ys, approximate=True)), glu_ref[...])
    y_s5 = ab[:, 0:BRANCH_W] * _sigmoid(ab[:, BRANCH_W:2 * BRANCH_W])
    y_hg = _head_rms(hf_ref[...] + hb_ref[...], bd, hgg_ref[...]) * _sigmoid(hgate_ref[...])
    ys_all = (y_dn, y_s5, y_hg, at_ref[...])
    acc = jnp.zeros(x.shape, F32)
    for i in range(4):
        acc = acc + _sigmoid(_dot(h, wg_ref[i])) * _dot(_bf(ys_all[i]), wb_ref[i])
    mixed = _dot(_bf(acc), wo_ref[...])
    o_ref[...] = x + rows(lambda r, hd: mod(5, hd) * mixed[r])


def _merge(xj, mods, g, br, params, *, tm, tc, j0):
    bsz, t_all, _ = xj.shape
    assert t_all % tm == 0 and (tc % tm == 0 or (j0 == 0 and tc < tm and tc % 8 == 0))
    nct = tc // tm
    head_rows = tc if tc % tm else 0
    n_tiles = t_all // tm - j0
    tok = lambda b, j: (b, j + j0, 0)
    col = lambda c: (lambda b, j: (b, j + j0, c))
    mod_map = lambda b, j: (jnp.where(j + j0 < nct, bsz, b), 0, 0)
    c2 = lambda b, j: (0, 0)
    c3 = lambda b, j: (0, 0, 0)
    tb = (None, tm, BRANCH_W)
    tmaj, tmaj_map = (tm, BRANCH_W), (lambda b, j: (j + j0, b))
    dn_g, hg_g, d_skip, w_glu, wg, wb, wo = params
    bd = jnp.asarray(_np_block_diag_ones(BRANCH_W, HEAD_DIM), BF16)
    return pl.pallas_call(
        functools.partial(_merge_kernel, head_rows=head_rows),
        grid=(bsz, n_tiles),
        in_specs=[pl.BlockSpec((None, tm, D_MODEL), tok),
                  pl.BlockSpec((None, N_MOD, D_MODEL), mod_map),
                  pl.BlockSpec((None, N_MOD, D_MODEL), lambda b, j: (bsz, 0, 0)),
                  pl.BlockSpec((1, D_MODEL), c2),
                  pl.BlockSpec(tb, tok), pl.BlockSpec(tb, tok), pl.BlockSpec(tb, col(3)),
                  pl.BlockSpec(tmaj, tmaj_map), pl.BlockSpec(tmaj, tmaj_map), pl.BlockSpec(tmaj, tmaj_map),
                  pl.BlockSpec(tb, tok), pl.BlockSpec(tb, tok), pl.BlockSpec(tb, col(4)),
                  pl.BlockSpec(tb, tok),
                  pl.BlockSpec((1, BRANCH_W), c2), pl.BlockSpec((1, BRANCH_W), c2), pl.BlockSpec((1, BRANCH_W), c2),
                  pl.BlockSpec(w_glu.shape, c2), pl.BlockSpec(bd.shape, c2),
                  pl.BlockSpec(wg.shape, c3, pipeline_mode=pl.Buffered(1)),
                  pl.BlockSpec(wb.shape, c3, pipeline_mode=pl.Buffered(1)),
                  pl.BlockSpec(wo.shape, c2, pipeline_mode=pl.Buffered(1))],
        out_specs=pl.BlockSpec((None, tm, D_MODEL), lambda b, j: (b, j, 0)),
        out_shape=jax.ShapeDtypeStruct((bsz, n_tiles * tm, D_MODEL), F32),
        compiler_params=pltpu.CompilerParams(dimension_semantics=("parallel", "parallel"),
                                             vmem_limit_bytes=V7X_VMEM_LIMIT),
        name="branch_merge",
    )(xj, mods, mods, g.reshape(1, D_MODEL), br['dn_f'], br['dn_b'], br['p_dn'], br['s5_f'], br['s5_b'], br['p_s5'],
      br['hg_f'], br['hg_b'], br['p_hg'], br['at'], dn_g, hg_g, d_skip, w_glu, bd, wg, wb, wo)


def kernel(x, c, ctx, c_ctx, ada_w, ada_b, norm_g, ffn_w1, ffn_w3, ffn_w2, w_in, dn_conv, dn_a_log, dn_dt_bias,
           dn_norm_g, s5_lam_re, s5_lam_im, s5_log_step, s5_b_re, s5_b_im, s5_c_re, s5_c_im, s5_d, s5_glu,
           hg_lb_logits, hg_norm_g, at_qn_g, at_kn_g, w_branch, w_out, final_g):
    bsz, tl, _ = x.shape
    tc = ctx.shape[1]
    t_all = tc + tl
    depth = ada_w.shape[0]
    tm = 256 if (tc % 256 == 0 and tl % 256 == 0) else 128
    assert tc % tm == 0 and tl % tm == 0 and tm % CHUNK == 0 and bsz % 8 == 0
    nct = tc // tm
    tm_joint = 768 if (t_all % 768 == 0 and tc < 768) else tm
    tm_latent = next((t for t in (1024, 512) if tl % t == 0), tm)
    d_ff = ffn_w1.shape[-1]
    n_ff = d_ff // FF_CHUNK

    n_rows = -(-(bsz + 1) // 8) * 8
    cond = jnp.zeros((n_rows, D_MODEL), F32).at[0:bsz].set(c.astype(F32)).at[bsz].set(c_ctx.astype(F32))
    mods_all = _ada_mods(cond, ada_w.astype(F32), ada_b.astype(F32))
    cos4, sin4 = _rope_tables(tl)
    xj = x.astype(F32)

    def ffn_weights(l, i):
        w1 = ffn_w1[l, i].astype(BF16).reshape(D_MODEL, n_ff, FF_CHUNK).transpose(1, 0, 2)
        w3 = ffn_w3[l, i].astype(BF16).reshape(D_MODEL, n_ff, FF_CHUNK).transpose(1, 0, 2)
        w2 = ffn_w2[l, i].astype(BF16)
        return w1, w3, w2

    for l in range(depth):
        last = l == depth - 1
        mods = mods_all[l]
        if l == 0:
            xj = _ffn_half_step(xj, mods, norm_g[l, 0], *ffn_weights(l, 0), mod_base=0, tm=tm, tc=tc,
                                ctx=ctx.astype(F32))
        else:
            xj = _ffn_half_step(xj, mods, norm_g[l, 0], *ffn_weights(l, 0), mod_base=0, tm=tm_joint, tc=tc)

        wl = w_in[l]
        w_dn = jnp.concatenate([wl[:, 0:1040], jnp.zeros((D_MODEL, 112), wl.dtype)], axis=1)
        ws = [w.astype(BF16) for w in (w_dn, wl[:, 1040:1296], wl[:, 1296:2576], wl[:, 2576:2832], wl[:, 2832:3088])]
        p_dn, p_s5, p_hg, p_aq, p_kv = _input_projection(xj, mods, norm_g[l, 1], ws, tm=tm_joint, tc=tc)

        dn_f, dn_b = _deltanet(p_dn, dn_conv[l], dn_a_log[l], dn_dt_bias[l], tm=tm, tc=tc)
        hg_f, hg_b = _hgrn2(p_hg, hg_lb_logits, l, tm=tm, tc=tc)
        mats = _s5_matrices(s5_lam_re[l], s5_lam_im[l], s5_log_step[l], s5_b_re[l], s5_b_im[l], s5_c_re[l], s5_c_im[l])
        s5_f, s5_b = _s5(p_s5.reshape(t_all * bsz, BRANCH_W), mats, bsz=bsz, tc=tc, tt=CHUNK)
        s5_f = s5_f.reshape(t_all, bsz * BRANCH_W)
        s5_b = s5_b.reshape(t_all, bsz * BRANCH_W)
        y_at = _attention(p_aq, p_kv, at_qn_g[l], at_kn_g[l], cos4, sin4, tm=tm_joint, tc=tc)

        br = dict(dn_f=dn_f, dn_b=dn_b, p_dn=p_dn, s5_f=s5_f, s5_b=s5_b, p_s5=p_s5, hg_f=hg_f, hg_b=hg_b,
                  p_hg=p_hg, at=y_at)
        params = (jnp.tile(dn_norm_g[l].astype(F32), N_HEADS).reshape(1, BRANCH_W),
                  jnp.tile(hg_norm_g[l].astype(F32), N_HEADS).reshape(1, BRANCH_W),
                  s5_d[l].astype(F32).reshape(1, BRANCH_W),
                  s5_glu[l].astype(BF16),
                  wl[:, 3088:].astype(BF16).reshape(D_MODEL, 4, D_MODEL).transpose(1, 0, 2),
                  w_branch[l].astype(BF16),
                  w_out[l].astype(BF16))
        if last:
            xj = _merge(xj, mods, norm_g[l, 1], br, params, tm=tm, tc=tc, j0=nct)
        else:
            xj = _merge(xj, mods, norm_g[l, 1], br, params, tm=tm_joint, tc=tc, j0=0)
        if last:
            xj = _ffn_half_step(xj, mods, norm_g[l, 2], *ffn_weights(l, 1), mod_base=6, tm=tm_latent, tc=0,
                                final_g=final_g.astype(F32))
        else:
            xj = _ffn_half_step(xj, mods, norm_g[l, 2], *ffn_weights(l, 1), mod_base=6, tm=tm_joint, tc=tc)
    return xj.astype(x.dtype)
```
